```python
import math
import jax, jax.numpy as jnp
from jax import lax
import numpy as np

D_MODEL = 1024
BATCH = 8
SEQ = 4096
DEPTH = 4

N_META = 16
Q_BLOCK = 128
ROPE_THETA = 500000.0
RMS_EPS = 1e-6
NEG_INF = -1e30

DIFF_HEADS = D_MODEL // 256
DIFF_DH = 64
DIFF_VD = 2 * DIFF_DH
DIFF_WIDTH = DIFF_HEADS * DIFF_VD
FOX_HEADS = D_MODEL // 128
FOX_DH = 64
FOX_WIDTH = FOX_HEADS * FOX_DH
D_FF = 2816
N_EXPERTS = 8
TOP_K = 2
D_FF_EXPERT = 3584
N_DENSE = (DEPTH + 1) // 2
N_MOE = DEPTH // 2

IN_SPLIT_SIZES = (DIFF_HEADS * 2 * DIFF_DH, DIFF_HEADS * 2 * DIFF_DH, DIFF_WIDTH,
                  FOX_WIDTH, FOX_WIDTH, FOX_WIDTH, FOX_HEADS, D_MODEL, D_MODEL)
IN_COLS = sum(IN_SPLIT_SIZES)

kernel_name = 'hybrid_diffattn_fox_moe_block'


def rmsnorm(x, g):
    xf = x.astype(jnp.float32)
    y = xf * lax.rsqrt(jnp.mean(xf * xf, axis=-1, keepdims=True) + RMS_EPS)
    return (y * g.astype(jnp.float32)).astype(x.dtype)


def rope_tables(L, dh):
    rd = dh // 4
    inv = ROPE_THETA ** (-jnp.arange(0, rd, 2, dtype=jnp.float32) / rd)
    ang = jnp.arange(L, dtype=jnp.float32)[:, None] * inv[None, :]
    return jnp.cos(ang), jnp.sin(ang)


def rope_partial(x, cos, sin):
    rd = x.shape[-1] // 4
    half = rd // 2
    shape = (1, cos.shape[0]) + (1,) * (x.ndim - 3) + (half,)
    c = cos.reshape(shape).astype(x.dtype)
    s = sin.reshape(shape).astype(x.dtype)
    x1 = x[..., :half]
    x2 = x[..., half:rd]
    return jnp.concatenate([x1 * c - x2 * s, x2 * c + x1 * s, x[..., rd:]], axis=-1)


def query_blocks(L):
    n_real = (L - N_META) // Q_BLOCK
    return [(0, N_META)] + [(N_META + i * Q_BLOCK, N_META + (i + 1) * Q_BLOCK) for i in range(n_real)]


def causal_mask(qs, qe):
    t = jnp.arange(qs, qe)[:, None]
    s = jnp.arange(qe)[None, :]
    return s <= t


def diff_attention(q, k, v, lam, lam_init, subln_g):
    B, L = q.shape[0], q.shape[1]
    scale = DIFF_DH ** -0.5
    q1, q2 = q[..., 0, :], q[..., 1, :]
    k1, k2 = k[..., 0, :], k[..., 1, :]
    outs = []
    for qs, qe in query_blocks(L):
        mask = causal_mask(qs, qe)
        s1 = jnp.einsum('bqhd,bkhd->bhqk', q1[:, qs:qe], k1[:, :qe]).astype(jnp.float32) * scale
        s2 = jnp.einsum('bqhd,bkhd->bhqk', q2[:, qs:qe], k2[:, :qe]).astype(jnp.float32) * scale
        p1 = jax.nn.softmax(jnp.where(mask, s1, NEG_INF), axis=-1)
        p2 = jax.nn.softmax(jnp.where(mask, s2, NEG_INF), axis=-1)
        a = (p1 - lam * p2).astype(v.dtype)
        outs.append(jnp.einsum('bhqk,bkhd->bqhd', a, v[:, :qe]))
    o = jnp.concatenate(outs, axis=1)
    o = rmsnorm(o, subln_g) * (1.0 - lam_init)
    return o.reshape(B, L, DIFF_WIDTH)


def forgetting_attention(q, k, v, f_logit, b_f):
    B, L = q.shape[0], q.shape[1]
    scale = FOX_DH ** -0.5
    log_f = jax.nn.log_sigmoid(f_logit.astype(jnp.float32) + b_f.astype(jnp.float32))
    c = jnp.transpose(lax.cumsum(log_f, axis=1), (0, 2, 1))
    outs = []
    for qs, qe in query_blocks(L):
        mask = causal_mask(qs, qe)
        s = jnp.einsum('bqhd,bkhd->bhqk', q[:, qs:qe], k[:, :qe]).astype(jnp.float32) * scale
        s = s + c[:, :, qs:qe, None] - c[:, :, None, :qe]
        p = jax.nn.softmax(jnp.where(mask, s, NEG_INF), axis=-1).astype(v.dtype)
        outs.append(jnp.einsum('bhqk,bkhd->bqhd', p, v[:, :qe]))
    o = jnp.concatenate(outs, axis=1)
    return o.reshape(B, L, FOX_WIDTH)


def split_cols(z):
    points = []
    acc = 0
    for size in IN_SPLIT_SIZES[:-1]:
        acc += size
        points.append(acc)
    return jnp.split(z, points, axis=-1)


def mixer(hn, w_in, b_forget, diff_lambda, diff_subln_g, w_branch_diff, w_branch_fox, w_out, cos, sin, lam_init):
    B, L, _ = hn.shape
    z = hn @ w_in
    dq, dk, dv, fq, fk, fv, ff, ga, gb = split_cols(z)
    dq = rope_partial(dq.reshape(B, L, DIFF_HEADS, 2, DIFF_DH), cos, sin)
    dk = rope_partial(dk.reshape(B, L, DIFF_HEADS, 2, DIFF_DH), cos, sin)
    dv = dv.reshape(B, L, DIFF_HEADS, DIFF_VD)
    lp = diff_lambda.astype(jnp.float32)
    lam = jnp.exp(jnp.sum(lp[0] * lp[1])) - jnp.exp(jnp.sum(lp[2] * lp[3])) + lam_init
    o_a = diff_attention(dq, dk, dv, lam, lam_init, diff_subln_g)
    o_b = forgetting_attention(fq.reshape(B, L, FOX_HEADS, FOX_DH),
                               fk.reshape(B, L, FOX_HEADS, FOX_DH),
                               fv.reshape(B, L, FOX_HEADS, FOX_DH), ff, b_forget)
    merged = jax.nn.sigmoid(ga) * (o_a @ w_branch_diff) + jax.nn.sigmoid(gb) * (o_b @ w_branch_fox)
    return merged @ w_out


def swiglu(h, wg, wu, wd):
    return (jax.nn.silu(h @ wg) * (h @ wu)) @ wd


def moe_ffn(h, router, wg, wu, wd):
    logits = jnp.einsum('bld,de->ble', h, router).astype(jnp.float32)
    top_v, top_i = lax.top_k(logits, TOP_K)
    w = jax.nn.softmax(top_v, axis=-1)
    comb = jnp.sum(jax.nn.one_hot(top_i, N_EXPERTS, dtype=jnp.float32) * w[..., None], axis=-2)
    comb = comb.astype(h.dtype)
    y = jnp.zeros_like(h)
    for e in range(N_EXPERTS):
        y = y + comb[..., e:e + 1] * swiglu(h, wg[e], wu[e], wd[e])
    return y


def setup_inputs(seed: int = 0) -> dict:
    key = jax.random.key(seed)
    ks = jax.random.split(key, 20)
    f32 = jnp.float32

    def nrm(k, shape, scale):
        return jax.random.normal(k, shape, f32) * scale

    return {
        'x': nrm(ks[0], (BATCH, SEQ, D_MODEL), 1.0),
        'meta_tokens': nrm(ks[1], (N_META, D_MODEL), 1.0),
        'norm_mix_g': 1.0 + nrm(ks[2], (DEPTH, D_MODEL), 0.02),
        'w_in': nrm(ks[3], (DEPTH, D_MODEL, IN_COLS), D_MODEL ** -0.5),
        'b_forget': jax.random.uniform(ks[4], (DEPTH, FOX_HEADS), f32, 1.0, 6.0),
        'diff_lambda': nrm(ks[5], (DEPTH, 4, DIFF_DH), 0.1),
        'diff_subln_g': 1.0 + nrm(ks[6], (DEPTH, DIFF_VD), 0.02),
        'w_branch_diff': nrm(ks[7], (DEPTH, DIFF_WIDTH, D_MODEL), DIFF_WIDTH ** -0.5),
        'w_branch_fox': nrm(ks[8], (DEPTH, FOX_WIDTH, D_MODEL), FOX_WIDTH ** -0.5),
        'w_out': nrm(ks[9], (DEPTH, D_MODEL, D_MODEL), D_MODEL ** -0.5),
        'norm_ffn_g': 1.0 + nrm(ks[10], (DEPTH, D_MODEL), 0.02),
        'ffn_w_gate': nrm(ks[11], (N_DENSE, D_MODEL, D_FF), D_MODEL ** -0.5),
        'ffn_w_up': nrm(ks[12], (N_DENSE, D_MODEL, D_FF), D_MODEL ** -0.5),
        'ffn_w_down': nrm(ks[13], (N_DENSE, D_FF, D_MODEL), D_FF ** -0.5),
        'moe_router': nrm(ks[14], (N_MOE, D_MODEL, N_EXPERTS), D_MODEL ** -0.5),
        'moe_w_gate': nrm(ks[15], (N_MOE, N_EXPERTS, D_MODEL, D_FF_EXPERT), D_MODEL ** -0.5),
        'moe_w_up': nrm(ks[16], (N_MOE, N_EXPERTS, D_MODEL, D_FF_EXPERT), D_MODEL ** -0.5),
        'moe_w_down': nrm(ks[17], (N_MOE, N_EXPERTS, D_FF_EXPERT, D_MODEL), D_FF_EXPERT ** -0.5),
        'final_norm_g': 1.0 + nrm(ks[18], (D_MODEL,), 0.02),
    }


def reference(x, meta_tokens, norm_mix_g, w_in, b_forget, diff_lambda, diff_subln_g, w_branch_diff,
              w_branch_fox, w_out, norm_ffn_g, ffn_w_gate, ffn_w_up, ffn_w_down, moe_router,
              moe_w_gate, moe_w_up, moe_w_down, final_norm_g):
    B = x.shape[0]
    meta = jnp.broadcast_to(meta_tokens[None].astype(x.dtype), (B, N_META, D_MODEL))
    h = jnp.concatenate([meta, x], axis=1)
    L = h.shape[1]
    cos, sin = rope_tables(L, DIFF_DH)
    for layer in range(DEPTH):
        lam_init = 0.8 - 0.6 * math.exp(-0.3 * layer)
        hn = rmsnorm(h, norm_mix_g[layer])
        h = h + mixer(hn, w_in[layer], b_forget[layer], diff_lambda[layer], diff_subln_g[layer],
                      w_branch_diff[layer], w_branch_fox[layer], w_out[layer], cos, sin, lam_init)
        hn = rmsnorm(h, norm_ffn_g[layer])
        j = layer // 2
        if layer % 2 == 0:
            h = h + swiglu(hn, ffn_w_gate[j], ffn_w_up[j], ffn_w_down[j])
        else:
            h = h + moe_ffn(hn, moe_router[j], moe_w_gate[j], moe_w_up[j], moe_w_down[j])
    return rmsnorm(h, final_norm_g)[:, N_META:]
```

```python
import functools
import math

import jax
import jax.numpy as jnp
from jax import lax
from jax.experimental import pallas as pl
from jax.experimental.pallas import tpu as pltpu

D_MODEL = 1024
N_META = 16
ROPE_THETA = 500000.0
RMS_EPS = 1e-6
NEG_INF = -1e30

DIFF_HEADS = 4
DIFF_DH = 64
FOX_HEADS = 8
FOX_DH = 64
HALF = 512
N_EXPERTS = 8
LANES = 128

TM = 512
TME = 512
TF_MOE = 512
TF_DENSE = 256
META_KEYS_BLOCK = 128
VMEM_LIMIT = 56 * 1024 * 1024

F32 = jnp.float32
BF16 = jnp.bfloat16


def _cparams(sem):
    return pltpu.CompilerParams(dimension_semantics=sem, vmem_limit_bytes=VMEM_LIMIT)


def _rms(x, g):
    ms = jnp.mean(x * x, axis=-1, keepdims=True)
    return x * lax.rsqrt(ms + RMS_EPS) * g


def _split3(x):
    hi = x.astype(BF16).astype(F32)
    r = x - hi
    mid = r.astype(BF16).astype(F32)
    lo = (r - mid).astype(BF16).astype(F32)
    return hi, mid, lo


def _tri_cumsum(x, inclusive):
    n = x.shape[0]
    row = lax.broadcasted_iota(jnp.int32, (n, n), 0)
    col = lax.broadcasted_iota(jnp.int32, (n, n), 1)
    tri = jnp.where((col <= row) if inclusive else (col < row), 1.0, 0.0).astype(BF16)
    out = jnp.zeros(x.shape, F32)
    for part in _split3(x):
        out = out + jnp.dot(tri, part.astype(BF16), preferred_element_type=F32)
    return out


def _inproj_kernel(h_ref, g_ref, w1_ref, w2_ref, w3_ref, w4_ref, bf_ref, rope_ref,
                   dq_ref, dk_ref, dv_ref, fv_ref, fq_ref, fk_ref, carry_ref, mcarry_ref, *, nq):
    i = pl.program_id(0)

    @pl.when(i == 0)
    def _():
        carry_ref[...] = jnp.zeros_like(carry_ref)
        mcarry_ref[...] = jnp.zeros_like(mcarry_ref)

    hb = _rms(h_ref[...], g_ref[...]).astype(BF16)

    z1 = jnp.dot(hb, w1_ref[...], preferred_element_type=F32)
    cos_t = rope_ref[:, 0:LANES]
    sin_lo = rope_ref[:, LANES:2 * LANES]
    sin_hi = rope_ref[:, 2 * LANES:3 * LANES]
    for j in range(8):
        zj = z1[:, LANES * j:LANES * (j + 1)]
        rot = zj * cos_t + pltpu.roll(zj, LANES - 8, 1) * sin_lo + pltpu.roll(zj, 8, 1) * sin_hi
        dst = dq_ref if j < 4 else dk_ref
        dst[:, LANES * (j % 4):LANES * (j % 4 + 1)] = rot.astype(BF16)

    z2 = jnp.dot(hb, w2_ref[...], preferred_element_type=F32)
    dv_ref[...] = z2[:, :HALF].astype(BF16)
    fv_ref[...] = z2[:, HALF:].astype(BF16)

    z4 = jnp.dot(hb, w4_ref[...], preferred_element_type=F32) + bf_ref[...]
    lane = lax.broadcasted_iota(jnp.int32, z4.shape, 1)
    logf = jnp.minimum(z4, 0.0) - jnp.log1p(jnp.exp(-jnp.abs(z4)))
    logf = jnp.where(lane < FOX_HEADS, logf, 0.0)
    j_in_batch = lax.rem(jnp.maximum(i - 1, 0), nq)
    base = jnp.where(i == 0, 0.0, jnp.where(j_in_batch == 0, mcarry_ref[...], carry_ref[...]))
    c = _tri_cumsum(logf, inclusive=True) + base

    @pl.when(i == 0)
    def _():
        mcarry_ref[...] = c[N_META - 1:N_META, :]

    carry_ref[...] = c[TM - 1:TM, :]

    z3 = jnp.dot(hb, w3_ref[...], preferred_element_type=F32)
    parts = _split3(c)
    for hd in range(FOX_HEADS):
        off = 64 if hd % 2 == 0 else 0
        cols = [p[:, hd:hd + 1] for p in parts]
        one_mask = (lane >= off + 3) & (lane < off + 6)
        aug_q = jnp.where(lane == off, cols[0], jnp.where(lane == off + 1, cols[1],
                          jnp.where(lane == off + 2, cols[2], jnp.where(one_mask, 1.0, 0.0))))
        aug_k = jnp.where(lane == off + 3, -cols[0], jnp.where(lane == off + 4, -cols[1],
                          jnp.where(lane == off + 5, -cols[2],
                                    jnp.where((lane >= off) & (lane < off + 3), 1.0, 0.0))))
        slab = hd // 2
        keep = (lane < 64) if hd % 2 == 0 else (lane >= 64)
        zq = z3[:, LANES * slab:LANES * (slab + 1)]
        zk = z3[:, HALF + LANES * slab:HALF + LANES * (slab + 1)]
        fq_ref[:, LANES * hd:LANES * (hd + 1)] = jnp.where(keep, zq, aug_q).astype(BF16)
        fk_ref[:, LANES * hd:LANES * (hd + 1)] = jnp.where(keep, zk, aug_k).astype(BF16)


def _inproj(h, g, w1, w2, w3, w4, bfp, rope, *, nq):
    R = h.shape[0]
    nt = R // TM
    rows = lambda i: (jnp.where(i == 0, nt - 1, i - 1), 0)
    rope_rows = lambda i: (jnp.where(i == 0, nq, lax.rem(jnp.maximum(i - 1, 0), nq)), 0)
    const = lambda i: (0, 0)
    out_sd = lambda w: jax.ShapeDtypeStruct((R, w), BF16)
    return pl.pallas_call(
        functools.partial(_inproj_kernel, nq=nq),
        grid=(nt,),
        in_specs=[
            pl.BlockSpec((TM, D_MODEL), rows),
            pl.BlockSpec((1, D_MODEL), const),
            pl.BlockSpec((D_MODEL, 2 * HALF), const),
            pl.BlockSpec((D_MODEL, 2 * HALF), const),
            pl.BlockSpec((D_MODEL, 2 * HALF), const),
            pl.BlockSpec((D_MODEL, LANES), const),
            pl.BlockSpec((1, LANES), const),
            pl.BlockSpec((TM, 3 * LANES), rope_rows),
        ],
        out_specs=[
            pl.BlockSpec((TM, HALF), rows), pl.BlockSpec((TM, HALF), rows),
            pl.BlockSpec((TM, HALF), rows), pl.BlockSpec((TM, HALF), rows),
            pl.BlockSpec((TM, 2 * HALF), rows), pl.BlockSpec((TM, 2 * HALF), rows),
        ],
        out_shape=[out_sd(HALF), out_sd(HALF), out_sd(HALF), out_sd(HALF), out_sd(2 * HALF), out_sd(2 * HALF)],
        scratch_shapes=[pltpu.VMEM((1, LANES), F32), pltpu.VMEM((1, LANES), F32)],
        compiler_params=_cparams(("arbitrary",)),
        name="inproj",
    )(h, g, w1, w2, w3, w4, bfp, rope)


def _attn_kernel(*refs, diff, nq, nb):
    if diff:
        q_ref, k_ref, v_ref, km_ref, vm_ref, par_ref, o_ref, acc_ref, m_ref = refs
    else:
        q_ref, k_ref, v_ref, km_ref, vm_ref, o_ref, acc_ref, m_ref = refs
    t = pl.program_id(1)
    is_real = t < nb * nq
    j = lax.rem(t, nq)

    def q_of(sub):
        if diff:
            q = q_ref[...]
            lane = lax.broadcasted_iota(jnp.int32, q.shape, 1)
            return jnp.where((lane < 64) if sub == 0 else (lane >= 64), q, jnp.zeros_like(q))
        return q_ref[:, LANES * sub:LANES * (sub + 1)]

    def kv_of(sub, k_tile, v_tile):
        lane_v = lax.broadcasted_iota(jnp.int32, v_tile.shape, 1)
        if diff:
            ones_slab = jnp.where(lane_v == 0, 1.0, 0.0).astype(BF16)
            return k_tile, jnp.concatenate([v_tile, ones_slab], axis=1)
        k = k_tile[:, LANES * sub:LANES * (sub + 1)]
        if sub == 0:
            v = jnp.where(lane_v < 64, v_tile, jnp.where(lane_v == 64, 1.0, 0.0).astype(BF16))
        else:
            v = jnp.where(lane_v >= 64, v_tile, jnp.where(lane_v == 0, 1.0, 0.0).astype(BF16))
        return k, v

    def step(sub, q, k, v, mask):
        s = lax.dot_general(q, k, (((1,), (1,)), ((), ())), preferred_element_type=F32)
        if mask is not None:
            s = jnp.where(mask, s, NEG_INF)
        m_prev = m_ref[sub]
        m_new = jnp.maximum(m_prev, jnp.max(s, axis=-1, keepdims=True))
        p = jnp.exp(s - m_new)
        acc_ref[sub] = jnp.exp(m_prev - m_new) * acc_ref[sub] + jnp.dot(p.astype(BF16), v,
                                                                        preferred_element_type=F32)
        m_ref[sub] = m_new

    row = lax.broadcasted_iota(jnp.int32, (TM, TM), 0)
    col = lax.broadcasted_iota(jnp.int32, (TM, TM), 1)
    causal = col <= row
    rowm = lax.broadcasted_iota(jnp.int32, (TM, META_KEYS_BLOCK), 0)
    colm = lax.broadcasted_iota(jnp.int32, (TM, META_KEYS_BLOCK), 1)
    meta_mask = colm <= jnp.where(is_real, N_META - 1, jnp.minimum(rowm, N_META - 1))

    for sub in range(2):
        q = q_of(sub)
        m_ref[sub] = jnp.full(m_ref.shape[1:], NEG_INF, F32)
        acc_ref[sub] = jnp.zeros(acc_ref.shape[1:], F32)
        k, v = kv_of(sub, km_ref[...], vm_ref[...])
        step(sub, q, k, v, meta_mask)

        def body(kk, carry, sub=sub, q=q):
            r0 = pl.multiple_of(kk * TM, TM)
            k, v = kv_of(sub, k_ref[pl.ds(r0, TM), :], v_ref[pl.ds(r0, TM), :])
            step(sub, q, k, v, None)
            return carry

        lax.fori_loop(0, jnp.where(is_real, j, 0), body, 0)

        @pl.when(is_real)
        def _(sub=sub, q=q):
            r0 = pl.multiple_of(j * TM, TM)
            k, v = kv_of(sub, k_ref[pl.ds(r0, TM), :], v_ref[pl.ds(r0, TM), :])
            step(sub, q, k, v, causal)

    if diff:
        a0 = acc_ref[0]
        a1 = acc_ref[1]
        o1 = a0[:, :LANES] / a0[:, LANES:LANES + 1]
        o2 = a1[:, :LANES] / a1[:, LANES:LANES + 1]
        d = o1 - par_ref[0:1, :] * o2
        o_ref[...] = (_rms(d, par_ref[1:2, :]) * par_ref[2:3, :]).astype(BF16)
    else:
        a0 = acc_ref[0]
        a1 = acc_ref[1]
        lane = lax.broadcasted_iota(jnp.int32, a0.shape, 1)
        o_ref[...] = jnp.where(lane < 64, a0 / a0[:, 64:65], a1 / a1[:, 0:1]).astype(BF16)


def _attention(q, k, v, par, *, diff, nq, nb):
    R = q.shape[0]
    S = nq * TM
    nt = R // TM
    qw = LANES if diff else 2 * LANES
    meta_blk = (nb * S) // META_KEYS_BLOCK
    batch_of = lambda t: jnp.minimum(t // nq, nb - 1)
    in_specs = [
        pl.BlockSpec((TM, qw), lambda p, t: (t, p)),
        pl.BlockSpec((S, qw), lambda p, t: (batch_of(t), p)),
        pl.BlockSpec((S, LANES), lambda p, t: (batch_of(t), p)),
        pl.BlockSpec((META_KEYS_BLOCK, qw), lambda p, t: (meta_blk, p)),
        pl.BlockSpec((META_KEYS_BLOCK, LANES), lambda p, t: (meta_blk, p)),
    ]
    args = [q, k, v, k, v]
    if diff:
        in_specs.append(pl.BlockSpec((8, LANES), lambda p, t: (0, 0)))
        args.append(par)
    aw = 2 * LANES if diff else LANES
    return pl.pallas_call(
        functools.partial(_attn_kernel, diff=diff, nq=nq, nb=nb),
        grid=(4, nt),
        in_specs=in_specs,
        out_specs=pl.BlockSpec((TM, LANES), lambda p, t: (t, p)),
        out_shape=jax.ShapeDtypeStruct((R, HALF), BF16),
        scratch_shapes=[pltpu.VMEM((2, TM, aw), F32), pltpu.VMEM((2, TM, 1), F32)],
        compiler_params=_cparams(("arbitrary", "arbitrary")),
        name="diff_attn" if diff else "fox_attn",
    )(*args)


def _mixout_kernel(h_ref, oa_ref, ob_ref, g_ref, wg_ref, wbd_ref, wbf_ref, wo_ref, out_ref):
    x = h_ref[...]
    hb = _rms(x, g_ref[...]).astype(BF16)
    gates = jax.nn.sigmoid(jnp.dot(hb, wg_ref[...], preferred_element_type=F32))
    a = jnp.dot(oa_ref[...], wbd_ref[...], preferred_element_type=F32)
    b = jnp.dot(ob_ref[...], wbf_ref[...], preferred_element_type=F32)
    merged = gates[:, :D_MODEL] * a + gates[:, D_MODEL:] * b
    out_ref[...] = x + jnp.dot(merged.astype(BF16), wo_ref[...], preferred_element_type=F32)


def _mixout(h, oa, ob, g, wg, wbd, wbf, wo):
    R = h.shape[0]
    rows = lambda i: (i, 0)
    const = lambda i: (0, 0)
    return pl.pallas_call(
        _mixout_kernel,
        grid=(R // TM,),
        in_specs=[
            pl.BlockSpec((TM, D_MODEL), rows), pl.BlockSpec((TM, HALF), rows), pl.BlockSpec((TM, HALF), rows),
            pl.BlockSpec((1, D_MODEL), const), pl.BlockSpec((D_MODEL, 2 * D_MODEL), const),
            pl.BlockSpec((HALF, D_MODEL), const), pl.BlockSpec((HALF, D_MODEL), const),
            pl.BlockSpec((D_MODEL, D_MODEL), const),
        ],
        out_specs=pl.BlockSpec((TM, D_MODEL), rows),
        out_shape=jax.ShapeDtypeStruct((R, D_MODEL), F32),
        compiler_params=_cparams(("arbitrary",)),
        name="mixout",
    )(h, oa, ob, g, wg, wbd, wbf, wo)


def _dense_ffn_kernel(h_ref, g_ref, wg_ref, wu_ref, wd_ref, out_ref, *, nf):
    x = h_ref[...]
    hb = _rms(x, g_ref[...]).astype(BF16)
    acc = x
    for c in range(nf):
        sl = slice(TF_DENSE * c, TF_DENSE * (c + 1))
        gate = jnp.dot(hb, wg_ref[:, sl], preferred_element_type=F32)
        up = jnp.dot(hb, wu_ref[:, sl], preferred_element_type=F32)
        mid = (gate * jax.nn.sigmoid(gate) * up).astype(BF16)
        acc = acc + jnp.dot(mid, wd_ref[sl, :], preferred_element_type=F32)
    out_ref[...] = acc


def _dense_ffn(h, g, wg, wu, wd):
    R = h.shape[0]
    dff = wg.shape[1]
    rows = lambda i: (i, 0)
    const = lambda i: (0, 0)
    return pl.pallas_call(
        functools.partial(_dense_ffn_kernel, nf=dff // TF_DENSE),
        grid=(R // TM,),
        in_specs=[
            pl.BlockSpec((TM, D_MODEL), rows), pl.BlockSpec((1, D_MODEL), const),
            pl.BlockSpec((D_MODEL, dff), const), pl.BlockSpec((D_MODEL, dff), const),
            pl.BlockSpec((dff, D_MODEL), const),
        ],
        out_specs=pl.BlockSpec((TM, D_MODEL), rows),
        out_shape=jax.ShapeDtypeStruct((R, D_MODEL), F32),
        compiler_params=_cparams(("arbitrary",)),
        name="dense_ffn",
    )(h, g, wg, wu, wd)


def _route_kernel(h_ref, g_ref, wr_ref, hn_ref, info_ref, cnt_ref, carry_ref):
    i = pl.program_id(0)

    @pl.when(i == 0)
    def _():
        carry_ref[...] = jnp.zeros_like(carry_ref)

    hn = _rms(h_ref[...], g_ref[...])
    hn_ref[...] = hn
    h_hi, h_mid, _ = _split3(hn)
    logits = (jnp.dot(h_hi.astype(BF16), wr_ref[0], preferred_element_type=F32)
              + jnp.dot(h_mid.astype(BF16), wr_ref[0], preferred_element_type=F32)
              + jnp.dot(h_hi.astype(BF16), wr_ref[1], preferred_element_type=F32))
    lane = lax.broadcasted_iota(jnp.int32, logits.shape, 1)
    logits = jnp.where(lane < N_EXPERTS, logits, -jnp.inf)
    v1 = jnp.max(logits, axis=-1, keepdims=True)
    e1 = jnp.min(jnp.where(logits == v1, lane, LANES), axis=-1, keepdims=True)
    rest = jnp.where(lane == e1, -jnp.inf, logits)
    v2 = jnp.max(rest, axis=-1, keepdims=True)
    e2 = jnp.min(jnp.where(rest == v2, lane, LANES), axis=-1, keepdims=True)
    ex = jnp.exp(v2 - v1)
    w1 = 1.0 / (1.0 + ex)
    w2 = ex / (1.0 + ex)
    hot1 = jnp.where(lane == e1, 1.0, 0.0)
    hot2 = jnp.where(lane == e2, 1.0, 0.0)
    hot = hot1 + hot2
    before = _tri_cumsum(hot, inclusive=False) + carry_ref[...]
    r1 = jnp.sum(before * hot1, axis=-1, keepdims=True)
    r2 = jnp.sum(before * hot2, axis=-1, keepdims=True)
    total = before[TM - 1:TM, :] + hot[TM - 1:TM, :]
    carry_ref[...] = total
    cnt_ref[...] = jnp.broadcast_to(total, cnt_ref.shape)
    info_ref[...] = jnp.where(lane == 0, e1.astype(F32), jnp.where(lane == 1, e2.astype(F32),
                              jnp.where(lane == 2, r1, jnp.where(lane == 3, r2,
                                        jnp.where(lane == 4, w1, jnp.where(lane == 5, w2, 0.0))))))


def _route(h, g, wr):
    R = h.shape[0]
    rows = lambda i: (i, 0)
    return pl.pallas_call(
        _route_kernel,
        grid=(R // TM,),
        in_specs=[pl.BlockSpec((TM, D_MODEL), rows), pl.BlockSpec((1, D_MODEL), lambda i: (0, 0)),
                  pl.BlockSpec((2, D_MODEL, LANES), lambda i: (0, 0, 0))],
        out_specs=[pl.BlockSpec((TM, D_MODEL), rows), pl.BlockSpec((TM, LANES), rows),
                   pl.BlockSpec((8, LANES), lambda i: (0, 0))],
        out_shape=[jax.ShapeDtypeStruct((R, D_MODEL), F32), jax.ShapeDtypeStruct((R, LANES), F32),
                   jax.ShapeDtypeStruct((8, LANES), F32)],
        scratch_shapes=[pltpu.VMEM((1, LANES), F32)],
        compiler_params=_cparams(("arbitrary",)),
        name="route",
    )(h, g, wr)


def _scatter_kernel(dest_ref, src_ref, init_ref, out_ref, sem):
    del init_ref
    i = pl.program_id(0)

    def copy(r, k):
        return pltpu.make_async_copy(src_ref.at[pl.ds(i * TM + r, 1)],
                                     out_ref.at[pl.ds(dest_ref[0, 0, 2 * r + k], 1)], sem)

    def issue(r, c):
        copy(r, 0).start()
        copy(r, 1).start()
        return c

    def drain(r, c):
        copy(r, 0).wait()
        copy(r, 1).wait()
        return c

    lax.fori_loop(0, TM, issue, 0)
    lax.fori_loop(0, TM, drain, 0)


def _scatter_rows(dest, src, init):
    R = src.shape[0]
    return pl.pallas_call(
        _scatter_kernel,
        grid=(R // TM,),
        in_specs=[pl.BlockSpec((1, 1, 2 * TM), lambda i: (i, 0, 0), memory_space=pltpu.SMEM),
                  pl.BlockSpec(memory_space=pl.ANY), pl.BlockSpec(memory_space=pl.ANY)],
        out_specs=pl.BlockSpec(memory_space=pl.ANY),
        out_shape=jax.ShapeDtypeStruct(init.shape, init.dtype),
        scratch_shapes=[pltpu.SemaphoreType.DMA(())],
        input_output_aliases={2: 0},
        compiler_params=pltpu.CompilerParams(dimension_semantics=("arbitrary",), has_side_effects=True),
        name="scatter_rows",
    )(dest, src, init)


def _expert_kernel(te_ref, act_ref, x_ref, wg_ref, wu_ref, wd_ref, y_ref):
    i = pl.program_id(0)
    f = pl.program_id(1)
    del te_ref

    @pl.when(f == 0)
    def _():
        y_ref[...] = jnp.zeros_like(y_ref)

    @pl.when(act_ref[i] > 0)
    def _():
        xb = x_ref[...].astype(BF16)
        gate = jnp.dot(xb, wg_ref[...], preferred_element_type=F32)
        up = jnp.dot(xb, wu_ref[...], preferred_element_type=F32)
        mid = (gate * jax.nn.sigmoid(gate) * up).astype(BF16)
        y_ref[...] += jnp.dot(mid, wd_ref[...], preferred_element_type=F32)


def _experts(tile_expert, tile_active, xs, wg, wu, wd):
    mt = tile_expert.shape[0]
    dffe = wg.shape[2]
    grid_spec = pltpu.PrefetchScalarGridSpec(
        num_scalar_prefetch=2,
        grid=(mt, dffe // TF_MOE),
        in_specs=[
            pl.BlockSpec((TME, D_MODEL), lambda i, f, te, act: (i, 0)),
            pl.BlockSpec((None, D_MODEL, TF_MOE), lambda i, f, te, act: (te[i], 0, f)),
            pl.BlockSpec((None, D_MODEL, TF_MOE), lambda i, f, te, act: (te[i], 0, f)),
            pl.BlockSpec((None, TF_MOE, D_MODEL), lambda i, f, te, act: (te[i], f, 0)),
        ],
        out_specs=pl.BlockSpec((TME, D_MODEL), lambda i, f, te, act: (i, 0)),
    )
    return pl.pallas_call(
        _expert_kernel,
        grid_spec=grid_spec,
        out_shape=jax.ShapeDtypeStruct((mt * TME, D_MODEL), F32),
        compiler_params=_cparams(("arbitrary", "arbitrary")),
        name="experts",
    )(tile_expert, tile_active, xs, wg, wu, wd)


def _combine_kernel(dest_ref, h_ref, info_ref, y_ref, out_ref, buf_ref, sem):
    def copy(r, k):
        return pltpu.make_async_copy(y_ref.at[pl.ds(dest_ref[0, 0, 2 * r + k], 1)],
                                     buf_ref.at[k, pl.ds(r, 1)], sem)

    def issue(r, c):
        copy(r, 0).start()
        copy(r, 1).start()
        return c

    def drain(r, c):
        copy(r, 0).wait()
        copy(r, 1).wait()
        return c

    lax.fori_loop(0, TM, issue, 0)
    lax.fori_loop(0, TM, drain, 0)
    info = info_ref[...]
    out_ref[...] = h_ref[...] + info[:, 4:5] * buf_ref[0] + info[:, 5:6] * buf_ref[1]


def _combine(dest, h, info, y):
    R = h.shape[0]
    rows = lambda i: (i, 0)
    return pl.pallas_call(
        _combine_kernel,
        grid=(R // TM,),
        in_specs=[pl.BlockSpec((1, 1, 2 * TM), lambda i: (i, 0, 0), memory_space=pltpu.SMEM),
                  pl.BlockSpec((TM, D_MODEL), rows), pl.BlockSpec((TM, LANES), rows),
                  pl.BlockSpec(memory_space=pl.ANY)],
        out_specs=pl.BlockSpec((TM, D_MODEL), rows),
        out_shape=jax.ShapeDtypeStruct((R, D_MODEL), F32),
        scratch_shapes=[pltpu.VMEM((2, TM, D_MODEL), F32), pltpu.SemaphoreType.DMA(())],
        compiler_params=_cparams(("arbitrary",)),
        name="combine",
    )(dest, h, info, y)


def _moe_ffn(h, g, wr, wg, wu, wd):
    R = h.shape[0]
    hn, info, counts = _route(h, g, wr)
    cnt = counts[0, :N_EXPERTS].astype(jnp.int32)
    padded = ((cnt + TME - 1) // TME) * TME
    ends = jnp.cumsum(padded)
    starts = ends - padded
    mt = (2 * R) // TME + N_EXPERTS
    tile_row = jnp.arange(mt, dtype=jnp.int32) * TME
    tile_expert = jnp.minimum(jnp.searchsorted(ends, tile_row, side="right"), N_EXPERTS - 1).astype(jnp.int32)
    tile_active = (tile_row < ends[-1]).astype(jnp.int32)
    e12 = info[:, 0:2].astype(jnp.int32)
    dest = (starts[e12] + info[:, 2:4].astype(jnp.int32)).reshape(R // TM, 1, 2 * TM)
    xs = _scatter_rows(dest, hn, jnp.zeros((mt * TME, D_MODEL), F32))
    y = _experts(tile_expert, tile_active, xs, wg, wu, wd)
    return _combine(dest, h, info, y)


def _final_kernel(h_ref, g_ref, out_ref):
    out_ref[...] = _rms(h_ref[...], g_ref[...])


def _final_norm(h, g, n_rows):
    rows = lambda i: (i, 0)
    return pl.pallas_call(
        _final_kernel,
        grid=(n_rows // TM,),
        in_specs=[pl.BlockSpec((TM, D_MODEL), rows), pl.BlockSpec((1, D_MODEL), lambda i: (0, 0))],
        out_specs=pl.BlockSpec((TM, D_MODEL), rows),
        out_shape=jax.ShapeDtypeStruct((n_rows, D_MODEL), F32),
        compiler_params=_cparams(("arbitrary",)),
        name="final_norm",
    )(h, g)


def _rope_table(S):
    rd = DIFF_DH // 4
    inv = ROPE_THETA ** (-jnp.arange(0, rd, 2, dtype=F32) / rd)
    pos = jnp.concatenate([jnp.arange(N_META, N_META + S, dtype=F32), jnp.arange(TM, dtype=F32)])
    ang = pos[:, None] * inv[None, :]
    cos, sin = jnp.cos(ang), jnp.sin(ang)
    n = pos.shape[0]
    cos_t = jnp.tile(jnp.concatenate([cos, cos, jnp.ones((n, 48), F32)], axis=1), (1, 2))
    sin_lo = jnp.tile(jnp.concatenate([-sin, jnp.zeros((n, 56), F32)], axis=1), (1, 2))
    sin_hi = jnp.tile(jnp.concatenate([jnp.zeros((n, 8), F32), sin, jnp.zeros((n, 48), F32)], axis=1), (1, 2))
    return jnp.concatenate([cos_t, sin_lo, sin_hi], axis=1)


def kernel(x, meta_tokens, norm_mix_g, w_in, b_forget, diff_lambda, diff_subln_g, w_branch_diff, w_branch_fox,
           w_out, norm_ffn_g, ffn_w_gate, ffn_w_up, ffn_w_down, moe_router, moe_w_gate, moe_w_up, moe_w_down,
           final_norm_g):
    B, S, D = x.shape
    depth = w_in.shape[0]
    assert D == D_MODEL and S % TM == 0 and meta_tokens.shape[0] == N_META
    nq = S // TM
    h = jnp.concatenate([x.reshape(B * S, D), meta_tokens.astype(x.dtype),
                         jnp.zeros((TM - N_META, D), x.dtype)], axis=0)
    rope = _rope_table(S)
    scale = DIFF_DH ** -0.5
    for layer in range(depth):
        lam_init = 0.8 - 0.6 * math.exp(-0.3 * layer)
        w = w_in[layer]
        dq, dk, dv, fq, fk, fv, ff, ga, gb = jnp.split(
            w, [HALF, 2 * HALF, 3 * HALF, 4 * HALF, 5 * HALF, 6 * HALF, 6 * HALF + FOX_HEADS,
                6 * HALF + FOX_HEADS + D_MODEL], axis=1)
        w1 = jnp.concatenate([dq * scale, dk], axis=1).astype(BF16)
        w2 = jnp.concatenate([dv, fv], axis=1).astype(BF16)
        w3 = jnp.concatenate([fq * scale, fk], axis=1).astype(BF16)
        w4 = jnp.pad(ff, ((0, 0), (0, LANES - FOX_HEADS))).astype(BF16)
        bfp = jnp.pad(b_forget[layer].astype(F32), (0, LANES - FOX_HEADS)).reshape(1, LANES)
        g_mix = norm_mix_g[layer].astype(F32).reshape(1, D)
        dq_a, dk_a, dv_a, fv_a, fq_a, fk_a = _inproj(h, g_mix, w1, w2, w3, w4, bfp, rope, nq=nq)

        lp = diff_lambda[layer].astype(F32)
        lam = jnp.exp(jnp.sum(lp[0] * lp[1])) - jnp.exp(jnp.sum(lp[2] * lp[3])) + lam_init
        par = jnp.zeros((8, LANES), F32)
        par = par.at[0].set(lam).at[1].set(diff_subln_g[layer].astype(F32)).at[2].set(1.0 - lam_init)
        o_a = _attention(dq_a, dk_a, dv_a, par, diff=True, nq=nq, nb=B)
        o_b = _attention(fq_a, fk_a, fv_a, None, diff=False, nq=nq, nb=B)

        wgate = jnp.concatenate([ga, gb], axis=1).astype(BF16)
        h = _mixout(h, o_a, o_b, g_mix, wgate, w_branch_diff[layer].astype(BF16),
                    w_branch_fox[layer].astype(BF16), w_out[layer].astype(BF16))

        g_ffn = norm_ffn_g[layer].astype(F32).reshape(1, D)
        jj = layer // 2
        if layer % 2 == 0:
            h = _dense_ffn(h, g_ffn, ffn_w_gate[jj].astype(BF16), ffn_w_up[jj].astype(BF16),
                           ffn_w_down[jj].astype(BF16))
        else:
            r_hi, r_mid, _ = _split3(jnp.pad(moe_router[jj].astype(F32), ((0, 0), (0, LANES - N_EXPERTS))))
            wr = jnp.stack([r_hi, r_mid]).astype(BF16)
            h = _moe_ffn(h, g_ffn, wr, moe_w_gate[jj].astype(BF16), moe_w_up[jj].astype(BF16),
                         moe_w_down[jj].astype(BF16))
    out = _final_norm(h, final_norm_g.astype(F32).reshape(1, D), B * S)
    return out.reshape(B, S, D)
```

```python
import functools
import math

import jax
import jax.numpy as jnp
from jax import lax
from jax.experimental import pallas as pl
from jax.experimental.pallas import tpu as pltpu

D_MODEL = 1024
N_META = 16
ROPE_THETA = 500000.0
RMS_EPS = 1e-6
NEG_INF = -1e30

DIFF_HEADS = 4
DIFF_DH = 64
FOX_HEADS = 8
FOX_DH = 64
HALF = 512
N_EXPERTS = 8
LANES = 128

TM = 512
TME = 512
TF_MOE = 512
TF_DENSE = 256
META_KEYS_BLOCK = 128
ONES_ROWS = 16
VMEM_LIMIT = 56 * 1024 * 1024

F32 = jnp.float32
BF16 = jnp.bfloat16


def _cparams(sem):
    return pltpu.CompilerParams(dimension_semantics=sem, vmem_limit_bytes=VMEM_LIMIT)


def _rms(x, g):
    ms = jnp.mean(x * x, axis=-1, keepdims=True)
    return x * lax.rsqrt(ms + RMS_EPS) * g


def _split3(x):
    hi = x.astype(BF16).astype(F32)
    r = x - hi
    mid = r.astype(BF16).astype(F32)
    lo = (r - mid).astype(BF16).astype(F32)
    return hi, mid, lo


def _tri_cumsum(x, inclusive):
    n = x.shape[0]
    row = lax.broadcasted_iota(jnp.int32, (n, n), 0)
    col = lax.broadcasted_iota(jnp.int32, (n, n), 1)
    tri = jnp.where((col <= row) if inclusive else (col < row), 1.0, 0.0).astype(BF16)
    out = jnp.zeros(x.shape, F32)
    for part in _split3(x):
        out = out + jnp.dot(tri, part.astype(BF16), preferred_element_type=F32)
    return out


def _inproj_kernel(h_ref, g_ref, w1_ref, w2_ref, w3_ref, w4_ref, bf_ref, rope_ref,
                   dq_ref, dk_ref, vt_ref, fq_ref, fk_ref, carry_ref, mcarry_ref, *, nq):
    i = pl.program_id(0)

    @pl.when(i == 0)
    def _():
        carry_ref[...] = jnp.zeros_like(carry_ref)
        mcarry_ref[...] = jnp.zeros_like(mcarry_ref)

    hb = _rms(h_ref[...], g_ref[...]).astype(BF16)

    z1 = jnp.dot(hb, w1_ref[...], preferred_element_type=F32)
    cos_t = rope_ref[:, 0:LANES]
    sin_lo = rope_ref[:, LANES:2 * LANES]
    sin_hi = rope_ref[:, 2 * LANES:3 * LANES]
    for j in range(8):
        zj = z1[:, LANES * j:LANES * (j + 1)]
        rot = zj * cos_t + pltpu.roll(zj, LANES - 8, 1) * sin_lo + pltpu.roll(zj, 8, 1) * sin_hi
        dst = dq_ref if j < 4 else dk_ref
        dst[:, LANES * (j % 4):LANES * (j % 4 + 1)] = rot.astype(BF16)

    vt_ref[0] = lax.dot_general(w2_ref[...], hb, (((1,), (1,)), ((), ())),
                                preferred_element_type=F32).astype(BF16)

    z4 = jnp.dot(hb, w4_ref[...], preferred_element_type=F32) + bf_ref[...]
    lane = lax.broadcasted_iota(jnp.int32, z4.shape, 1)
    logf = jnp.minimum(z4, 0.0) - jnp.log1p(jnp.exp(-jnp.abs(z4)))
    logf = jnp.where(lane < FOX_HEADS, logf, 0.0)
    j_in_batch = lax.rem(jnp.maximum(i - 1, 0), nq)
    base = jnp.where(i == 0, 0.0, jnp.where(j_in_batch == 0, mcarry_ref[...], carry_ref[...]))
    c = _tri_cumsum(logf, inclusive=True) + base

    @pl.when(i == 0)
    def _():
        mcarry_ref[...] = c[N_META - 1:N_META, :]

    carry_ref[...] = c[TM - 1:TM, :]

    z3 = jnp.dot(hb, w3_ref[...], preferred_element_type=F32)
    parts = _split3(c)
    for hd in range(FOX_HEADS):
        off = 64 if hd % 2 == 0 else 0
        cols = [p[:, hd:hd + 1] for p in parts]
        one_mask = (lane >= off + 3) & (lane < off + 6)
        aug_q = jnp.where(lane == off, cols[0], jnp.where(lane == off + 1, cols[1],
                          jnp.where(lane == off + 2, cols[2], jnp.where(one_mask, 1.0, 0.0))))
        aug_k = jnp.where(lane == off + 3, -cols[0], jnp.where(lane == off + 4, -cols[1],
                          jnp.where(lane == off + 5, -cols[2],
                                    jnp.where((lane >= off) & (lane < off + 3), 1.0, 0.0))))
        slab = hd // 2
        keep = (lane < 64) if hd % 2 == 0 else (lane >= 64)
        zq = z3[:, LANES * slab:LANES * (slab + 1)]
        zk = z3[:, HALF + LANES * slab:HALF + LANES * (slab + 1)]
        fq_ref[:, LANES * hd:LANES * (hd + 1)] = jnp.where(keep, zq, aug_q).astype(BF16)
        fk_ref[:, LANES * hd:LANES * (hd + 1)] = jnp.where(keep, zk, aug_k).astype(BF16)


def _inproj(h, g, w1, w2, w3, w4, bfp, rope, *, nq):
    R = h.shape[0]
    nt = R // TM
    rows = lambda i: (jnp.where(i == 0, nt - 1, i - 1), 0)
    rope_rows = lambda i: (jnp.where(i == 0, nq, lax.rem(jnp.maximum(i - 1, 0), nq)), 0)
    const = lambda i: (0, 0)
    out_sd = lambda w: jax.ShapeDtypeStruct((R, w), BF16)
    return pl.pallas_call(
        functools.partial(_inproj_kernel, nq=nq),
        grid=(nt,),
        in_specs=[
            pl.BlockSpec((TM, D_MODEL), rows),
            pl.BlockSpec((1, D_MODEL), const),
            pl.BlockSpec((D_MODEL, 2 * HALF), const),
            pl.BlockSpec((2 * HALF, D_MODEL), const),
            pl.BlockSpec((D_MODEL, 2 * HALF), const),
            pl.BlockSpec((D_MODEL, LANES), const),
            pl.BlockSpec((1, LANES), const),
            pl.BlockSpec((TM, 3 * LANES), rope_rows),
        ],
        out_specs=[
            pl.BlockSpec((TM, HALF), rows), pl.BlockSpec((TM, HALF), rows),
            pl.BlockSpec((1, 2 * HALF, TM), lambda i: (rows(i)[0], 0, 0)),
            pl.BlockSpec((TM, 2 * HALF), rows), pl.BlockSpec((TM, 2 * HALF), rows),
        ],
        out_shape=[out_sd(HALF), out_sd(HALF), jax.ShapeDtypeStruct((nt, 2 * HALF, TM), BF16),
                   out_sd(2 * HALF), out_sd(2 * HALF)],
        scratch_shapes=[pltpu.VMEM((1, LANES), F32), pltpu.VMEM((1, LANES), F32)],
        compiler_params=_cparams(("arbitrary",)),
        name="inproj",
    )(h, g, w1, w2, w3, w4, bfp, rope)


def _attn_kernel(*refs, diff, nq, nb):
    if diff:
        q_ref, k_ref, vt_ref, km_ref, vtm_ref, par_ref, o_ref, acc_ref, m_ref, q_scr, sa_ref, sb_ref = refs
    else:
        q_ref, k_ref, vt_ref, km_ref, vtm_ref, o_ref, acc_ref, m_ref, q_scr, sa_ref, sb_ref = refs
    t = pl.program_id(1)
    is_real = t < nb * nq
    j = lax.rem(t, nq)
    dv = acc_ref.shape[1] - ONES_ROWS

    for sub in range(2):
        if diff:
            q = q_ref[...]
            lane = lax.broadcasted_iota(jnp.int32, q.shape, 1)
            q_scr[sub] = jnp.where((lane < 64) if sub == 0 else (lane >= 64), q, jnp.zeros_like(q))
        else:
            q_scr[sub] = q_ref[:, LANES * sub:LANES * (sub + 1)]
        m_ref[sub] = jnp.full(m_ref.shape[1:], NEG_INF, F32)
        acc_ref[sub] = jnp.zeros(acc_ref.shape[1:], F32)

    def keys_of(sub, k_tile):
        return k_tile if diff else k_tile[:, LANES * sub:LANES * (sub + 1)]

    def values_of(sub, vt_tile):
        ones = jnp.ones((ONES_ROWS, vt_tile.shape[1]), BF16)
        vt = vt_tile if diff else vt_tile[dv * sub:dv * (sub + 1), :]
        return jnp.concatenate([vt, ones], axis=0)

    def scores(sub, k):
        return lax.dot_general(k, q_scr[sub], (((1,), (1,)), ((), ())), preferred_element_type=F32)

    def update(sub, st, vt, mask):
        if mask is not None:
            st = jnp.where(mask, st, NEG_INF)
        m_prev = m_ref[sub]
        m_new = jnp.maximum(m_prev, jnp.max(st, axis=0, keepdims=True))
        p = jnp.exp(st - m_new)
        acc_ref[sub] = (jnp.exp(m_prev - m_new) * acc_ref[sub]
                        + jnp.dot(vt, p.astype(BF16), preferred_element_type=F32))
        m_ref[sub] = m_new

    def scores_into(buf, tile):
        k_tile = k_ref[pl.ds(pl.multiple_of(tile * TM, TM), TM), :]
        for sub in range(2):
            buf[sub] = scores(sub, keys_of(sub, k_tile))

    def update_from(buf, tile, causal):
        vt_tile = vt_ref[tile]
        mask = None
        if causal:
            key = lax.broadcasted_iota(jnp.int32, (TM, TM), 0)
            qry = lax.broadcasted_iota(jnp.int32, (TM, TM), 1)
            mask = key <= qry
        for sub in range(2):
            update(sub, buf[sub], values_of(sub, vt_tile), mask)

    @pl.when(is_real)
    def _():
        scores_into(sa_ref, 0)

    key = lax.broadcasted_iota(jnp.int32, (META_KEYS_BLOCK, TM), 0)
    qry = lax.broadcasted_iota(jnp.int32, (META_KEYS_BLOCK, TM), 1)
    meta_mask = key <= jnp.where(is_real, N_META - 1, jnp.minimum(qry, N_META - 1))
    km_tile = km_ref[...]
    vtm_tile = vtm_ref[0][:, :META_KEYS_BLOCK]
    meta_scores = [scores(sub, keys_of(sub, km_tile)) for sub in range(2)]
    for sub in range(2):
        update(sub, meta_scores[sub], values_of(sub, vtm_tile), meta_mask)

    def pair(i, carry):
        scores_into(sb_ref, 2 * i + 1)
        update_from(sa_ref, 2 * i, False)
        scores_into(sa_ref, 2 * i + 2)
        update_from(sb_ref, 2 * i + 1, False)
        return carry

    lax.fori_loop(0, jnp.where(is_real, j // 2, 0), pair, 0)
    j_odd = lax.rem(j, 2) == 1

    @pl.when(is_real & j_odd)
    def _():
        scores_into(sb_ref, j)
        update_from(sa_ref, j - 1, False)
        update_from(sb_ref, j, True)

    @pl.when(is_real & jnp.logical_not(j_odd))
    def _():
        update_from(sa_ref, j, True)

    a0 = acc_ref[0]
    a1 = acc_ref[1]
    o0 = a0[:dv] / a0[dv:dv + 1]
    o1 = a1[:dv] / a1[dv:dv + 1]
    if diff:
        d = (o0 - par_ref[3:4, 0:1] * o1).T
        o_ref[...] = (_rms(d, par_ref[1:2, :]) * par_ref[2:3, :]).astype(BF16)
    else:
        o_ref[...] = jnp.concatenate([o0, o1], axis=0).T.astype(BF16)


def _attention(q, k, vt, par, *, diff, nq, nb):
    R = q.shape[0]
    S = nq * TM
    nt = R // TM
    qw = LANES if diff else 2 * LANES
    voff = 0 if diff else HALF // LANES
    meta_blk = (nb * S) // META_KEYS_BLOCK
    batch_of = lambda t: jnp.minimum(t // nq, nb - 1)
    in_specs = [
        pl.BlockSpec((TM, qw), lambda p, t: (t, p)),
        pl.BlockSpec((S, qw), lambda p, t: (batch_of(t), p)),
        pl.BlockSpec((nq, LANES, TM), lambda p, t: (batch_of(t), p + voff, 0)),
        pl.BlockSpec((META_KEYS_BLOCK, qw), lambda p, t: (meta_blk, p)),
        pl.BlockSpec((1, LANES, TM), lambda p, t: (nb * nq, p + voff, 0)),
    ]
    args = [q, k, vt, k, vt]
    if diff:
        in_specs.append(pl.BlockSpec((8, LANES), lambda p, t: (0, 0)))
        args.append(par)
    acc_rows = (LANES if diff else LANES // 2) + ONES_ROWS
    return pl.pallas_call(
        functools.partial(_attn_kernel, diff=diff, nq=nq, nb=nb),
        grid=(4, nt),
        in_specs=in_specs,
        out_specs=pl.BlockSpec((TM, LANES), lambda p, t: (t, p)),
        out_shape=jax.ShapeDtypeStruct((R, HALF), BF16),
        scratch_shapes=[pltpu.VMEM((2, acc_rows, TM), F32), pltpu.VMEM((2, 1, TM), F32),
                        pltpu.VMEM((2, TM, LANES), BF16),
                        pltpu.VMEM((2, TM, TM), F32), pltpu.VMEM((2, TM, TM), F32)],
        compiler_params=_cparams(("arbitrary", "arbitrary")),
        name="diff_attn" if diff else "fox_attn",
    )(*args)


def _mixout_kernel(h_ref, oa_ref, ob_ref, g_ref, wg_ref, wbd_ref, wbf_ref, wo_ref, out_ref):
    x = h_ref[...]
    hb = _rms(x, g_ref[...]).astype(BF16)
    gates = jax.nn.sigmoid(jnp.dot(hb, wg_ref[...], preferred_element_type=F32))
    a = jnp.dot(oa_ref[...], wbd_ref[...], preferred_element_type=F32)
    b = jnp.dot(ob_ref[...], wbf_ref[...], preferred_element_type=F32)
    merged = gates[:, :D_MODEL] * a + gates[:, D_MODEL:] * b
    out_ref[...] = x + jnp.dot(merged.astype(BF16), wo_ref[...], preferred_element_type=F32)


def _mixout(h, oa, ob, g, wg, wbd, wbf, wo):
    R = h.shape[0]
    rows = lambda i: (i, 0)
    const = lambda i: (0, 0)
    return pl.pallas_call(
        _mixout_kernel,
        grid=(R // TM,),
        in_specs=[
            pl.BlockSpec((TM, D_MODEL), rows), pl.BlockSpec((TM, HALF), rows), pl.BlockSpec((TM, HALF), rows),
            pl.BlockSpec((1, D_MODEL), const), pl.BlockSpec((D_MODEL, 2 * D_MODEL), const),
            pl.BlockSpec((HALF, D_MODEL), const), pl.BlockSpec((HALF, D_MODEL), const),
            pl.BlockSpec((D_MODEL, D_MODEL), const),
        ],
        out_specs=pl.BlockSpec((TM, D_MODEL), rows),
        out_shape=jax.ShapeDtypeStruct((R, D_MODEL), F32),
        compiler_params=_cparams(("arbitrary",)),
        name="mixout",
    )(h, oa, ob, g, wg, wbd, wbf, wo)


def _dense_ffn_kernel(h_ref, g_ref, wg_ref, wu_ref, wd_ref, out_ref, *, nf):
    x = h_ref[...]
    hb = _rms(x, g_ref[...]).astype(BF16)
    acc = x
    for c in range(nf):
        sl = slice(TF_DENSE * c, TF_DENSE * (c + 1))
        gate = jnp.dot(hb, wg_ref[:, sl], preferred_element_type=F32)
        up = jnp.dot(hb, wu_ref[:, sl], preferred_element_type=F32)
        mid = (gate * jax.nn.sigmoid(gate) * up).astype(BF16)
        acc = acc + jnp.dot(mid, wd_ref[sl, :], preferred_element_type=F32)
    out_ref[...] = acc


def _dense_ffn(h, g, wg, wu, wd):
    R = h.shape[0]
    dff = wg.shape[1]
    rows = lambda i: (i, 0)
    const = lambda i: (0, 0)
    return pl.pallas_call(
        functools.partial(_dense_ffn_kernel, nf=dff // TF_DENSE),
        grid=(R // TM,),
        in_specs=[
            pl.BlockSpec((TM, D_MODEL), rows), pl.BlockSpec((1, D_MODEL), const),
            pl.BlockSpec((D_MODEL, dff), const), pl.BlockSpec((D_MODEL, dff), const),
            pl.BlockSpec((dff, D_MODEL), const),
        ],
        out_specs=pl.BlockSpec((TM, D_MODEL), rows),
        out_shape=jax.ShapeDtypeStruct((R, D_MODEL), F32),
        compiler_params=_cparams(("arbitrary",)),
        name="dense_ffn",
    )(h, g, wg, wu, wd)


def _route_kernel(h_ref, g_ref, wr_ref, hn_ref, info_ref, cnt_ref, carry_ref):
    i = pl.program_id(0)

    @pl.when(i == 0)
    def _():
        carry_ref[...] = jnp.zeros_like(carry_ref)

    hn = _rms(h_ref[...], g_ref[...])
    hn_ref[...] = hn
    h_hi, h_mid, _ = _split3(hn)
    logits = (jnp.dot(h_hi.astype(BF16), wr_ref[0], preferred_element_type=F32)
              + jnp.dot(h_mid.astype(BF16), wr_ref[0], preferred_element_type=F32)
              + jnp.dot(h_hi.astype(BF16), wr_ref[1], preferred_element_type=F32))
    lane = lax.broadcasted_iota(jnp.int32, logits.shape, 1)
    logits = jnp.where(lane < N_EXPERTS, logits, -jnp.inf)
    v1 = jnp.max(logits, axis=-1, keepdims=True)
    e1 = jnp.min(jnp.where(logits == v1, lane, LANES), axis=-1, keepdims=True)
    rest = jnp.where(lane == e1, -jnp.inf, logits)
    v2 = jnp.max(rest, axis=-1, keepdims=True)
    e2 = jnp.min(jnp.where(rest == v2, lane, LANES), axis=-1, keepdims=True)
    ex = jnp.exp(v2 - v1)
    w1 = 1.0 / (1.0 + ex)
    w2 = ex / (1.0 + ex)
    hot1 = jnp.where(lane == e1, 1.0, 0.0)
    hot2 = jnp.where(lane == e2, 1.0, 0.0)
    hot = hot1 + hot2
    before = _tri_cumsum(hot, inclusive=False) + carry_ref[...]
    r1 = jnp.sum(before * hot1, axis=-1, keepdims=True)
    r2 = jnp.sum(before * hot2, axis=-1, keepdims=True)
    total = before[TM - 1:TM, :] + hot[TM - 1:TM, :]
    carry_ref[...] = total
    cnt_ref[...] = jnp.broadcast_to(total, cnt_ref.shape)
    info_ref[...] = jnp.where(lane == 0, e1.astype(F32), jnp.where(lane == 1, e2.astype(F32),
                              jnp.where(lane == 2, r1, jnp.where(lane == 3, r2,
                                        jnp.where(lane == 4, w1, jnp.where(lane == 5, w2, 0.0))))))


def _route(h, g, wr):
    R = h.shape[0]
    rows = lambda i: (i, 0)
    return pl.pallas_call(
        _route_kernel,
        grid=(R // TM,),
        in_specs=[pl.BlockSpec((TM, D_MODEL), rows), pl.BlockSpec((1, D_MODEL), lambda i: (0, 0)),
                  pl.BlockSpec((2, D_MODEL, LANES), lambda i: (0, 0, 0))],
        out_specs=[pl.BlockSpec((TM, D_MODEL), rows), pl.BlockSpec((TM, LANES), rows),
                   pl.BlockSpec((8, LANES), lambda i: (0, 0))],
        out_shape=[jax.ShapeDtypeStruct((R, D_MODEL), F32), jax.ShapeDtypeStruct((R, LANES), F32),
                   jax.ShapeDtypeStruct((8, LANES), F32)],
        scratch_shapes=[pltpu.VMEM((1, LANES), F32)],
        compiler_params=_cparams(("arbitrary",)),
        name="route",
    )(h, g, wr)


def _scatter_kernel(dest_ref, src_ref, init_ref, out_ref, sem):
    del init_ref

    def copy(r, k):
        return pltpu.make_async_copy(src_ref.at[pl.ds(r, 1)],
                                     out_ref.at[pl.ds(dest_ref[0, 0, 2 * r + k], 1)], sem)

    def issue(r, c):
        copy(r, 0).start()
        copy(r, 1).start()
        return c

    def drain(r, c):
        copy(r, 0).wait()
        copy(r, 1).wait()
        return c

    lax.fori_loop(0, TM, issue, 0)
    lax.fori_loop(0, TM, drain, 0)


def _scatter_rows(dest, src, init):
    R = src.shape[0]
    return pl.pallas_call(
        _scatter_kernel,
        grid=(R // TM,),
        in_specs=[pl.BlockSpec((1, 1, 2 * TM), lambda i: (i, 0, 0), memory_space=pltpu.SMEM),
                  pl.BlockSpec((TM, D_MODEL), lambda i: (i, 0)), pl.BlockSpec(memory_space=pl.ANY)],
        out_specs=pl.BlockSpec(memory_space=pl.ANY),
        out_shape=jax.ShapeDtypeStruct(init.shape, init.dtype),
        scratch_shapes=[pltpu.SemaphoreType.DMA(())],
        input_output_aliases={2: 0},
        compiler_params=pltpu.CompilerParams(dimension_semantics=("arbitrary",), has_side_effects=True),
        name="scatter_rows",
    )(dest, src, init)


def _expert_kernel(te_ref, act_ref, x_ref, wg_ref, wu_ref, wd_ref, y_ref):
    i = pl.program_id(0)
    f = pl.program_id(1)
    del te_ref

    @pl.when(f == 0)
    def _():
        y_ref[...] = jnp.zeros_like(y_ref)

    @pl.when(act_ref[i] > 0)
    def _():
        xb = x_ref[...].astype(BF16)
        gate = jnp.dot(xb, wg_ref[...], preferred_element_type=F32)
        up = jnp.dot(xb, wu_ref[...], preferred_element_type=F32)
        mid = (gate * jax.nn.sigmoid(gate) * up).astype(BF16)
        y_ref[...] += jnp.dot(mid, wd_ref[...], preferred_element_type=F32)


def _experts(tile_expert, tile_active, xs, wg, wu, wd):
    mt = tile_expert.shape[0]
    dffe = wg.shape[2]
    grid_spec = pltpu.PrefetchScalarGridSpec(
        num_scalar_prefetch=2,
        grid=(mt, dffe // TF_MOE),
        in_specs=[
            pl.BlockSpec((TME, D_MODEL), lambda i, f, te, act: (i, 0)),
            pl.BlockSpec((None, D_MODEL, TF_MOE), lambda i, f, te, act: (te[i], 0, f)),
            pl.BlockSpec((None, D_MODEL, TF_MOE), lambda i, f, te, act: (te[i], 0, f)),
            pl.BlockSpec((None, TF_MOE, D_MODEL), lambda i, f, te, act: (te[i], f, 0)),
        ],
        out_specs=pl.BlockSpec((TME, D_MODEL), lambda i, f, te, act: (i, 0)),
    )
    return pl.pallas_call(
        _expert_kernel,
        grid_spec=grid_spec,
        out_shape=jax.ShapeDtypeStruct((mt * TME, D_MODEL), F32),
        compiler_params=_cparams(("arbitrary", "arbitrary")),
        name="experts",
    )(tile_expert, tile_active, xs, wg, wu, wd)


def _combine_kernel(dest_ref, h_ref, info_ref, y_ref, out_ref, buf_ref, sem):
    def copy(r, k):
        return pltpu.make_async_copy(y_ref.at[pl.ds(dest_ref[0, 0, 2 * r + k], 1)],
                                     buf_ref.at[k, pl.ds(r, 1)], sem)

    def issue(r, c):
        copy(r, 0).start()
        copy(r, 1).start()
        return c

    def drain(r, c):
        copy(r, 0).wait()
        copy(r, 1).wait()
        return c

    lax.fori_loop(0, TM, issue, 0)
    lax.fori_loop(0, TM, drain, 0)
    info = info_ref[...]
    out_ref[...] = h_ref[...] + info[:, 4:5] * buf_ref[0] + info[:, 5:6] * buf_ref[1]


def _combine(dest, h, info, y):
    R = h.shape[0]
    rows = lambda i: (i, 0)
    return pl.pallas_call(
        _combine_kernel,
        grid=(R // TM,),
        in_specs=[pl.BlockSpec((1, 1, 2 * TM), lambda i: (i, 0, 0), memory_space=pltpu.SMEM),
                  pl.BlockSpec((TM, D_MODEL), rows), pl.BlockSpec((TM, LANES), rows),
                  pl.BlockSpec(memory_space=pl.ANY)],
        out_specs=pl.BlockSpec((TM, D_MODEL), rows),
        out_shape=jax.ShapeDtypeStruct((R, D_MODEL), F32),
        scratch_shapes=[pltpu.VMEM((2, TM, D_MODEL), F32), pltpu.SemaphoreType.DMA(())],
        compiler_params=_cparams(("arbitrary",)),
        name="combine",
    )(dest, h, info, y)


def _moe_ffn(h, g, wr, wg, wu, wd):
    R = h.shape[0]
    hn, info, counts = _route(h, g, wr)
    cnt = counts[0, :N_EXPERTS].astype(jnp.int32)
    padded = ((cnt + TME - 1) // TME) * TME
    ends = jnp.cumsum(padded)
    starts = ends - padded
    mt = (2 * R) // TME + N_EXPERTS
    tile_row = jnp.arange(mt, dtype=jnp.int32) * TME
    tile_expert = jnp.minimum(jnp.searchsorted(ends, tile_row, side="right"), N_EXPERTS - 1).astype(jnp.int32)
    tile_active = (tile_row < ends[-1]).astype(jnp.int32)
    e12 = info[:, 0:2].astype(jnp.int32)
    dest = (starts[e12] + info[:, 2:4].astype(jnp.int32)).reshape(R // TM, 1, 2 * TM)
    xs = _scatter_rows(dest, hn, jnp.zeros((mt * TME, D_MODEL), F32))
    y = _experts(tile_expert, tile_active, xs, wg, wu, wd)
    return _combine(dest, h, info, y)


def _final_kernel(h_ref, g_ref, out_ref):
    out_ref[...] = _rms(h_ref[...], g_ref[...])


def _final_norm(h, g, n_rows):
    rows = lambda i: (i, 0)
    return pl.pallas_call(
        _final_kernel,
        grid=(n_rows // TM,),
        in_specs=[pl.BlockSpec((TM, D_MODEL), rows), pl.BlockSpec((1, D_MODEL), lambda i: (0, 0))],
        out_specs=pl.BlockSpec((TM, D_MODEL), rows),
        out_shape=jax.ShapeDtypeStruct((n_rows, D_MODEL), F32),
        compiler_params=_cparams(("arbitrary",)),
        name="final_norm",
    )(h, g)


def _rope_table(S):
    rd = DIFF_DH // 4
    inv = ROPE_THETA ** (-jnp.arange(0, rd, 2, dtype=F32) / rd)
    pos = jnp.concatenate([jnp.arange(N_META, N_META + S, dtype=F32), jnp.arange(TM, dtype=F32)])
    ang = pos[:, None] * inv[None, :]
    cos, sin = jnp.cos(ang), jnp.sin(ang)
    n = pos.shape[0]
    cos_t = jnp.tile(jnp.concatenate([cos, cos, jnp.ones((n, 48), F32)], axis=1), (1, 2))
    sin_lo = jnp.tile(jnp.concatenate([-sin, jnp.zeros((n, 56), F32)], axis=1), (1, 2))
    sin_hi = jnp.tile(jnp.concatenate([jnp.zeros((n, 8), F32), sin, jnp.zeros((n, 48), F32)], axis=1), (1, 2))
    return jnp.concatenate([cos_t, sin_lo, sin_hi], axis=1)


def kernel(x, meta_tokens, norm_mix_g, w_in, b_forget, diff_lambda, diff_subln_g, w_branch_diff, w_branch_fox,
           w_out, norm_ffn_g, ffn_w_gate, ffn_w_up, ffn_w_down, moe_router, moe_w_gate, moe_w_up, moe_w_down,
           final_norm_g):
    B, S, D = x.shape
    depth = w_in.shape[0]
    assert D == D_MODEL and S % TM == 0 and meta_tokens.shape[0] == N_META
    nq = S // TM
    h = jnp.concatenate([x.reshape(B * S, D), meta_tokens.astype(x.dtype),
                         jnp.zeros((TM - N_META, D), x.dtype)], axis=0)
    rope = _rope_table(S)
    scale = DIFF_DH ** -0.5
    for layer in range(depth):
        lam_init = 0.8 - 0.6 * math.exp(-0.3 * layer)
        w = w_in[layer]
        dq, dk, dv, fq, fk, fv, ff, ga, gb = jnp.split(
            w, [HALF, 2 * HALF, 3 * HALF, 4 * HALF, 5 * HALF, 6 * HALF, 6 * HALF + FOX_HEADS,
                6 * HALF + FOX_HEADS + D_MODEL], axis=1)
        w1 = jnp.concatenate([dq * scale, dk], axis=1).astype(BF16)
        w2 = jnp.concatenate([dv, fv], axis=1).T.astype(BF16)
        w3 = jnp.concatenate([fq * scale, fk], axis=1).astype(BF16)
        w4 = jnp.pad(ff, ((0, 0), (0, LANES - FOX_HEADS))).astype(BF16)
        bfp = jnp.pad(b_forget[layer].astype(F32), (0, LANES - FOX_HEADS)).reshape(1, LANES)
        g_mix = norm_mix_g[layer].astype(F32).reshape(1, D)
        dq_a, dk_a, vt_a, fq_a, fk_a = _inproj(h, g_mix, w1, w2, w3, w4, bfp, rope, nq=nq)

        lp = diff_lambda[layer].astype(F32)
        lam = jnp.exp(jnp.sum(lp[0] * lp[1])) - jnp.exp(jnp.sum(lp[2] * lp[3])) + lam_init
        par = jnp.zeros((8, LANES), F32)
        par = par.at[1].set(diff_subln_g[layer].astype(F32)).at[2].set(1.0 - lam_init).at[3].set(lam)
        o_a = _attention(dq_a, dk_a, vt_a, par, diff=True, nq=nq, nb=B)
        o_b = _attention(fq_a, fk_a, vt_a, None, diff=False, nq=nq, nb=B)

        wgate = jnp.concatenate([ga, gb], axis=1).astype(BF16)
        h = _mixout(h, o_a, o_b, g_mix, wgate, w_branch_diff[layer].astype(BF16),
                    w_branch_fox[layer].astype(BF16), w_out[layer].astype(BF16))

        g_ffn = norm_ffn_g[layer].astype(F32).reshape(1, D)
        jj = layer // 2
        if layer % 2 == 0:
            h = _dense_ffn(h, g_ffn, ffn_w_gate[jj].astype(BF16), ffn_w_up[jj].astype(BF16),
                           ffn_w_down[jj].astype(BF16))
        else:
            r_hi, r_mid, _ = _split3(jnp.pad(moe_router[jj].astype(F32), ((0, 0), (0, LANES - N_EXPERTS))))
            wr = jnp.stack([r_hi, r_mid]).astype(BF16)
            h = _moe_ffn(h, g_ffn, wr, moe_w_gate[jj].astype(BF16), moe_w_up[jj].astype(BF16),
                         moe_w_down[jj].astype(BF16))
    out = _final_norm(h, final_norm_g.astype(F32).reshape(1, D), B * S)
    return out.reshape(B, S, D)
```

```python
import functools
import math

import jax
import jax.numpy as jnp
from jax import lax
from jax.experimental import pallas as pl
from jax.experimental.pallas import tpu as pltpu

D_MODEL = 1024
N_META = 16
ROPE_THETA = 500000.0
RMS_EPS = 1e-6
NEG_INF = -1e30

DIFF_HEADS = 4
DIFF_DH = 64
FOX_HEADS = 8
FOX_DH = 64
HALF = 512
N_EXPERTS = 8
LANES = 128

TM = 512
TME = 512
TF_MOE = 1792
FF_CHUNK = 256
DMA_UNROLL = 8
LOG2E = 1.4426950408889634
META_KEYS_BLOCK = 128
ONES_ROWS = 16
VMEM_LIMIT = 56 * 1024 * 1024

F32 = jnp.float32
BF16 = jnp.bfloat16


def _cparams(sem):
    return pltpu.CompilerParams(dimension_semantics=sem, vmem_limit_bytes=VMEM_LIMIT)


def _rms(x, g):
    ms = jnp.mean(x * x, axis=-1, keepdims=True)
    return x * lax.rsqrt(ms + RMS_EPS) * g


def _split3(x):
    hi = x.astype(BF16).astype(F32)
    r = x - hi
    mid = r.astype(BF16).astype(F32)
    lo = (r - mid).astype(BF16).astype(F32)
    return hi, mid, lo


def _tri_cumsum(x, inclusive):
    n = x.shape[0]
    row = lax.broadcasted_iota(jnp.int32, (n, n), 0)
    col = lax.broadcasted_iota(jnp.int32, (n, n), 1)
    tri = jnp.where((col <= row) if inclusive else (col < row), 1.0, 0.0).astype(BF16)
    out = jnp.zeros(x.shape, F32)
    for part in _split3(x):
        out = out + jnp.dot(tri, part.astype(BF16), preferred_element_type=F32)
    return out


def _inproj_kernel(h_ref, g_ref, w1_ref, w2_ref, w3_ref, w4_ref, bf_ref, rope_ref,
                   dq_ref, dk_ref, vt_ref, fq_ref, fk_ref, carry_ref, mcarry_ref, *, nq):
    i = pl.program_id(0)

    @pl.when(i == 0)
    def _():
        carry_ref[...] = jnp.zeros_like(carry_ref)
        mcarry_ref[...] = jnp.zeros_like(mcarry_ref)

    hb = _rms(h_ref[...], g_ref[...]).astype(BF16)

    z1 = jnp.dot(hb, w1_ref[...], preferred_element_type=F32)
    cos_t = rope_ref[:, 0:LANES]
    sin_lo = rope_ref[:, LANES:2 * LANES]
    sin_hi = rope_ref[:, 2 * LANES:3 * LANES]
    for j in range(8):
        zj = z1[:, LANES * j:LANES * (j + 1)]
        rot = zj * cos_t + pltpu.roll(zj, LANES - 8, 1) * sin_lo + pltpu.roll(zj, 8, 1) * sin_hi
        dst = dq_ref if j < 4 else dk_ref
        dst[:, LANES * (j % 4):LANES * (j % 4 + 1)] = rot.astype(BF16)

    vt_ref[0] = lax.dot_general(w2_ref[...], hb, (((1,), (1,)), ((), ())),
                                preferred_element_type=F32).astype(BF16)

    z4 = jnp.dot(hb, w4_ref[...], preferred_element_type=F32) + bf_ref[...]
    lane = lax.broadcasted_iota(jnp.int32, z4.shape, 1)
    logf = jnp.minimum(z4, 0.0) - jnp.log1p(jnp.exp(-jnp.abs(z4)))
    logf = jnp.where(lane < FOX_HEADS, logf, 0.0)
    j_in_batch = lax.rem(jnp.maximum(i - 1, 0), nq)
    base = jnp.where(i == 0, 0.0, jnp.where(j_in_batch == 0, mcarry_ref[...], carry_ref[...]))
    c = _tri_cumsum(logf, inclusive=True) + base

    @pl.when(i == 0)
    def _():
        mcarry_ref[...] = c[N_META - 1:N_META, :]

    carry_ref[...] = c[TM - 1:TM, :]

    z3 = jnp.dot(hb, w3_ref[...], preferred_element_type=F32)
    parts = _split3(c * LOG2E)
    for hd in range(FOX_HEADS):
        off = 64 if hd % 2 == 0 else 0
        cols = [p[:, hd:hd + 1] for p in parts]
        one_mask = (lane >= off + 3) & (lane < off + 6)
        aug_q = jnp.where(lane == off, cols[0], jnp.where(lane == off + 1, cols[1],
                          jnp.where(lane == off + 2, cols[2], jnp.where(one_mask, 1.0, 0.0))))
        aug_k = jnp.where(lane == off + 3, -cols[0], jnp.where(lane == off + 4, -cols[1],
                          jnp.where(lane == off + 5, -cols[2],
                                    jnp.where((lane >= off) & (lane < off + 3), 1.0, 0.0))))
        slab = hd // 2
        keep = (lane < 64) if hd % 2 == 0 else (lane >= 64)
        zq = z3[:, LANES * slab:LANES * (slab + 1)]
        zk = z3[:, HALF + LANES * slab:HALF + LANES * (slab + 1)]
        fq_ref[:, LANES * hd:LANES * (hd + 1)] = jnp.where(keep, zq, aug_q).astype(BF16)
        fk_ref[:, LANES * hd:LANES * (hd + 1)] = jnp.where(keep, zk, aug_k).astype(BF16)


def _inproj(h, g, w1, w2, w3, w4, bfp, rope, *, nq):
    R = h.shape[0]
    nt = R // TM
    rows = lambda i: (jnp.where(i == 0, nt - 1, i - 1), 0)
    rope_rows = lambda i: (jnp.where(i == 0, nq, lax.rem(jnp.maximum(i - 1, 0), nq)), 0)
    const = lambda i: (0, 0)
    out_sd = lambda w: jax.ShapeDtypeStruct((R, w), BF16)
    return pl.pallas_call(
        functools.partial(_inproj_kernel, nq=nq),
        grid=(nt,),
        in_specs=[
            pl.BlockSpec((TM, D_MODEL), rows),
            pl.BlockSpec((1, D_MODEL), const),
            pl.BlockSpec((D_MODEL, 2 * HALF), const),
            pl.BlockSpec((2 * HALF, D_MODEL), const),
            pl.BlockSpec((D_MODEL, 2 * HALF), const),
            pl.BlockSpec((D_MODEL, LANES), const),
            pl.BlockSpec((1, LANES), const),
            pl.BlockSpec((TM, 3 * LANES), rope_rows),
        ],
        out_specs=[
            pl.BlockSpec((TM, HALF), rows), pl.BlockSpec((TM, HALF), rows),
            pl.BlockSpec((1, 2 * HALF, TM), lambda i: (rows(i)[0], 0, 0)),
            pl.BlockSpec((TM, 2 * HALF), rows), pl.BlockSpec((TM, 2 * HALF), rows),
        ],
        out_shape=[out_sd(HALF), out_sd(HALF), jax.ShapeDtypeStruct((nt, 2 * HALF, TM), BF16),
                   out_sd(2 * HALF), out_sd(2 * HALF)],
        scratch_shapes=[pltpu.VMEM((1, LANES), F32), pltpu.VMEM((1, LANES), F32)],
        compiler_params=_cparams(("arbitrary",)),
        name="inproj",
    )(h, g, w1, w2, w3, w4, bfp, rope)


def _attn_kernel(*refs, diff, nq, nb):
    if diff:
        q_ref, k_ref, vt_ref, km_ref, vtm_ref, par_ref, o_ref, acc_ref, m_ref, q_scr, sa_ref, sb_ref = refs
    else:
        q_ref, k_ref, vt_ref, km_ref, vtm_ref, o_ref, acc_ref, m_ref, q_scr, sa_ref, sb_ref = refs
    t = pl.program_id(1)
    is_real = t < nb * nq
    j = lax.rem(t, nq)
    dv = acc_ref.shape[1] - ONES_ROWS

    for sub in range(2):
        if diff:
            q = q_ref[...]
            lane = lax.broadcasted_iota(jnp.int32, q.shape, 1)
            q_scr[sub] = jnp.where((lane < 64) if sub == 0 else (lane >= 64), q, jnp.zeros_like(q))
        else:
            q_scr[sub] = q_ref[:, LANES * sub:LANES * (sub + 1)]
        m_ref[sub] = jnp.full(m_ref.shape[1:], NEG_INF, F32)
        acc_ref[sub] = jnp.zeros(acc_ref.shape[1:], F32)

    def keys_of(sub, k_tile):
        return k_tile if diff else k_tile[:, LANES * sub:LANES * (sub + 1)]

    def values_of(sub, vt_tile):
        ones = jnp.ones((ONES_ROWS, vt_tile.shape[1]), BF16)
        vt = vt_tile if diff else vt_tile[dv * sub:dv * (sub + 1), :]
        return jnp.concatenate([vt, ones], axis=0)

    def scores(sub, k):
        return lax.dot_general(k, q_scr[sub], (((1,), (1,)), ((), ())), preferred_element_type=F32)

    def update(sub, st, vt, mask):
        if mask is not None:
            st = jnp.where(mask, st, NEG_INF)
        m_prev = m_ref[sub]
        m_new = jnp.maximum(m_prev, jnp.max(st, axis=0, keepdims=True))
        p = jnp.exp2(st - m_new)
        acc_ref[sub] = (jnp.exp2(m_prev - m_new) * acc_ref[sub]
                        + jnp.dot(vt, p.astype(BF16), preferred_element_type=F32))
        m_ref[sub] = m_new

    def scores_into(buf, tile):
        k_tile = k_ref[pl.ds(pl.multiple_of(tile * TM, TM), TM), :]
        for sub in range(2):
            buf[sub] = scores(sub, keys_of(sub, k_tile))

    def update_from(buf, tile, causal):
        vt_tile = vt_ref[tile]
        mask = None
        if causal:
            key = lax.broadcasted_iota(jnp.int32, (TM, TM), 0)
            qry = lax.broadcasted_iota(jnp.int32, (TM, TM), 1)
            mask = key <= qry
        for sub in range(2):
            update(sub, buf[sub], values_of(sub, vt_tile), mask)

    key = lax.broadcasted_iota(jnp.int32, (META_KEYS_BLOCK, TM), 0)
    qry = lax.broadcasted_iota(jnp.int32, (META_KEYS_BLOCK, TM), 1)
    meta_mask = key <= jnp.where(is_real, N_META - 1, jnp.minimum(qry, N_META - 1))
    km_tile = km_ref[...]
    vtm_tile = vtm_ref[0][:, :META_KEYS_BLOCK]
    meta_scores = [scores(sub, keys_of(sub, km_tile)) for sub in range(2)]
    scores_into(sa_ref, 0)
    for sub in range(2):
        update(sub, meta_scores[sub], values_of(sub, vtm_tile), meta_mask)

    def pair(i, carry):
        scores_into(sb_ref, 2 * i + 1)
        update_from(sa_ref, 2 * i, False)
        scores_into(sa_ref, 2 * i + 2)
        update_from(sb_ref, 2 * i + 1, False)
        return carry

    lax.fori_loop(0, jnp.where(is_real, j // 2, 0), pair, 0)
    j_odd = lax.rem(j, 2) == 1

    @pl.when(is_real & j_odd)
    def _():
        scores_into(sb_ref, j)
        update_from(sa_ref, j - 1, False)
        update_from(sb_ref, j, True)

    @pl.when(is_real & jnp.logical_not(j_odd))
    def _():
        update_from(sa_ref, j, True)

    a0 = acc_ref[0]
    a1 = acc_ref[1]
    o0 = a0[:dv] / a0[dv:dv + 1]
    o1 = a1[:dv] / a1[dv:dv + 1]
    if diff:
        d = (o0 - par_ref[3:4, 0:1] * o1).T
        o_ref[...] = (_rms(d, par_ref[1:2, :]) * par_ref[2:3, :]).astype(BF16)
    else:
        o_ref[...] = jnp.concatenate([o0, o1], axis=0).T.astype(BF16)


def _attention(q, k, vt, par, *, diff, nq, nb):
    R = q.shape[0]
    S = nq * TM
    nt = R // TM
    qw = LANES if diff else 2 * LANES
    voff = 0 if diff else HALF // LANES
    meta_blk = (nb * S) // META_KEYS_BLOCK
    batch_of = lambda t: jnp.minimum(t // nq, nb - 1)
    in_specs = [
        pl.BlockSpec((TM, qw), lambda p, t: (t, p)),
        pl.BlockSpec((S, qw), lambda p, t: (batch_of(t), p)),
        pl.BlockSpec((nq, LANES, TM), lambda p, t: (batch_of(t), p + voff, 0)),
        pl.BlockSpec((META_KEYS_BLOCK, qw), lambda p, t: (meta_blk, p)),
        pl.BlockSpec((1, LANES, TM), lambda p, t: (nb * nq, p + voff, 0)),
    ]
    args = [q, k, vt, k, vt]
    if diff:
        in_specs.append(pl.BlockSpec((8, LANES), lambda p, t: (0, 0)))
        args.append(par)
    acc_rows = (LANES if diff else LANES // 2) + ONES_ROWS
    return pl.pallas_call(
        functools.partial(_attn_kernel, diff=diff, nq=nq, nb=nb),
        grid=(4, nt),
        in_specs=in_specs,
        out_specs=pl.BlockSpec((TM, LANES), lambda p, t: (t, p)),
        out_shape=jax.ShapeDtypeStruct((R, HALF), BF16),
        scratch_shapes=[pltpu.VMEM((2, acc_rows, TM), F32), pltpu.VMEM((2, 1, TM), F32),
                        pltpu.VMEM((2, TM, LANES), BF16),
                        pltpu.VMEM((2, TM, TM), F32), pltpu.VMEM((2, TM, TM), F32)],
        compiler_params=_cparams(("arbitrary", "arbitrary")),
        name="diff_attn" if diff else "fox_attn",
    )(*args)


def _mixout_kernel(h_ref, oa_ref, ob_ref, g_ref, wg_ref, wbd_ref, wbf_ref, wo_ref, out_ref):
    x = h_ref[...]
    hb = _rms(x, g_ref[...]).astype(BF16)
    gates = jax.nn.sigmoid(jnp.dot(hb, wg_ref[...], preferred_element_type=F32))
    a = jnp.dot(oa_ref[...], wbd_ref[...], preferred_element_type=F32)
    b = jnp.dot(ob_ref[...], wbf_ref[...], preferred_element_type=F32)
    merged = gates[:, :D_MODEL] * a + gates[:, D_MODEL:] * b
    out_ref[...] = x + jnp.dot(merged.astype(BF16), wo_ref[...], preferred_element_type=F32)


def _mixout(h, oa, ob, g, wg, wbd, wbf, wo):
    R = h.shape[0]
    rows = lambda i: (i, 0)
    const = lambda i: (0, 0)
    return pl.pallas_call(
        _mixout_kernel,
        grid=(R // TM,),
        in_specs=[
            pl.BlockSpec((TM, D_MODEL), rows), pl.BlockSpec((TM, HALF), rows), pl.BlockSpec((TM, HALF), rows),
            pl.BlockSpec((1, D_MODEL), const), pl.BlockSpec((D_MODEL, 2 * D_MODEL), const),
            pl.BlockSpec((HALF, D_MODEL), const), pl.BlockSpec((HALF, D_MODEL), const),
            pl.BlockSpec((D_MODEL, D_MODEL), const),
        ],
        out_specs=pl.BlockSpec((TM, D_MODEL), rows),
        out_shape=jax.ShapeDtypeStruct((R, D_MODEL), F32),
        compiler_params=_cparams(("arbitrary",)),
        name="mixout",
    )(h, oa, ob, g, wg, wbd, wbf, wo)


def _swiglu_acc(xb, wg_ref, wu_ref, wd_ref, acc):
    def gate_up(c):
        sl = slice(FF_CHUNK * c, FF_CHUNK * (c + 1))
        return (jnp.dot(xb, wg_ref[:, sl], preferred_element_type=F32),
                jnp.dot(xb, wu_ref[:, sl], preferred_element_type=F32))

    nf = wg_ref.shape[1] // FF_CHUNK
    nxt = gate_up(0)
    for c in range(nf):
        gate, up = nxt
        if c + 1 < nf:
            nxt = gate_up(c + 1)
        mid = (gate * jax.nn.sigmoid(gate) * up).astype(BF16)
        acc = acc + jnp.dot(mid, wd_ref[FF_CHUNK * c:FF_CHUNK * (c + 1), :], preferred_element_type=F32)
    return acc


def _dense_ffn_kernel(h_ref, g_ref, wg_ref, wu_ref, wd_ref, out_ref):
    x = h_ref[...]
    hb = _rms(x, g_ref[...]).astype(BF16)
    out_ref[...] = _swiglu_acc(hb, wg_ref, wu_ref, wd_ref, x)


def _dense_ffn(h, g, wg, wu, wd):
    R = h.shape[0]
    dff = wg.shape[1]
    rows = lambda i: (i, 0)
    const = lambda i: (0, 0)
    return pl.pallas_call(
        _dense_ffn_kernel,
        grid=(R // TM,),
        in_specs=[
            pl.BlockSpec((TM, D_MODEL), rows), pl.BlockSpec((1, D_MODEL), const),
            pl.BlockSpec((D_MODEL, dff), const), pl.BlockSpec((D_MODEL, dff), const),
            pl.BlockSpec((dff, D_MODEL), const),
        ],
        out_specs=pl.BlockSpec((TM, D_MODEL), rows),
        out_shape=jax.ShapeDtypeStruct((R, D_MODEL), F32),
        compiler_params=_cparams(("arbitrary",)),
        name="dense_ffn",
    )(h, g, wg, wu, wd)


def _route_kernel(h_ref, g_ref, wr_ref, hn_ref, info_ref, cnt_ref, carry_ref):
    i = pl.program_id(0)

    @pl.when(i == 0)
    def _():
        carry_ref[...] = jnp.zeros_like(carry_ref)

    hn = _rms(h_ref[...], g_ref[...])
    hn_ref[...] = hn
    h_hi, h_mid, _ = _split3(hn)
    logits = (jnp.dot(h_hi.astype(BF16), wr_ref[0], preferred_element_type=F32)
              + jnp.dot(h_mid.astype(BF16), wr_ref[0], preferred_element_type=F32)
              + jnp.dot(h_hi.astype(BF16), wr_ref[1], preferred_element_type=F32))
    lane = lax.broadcasted_iota(jnp.int32, logits.shape, 1)
    logits = jnp.where(lane < N_EXPERTS, logits, -jnp.inf)
    v1 = jnp.max(logits, axis=-1, keepdims=True)
    e1 = jnp.min(jnp.where(logits == v1, lane, LANES), axis=-1, keepdims=True)
    rest = jnp.where(lane == e1, -jnp.inf, logits)
    v2 = jnp.max(rest, axis=-1, keepdims=True)
    e2 = jnp.min(jnp.where(rest == v2, lane, LANES), axis=-1, keepdims=True)
    ex = jnp.exp(v2 - v1)
    w1 = 1.0 / (1.0 + ex)
    w2 = ex / (1.0 + ex)
    hot1 = jnp.where(lane == e1, 1.0, 0.0)
    hot2 = jnp.where(lane == e2, 1.0, 0.0)
    hot = hot1 + hot2
    before = _tri_cumsum(hot, inclusive=False) + carry_ref[...]
    r1 = jnp.sum(before * hot1, axis=-1, keepdims=True)
    r2 = jnp.sum(before * hot2, axis=-1, keepdims=True)
    total = before[TM - 1:TM, :] + hot[TM - 1:TM, :]
    carry_ref[...] = total
    cnt_ref[...] = jnp.broadcast_to(total, cnt_ref.shape)
    info_ref[...] = jnp.where(lane == 0, e1.astype(F32), jnp.where(lane == 1, e2.astype(F32),
                              jnp.where(lane == 2, r1, jnp.where(lane == 3, r2,
                                        jnp.where(lane == 4, w1, jnp.where(lane == 5, w2, 0.0))))))


def _route(h, g, wr):
    R = h.shape[0]
    rows = lambda i: (i, 0)
    return pl.pallas_call(
        _route_kernel,
        grid=(R // TM,),
        in_specs=[pl.BlockSpec((TM, D_MODEL), rows), pl.BlockSpec((1, D_MODEL), lambda i: (0, 0)),
                  pl.BlockSpec((2, D_MODEL, LANES), lambda i: (0, 0, 0))],
        out_specs=[pl.BlockSpec((TM, D_MODEL), rows), pl.BlockSpec((TM, LANES), rows),
                   pl.BlockSpec((8, LANES), lambda i: (0, 0))],
        out_shape=[jax.ShapeDtypeStruct((R, D_MODEL), F32), jax.ShapeDtypeStruct((R, LANES), F32),
                   jax.ShapeDtypeStruct((8, LANES), F32)],
        scratch_shapes=[pltpu.VMEM((1, LANES), F32)],
        compiler_params=_cparams(("arbitrary",)),
        name="route",
    )(h, g, wr)


def _scatter_kernel(dest_ref, src_ref, init_ref, out_ref, sem):
    del init_ref

    def copy(r, k):
        return pltpu.make_async_copy(src_ref.at[pl.ds(r, 1)],
                                     out_ref.at[pl.ds(dest_ref[0, 0, 2 * r + k], 1)], sem)

    def issue(r, c):
        copy(r, 0).start()
        copy(r, 1).start()
        return c

    lax.fori_loop(0, TM, issue, 0, unroll=DMA_UNROLL)
    for _ in range(2):
        pltpu.make_async_copy(src_ref, out_ref.at[pl.ds(0, TM)], sem).wait()


def _scatter_rows(dest, src, init):
    R = src.shape[0]
    return pl.pallas_call(
        _scatter_kernel,
        grid=(R // TM,),
        in_specs=[pl.BlockSpec((1, 1, 2 * TM), lambda i: (i, 0, 0), memory_space=pltpu.SMEM),
                  pl.BlockSpec((TM, D_MODEL), lambda i: (i, 0)), pl.BlockSpec(memory_space=pl.ANY)],
        out_specs=pl.BlockSpec(memory_space=pl.ANY),
        out_shape=jax.ShapeDtypeStruct(init.shape, init.dtype),
        scratch_shapes=[pltpu.SemaphoreType.DMA(())],
        input_output_aliases={2: 0},
        compiler_params=pltpu.CompilerParams(dimension_semantics=("arbitrary",), has_side_effects=True),
        name="scatter_rows",
    )(dest, src, init)


def _expert_kernel(te_ref, act_ref, x_ref, wg_ref, wu_ref, wd_ref, y_ref):
    i = pl.program_id(0)
    f = pl.program_id(1)
    del te_ref

    @pl.when(f == 0)
    def _():
        y_ref[...] = jnp.zeros_like(y_ref)

    @pl.when(act_ref[i] > 0)
    def _():
        y_ref[...] = _swiglu_acc(x_ref[...].astype(BF16), wg_ref, wu_ref, wd_ref, y_ref[...])


def _experts(tile_expert, tile_active, xs, wg, wu, wd):
    mt = tile_expert.shape[0]
    dffe = wg.shape[2]
    grid_spec = pltpu.PrefetchScalarGridSpec(
        num_scalar_prefetch=2,
        grid=(mt, dffe // TF_MOE),
        in_specs=[
            pl.BlockSpec((TME, D_MODEL), lambda i, f, te, act: (i, 0)),
            pl.BlockSpec((None, D_MODEL, TF_MOE), lambda i, f, te, act: (te[i], 0, f)),
            pl.BlockSpec((None, D_MODEL, TF_MOE), lambda i, f, te, act: (te[i], 0, f)),
            pl.BlockSpec((None, TF_MOE, D_MODEL), lambda i, f, te, act: (te[i], f, 0)),
        ],
        out_specs=pl.BlockSpec((TME, D_MODEL), lambda i, f, te, act: (i, 0)),
    )
    return pl.pallas_call(
        _expert_kernel,
        grid_spec=grid_spec,
        out_shape=jax.ShapeDtypeStruct((mt * TME, D_MODEL), F32),
        compiler_params=_cparams(("arbitrary", "arbitrary")),
        name="experts",
    )(tile_expert, tile_active, xs, wg, wu, wd)


def _combine_kernel(dest_ref, h_ref, info_ref, y_ref, out_ref, buf_ref, sem):
    def copy(r, k):
        return pltpu.make_async_copy(y_ref.at[pl.ds(dest_ref[0, 0, 2 * r + k], 1)],
                                     buf_ref.at[k, pl.ds(r, 1)], sem)

    def issue(r, c):
        copy(r, 0).start()
        copy(r, 1).start()
        return c

    lax.fori_loop(0, TM, issue, 0, unroll=DMA_UNROLL)
    for k in range(2):
        pltpu.make_async_copy(y_ref.at[pl.ds(0, TM)], buf_ref.at[k], sem).wait()
    info = info_ref[...]
    out_ref[...] = h_ref[...] + info[:, 4:5] * buf_ref[0] + info[:, 5:6] * buf_ref[1]


def _combine(dest, h, info, y):
    R = h.shape[0]
    rows = lambda i: (i, 0)
    return pl.pallas_call(
        _combine_kernel,
        grid=(R // TM,),
        in_specs=[pl.BlockSpec((1, 1, 2 * TM), lambda i: (i, 0, 0), memory_space=pltpu.SMEM),
                  pl.BlockSpec((TM, D_MODEL), rows), pl.BlockSpec((TM, LANES), rows),
                  pl.BlockSpec(memory_space=pl.ANY)],
        out_specs=pl.BlockSpec((TM, D_MODEL), rows),
        out_shape=jax.ShapeDtypeStruct((R, D_MODEL), F32),
        scratch_shapes=[pltpu.VMEM((2, TM, D_MODEL), F32), pltpu.SemaphoreType.DMA(())],
        compiler_params=_cparams(("arbitrary",)),
        name="combine",
    )(dest, h, info, y)


def _moe_ffn(h, g, wr, wg, wu, wd):
    R = h.shape[0]
    hn, info, counts = _route(h, g, wr)
    cnt = counts[0, :N_EXPERTS].astype(jnp.int32)
    padded = ((cnt + TME - 1) // TME) * TME
    ends = jnp.cumsum(padded)
    starts = ends - padded
    mt = (2 * R) // TME + N_EXPERTS
    tile_row = jnp.arange(mt, dtype=jnp.int32) * TME
    tile_expert = jnp.minimum(jnp.searchsorted(ends, tile_row, side="right"), N_EXPERTS - 1).astype(jnp.int32)
    tile_active = (tile_row < ends[-1]).astype(jnp.int32)
    e12 = info[:, 0:2].astype(jnp.int32)
    dest = (starts[e12] + info[:, 2:4].astype(jnp.int32)).reshape(R // TM, 1, 2 * TM)
    xs = _scatter_rows(dest, hn, jnp.zeros((mt * TME, D_MODEL), F32))
    y = _experts(tile_expert, tile_active, xs, wg, wu, wd)
    return _combine(dest, h, info, y)


def _final_kernel(h_ref, g_ref, out_ref):
    out_ref[...] = _rms(h_ref[...], g_ref[...])


def _final_norm(h, g, n_rows):
    rows = lambda i: (i, 0)
    return pl.pallas_call(
        _final_kernel,
        grid=(n_rows // TM,),
        in_specs=[pl.BlockSpec((TM, D_MODEL), rows), pl.BlockSpec((1, D_MODEL), lambda i: (0, 0))],
        out_specs=pl.BlockSpec((TM, D_MODEL), rows),
        out_shape=jax.ShapeDtypeStruct((n_rows, D_MODEL), F32),
        compiler_params=_cparams(("arbitrary",)),
        name="final_norm",
    )(h, g)


def _rope_table(S):
    rd = DIFF_DH // 4
    inv = ROPE_THETA ** (-jnp.arange(0, rd, 2, dtype=F32) / rd)
    pos = jnp.concatenate([jnp.arange(N_META, N_META + S, dtype=F32), jnp.arange(TM, dtype=F32)])
    ang = pos[:, None] * inv[None, :]
    cos, sin = jnp.cos(ang), jnp.sin(ang)
    n = pos.shape[0]
    cos_t = jnp.tile(jnp.concatenate([cos, cos, jnp.ones((n, 48), F32)], axis=1), (1, 2))
    sin_lo = jnp.tile(jnp.concatenate([-sin, jnp.zeros((n, 56), F32)], axis=1), (1, 2))
    sin_hi = jnp.tile(jnp.concatenate([jnp.zeros((n, 8), F32), sin, jnp.zeros((n, 48), F32)], axis=1), (1, 2))
    return jnp.concatenate([cos_t, sin_lo, sin_hi], axis=1)


def kernel(x, meta_tokens, norm_mix_g, w_in, b_forget, diff_lambda, diff_subln_g, w_branch_diff, w_branch_fox,
           w_out, norm_ffn_g, ffn_w_gate, ffn_w_up, ffn_w_down, moe_router, moe_w_gate, moe_w_up, moe_w_down,
           final_norm_g):
    B, S, D = x.shape
    depth = w_in.shape[0]
    assert D == D_MODEL and S % TM == 0 and meta_tokens.shape[0] == N_META
    nq = S // TM
    h = jnp.concatenate([x.reshape(B * S, D), meta_tokens.astype(x.dtype),
                         jnp.zeros((TM - N_META, D), x.dtype)], axis=0)
    rope = _rope_table(S)
    scale = DIFF_DH ** -0.5 * LOG2E
    for layer in range(depth):
        lam_init = 0.8 - 0.6 * math.exp(-0.3 * layer)
        w = w_in[layer]
        dq, dk, dv, fq, fk, fv, ff, ga, gb = jnp.split(
            w, [HALF, 2 * HALF, 3 * HALF, 4 * HALF, 5 * HALF, 6 * HALF, 6 * HALF + FOX_HEADS,
                6 * HALF + FOX_HEADS + D_MODEL], axis=1)
        w1 = jnp.concatenate([dq * scale, dk], axis=1).astype(BF16)
        w2 = jnp.concatenate([dv, fv], axis=1).T.astype(BF16)
        w3 = jnp.concatenate([fq * scale, fk], axis=1).astype(BF16)
        w4 = jnp.pad(ff, ((0, 0), (0, LANES - FOX_HEADS))).astype(BF16)
        bfp = jnp.pad(b_forget[layer].astype(F32), (0, LANES - FOX_HEADS)).reshape(1, LANES)
        g_mix = norm_mix_g[layer].astype(F32).reshape(1, D)
        dq_a, dk_a, vt_a, fq_a, fk_a = _inproj(h, g_mix, w1, w2, w3, w4, bfp, rope, nq=nq)

        lp = diff_lambda[layer].astype(F32)
        lam = jnp.exp(jnp.sum(lp[0] * lp[1])) - jnp.exp(jnp.sum(lp[2] * lp[3])) + lam_init
        par = jnp.zeros((8, LANES), F32)
        par = par.at[1].set(diff_subln_g[layer].astype(F32)).at[2].set(1.0 - lam_init).at[3].set(lam)
        o_a = _attention(dq_a, dk_a, vt_a, par, diff=True, nq=nq, nb=B)
        o_b = _attention(fq_a, fk_a, vt_a, None, diff=False, nq=nq, nb=B)

        wgate = jnp.concatenate([ga, gb], axis=1).astype(BF16)
        h = _mixout(h, o_a, o_b, g_mix, wgate, w_branch_diff[layer].astype(BF16),
                    w_branch_fox[layer].astype(BF16), w_out[layer].astype(BF16))

        g_ffn = norm_ffn_g[layer].astype(F32).reshape(1, D)
        jj = layer // 2
        if layer % 2 == 0:
            h = _dense_ffn(h, g_ffn, ffn_w_gate[jj].astype(BF16), ffn_w_up[jj].astype(BF16),
                           ffn_w_down[jj].astype(BF16))
        else:
            r_hi, r_mid, _ = _split3(jnp.pad(moe_router[jj].astype(F32), ((0, 0), (0, LANES - N_EXPERTS))))
            wr = jnp.stack([r_hi, r_mid]).astype(BF16)
            h = _moe_ffn(h, g_ffn, wr, moe_w_gate[jj].astype(BF16), moe_w_up[jj].astype(BF16),
                         moe_w_down[jj].astype(BF16))
    out = _final_norm(h, final_norm_g.astype(F32).reshape(1, D), B * S)
    return out.reshape(B, S, D)
```

```python
import functools
import math

import jax
import jax.numpy as jnp
from jax import lax
from jax.experimental import pallas as pl
from jax.experimental.pallas import tpu as pltpu

D_MODEL = 1024
N_META = 16
ROPE_THETA = 500000.0
RMS_EPS = 1e-6
NEG_INF = -1e30

DIFF_HEADS = 4
DIFF_DH = 64
FOX_HEADS = 8
FOX_DH = 64
HALF = 512
N_EXPERTS = 8
LANES = 128

TM = 512
KEY_TILES_PER_Q = 2
TQ = KEY_TILES_PER_Q * TM
TME = 512
TF_MOE = 1792
FF_CHUNK = 256
DMA_UNROLL = 8
LOG2E = 1.4426950408889634
META_KEYS_BLOCK = 128
ONES_ROWS = 16
VMEM_LIMIT = 56 * 1024 * 1024

F32 = jnp.float32
BF16 = jnp.bfloat16


def _cparams(sem):
    return pltpu.CompilerParams(dimension_semantics=sem, vmem_limit_bytes=VMEM_LIMIT)


def _rms(x, g):
    ms = jnp.mean(x * x, axis=-1, keepdims=True)
    return x * lax.rsqrt(ms + RMS_EPS) * g


def _split3(x):
    hi = x.astype(BF16).astype(F32)
    r = x - hi
    mid = r.astype(BF16).astype(F32)
    lo = (r - mid).astype(BF16).astype(F32)
    return hi, mid, lo


def _tri_cumsum(x, inclusive):
    n = x.shape[0]
    row = lax.broadcasted_iota(jnp.int32, (n, n), 0)
    col = lax.broadcasted_iota(jnp.int32, (n, n), 1)
    tri = jnp.where((col <= row) if inclusive else (col < row), 1.0, 0.0).astype(BF16)
    out = jnp.zeros(x.shape, F32)
    for part in _split3(x):
        out = out + jnp.dot(tri, part.astype(BF16), preferred_element_type=F32)
    return out


def _inproj_kernel(h_ref, g_ref, w1_ref, w2_ref, w3_ref, w4_ref, bf_ref, rope_ref,
                   dq_ref, dk_ref, vt_ref, fq_ref, fk_ref, carry_ref, mcarry_ref, *, nq):
    i = pl.program_id(0)

    @pl.when(i == 0)
    def _():
        carry_ref[...] = jnp.zeros_like(carry_ref)
        mcarry_ref[...] = jnp.zeros_like(mcarry_ref)

    hb = _rms(h_ref[...], g_ref[...]).astype(BF16)

    z1 = jnp.dot(hb, w1_ref[...], preferred_element_type=F32)
    cos_t = rope_ref[:, 0:LANES]
    sin_lo = rope_ref[:, LANES:2 * LANES]
    sin_hi = rope_ref[:, 2 * LANES:3 * LANES]
    for j in range(8):
        zj = z1[:, LANES * j:LANES * (j + 1)]
        rot = zj * cos_t + pltpu.roll(zj, LANES - 8, 1) * sin_lo + pltpu.roll(zj, 8, 1) * sin_hi
        dst = dq_ref if j < 4 else dk_ref
        dst[:, LANES * (j % 4):LANES * (j % 4 + 1)] = rot.astype(BF16)

    vt_ref[0] = lax.dot_general(w2_ref[...], hb, (((1,), (1,)), ((), ())),
                                preferred_element_type=F32).astype(BF16)

    z4 = jnp.dot(hb, w4_ref[...], preferred_element_type=F32) + bf_ref[...]
    lane = lax.broadcasted_iota(jnp.int32, z4.shape, 1)
    logf = jnp.minimum(z4, 0.0) - jnp.log1p(jnp.exp(-jnp.abs(z4)))
    logf = jnp.where(lane < FOX_HEADS, logf, 0.0)
    j_in_batch = lax.rem(jnp.maximum(i - 1, 0), nq)
    base = jnp.where(i == 0, 0.0, jnp.where(j_in_batch == 0, mcarry_ref[...], carry_ref[...]))
    c = _tri_cumsum(logf, inclusive=True) + base

    @pl.when(i == 0)
    def _():
        mcarry_ref[...] = c[N_META - 1:N_META, :]

    carry_ref[...] = c[TM - 1:TM, :]

    z3 = jnp.dot(hb, w3_ref[...], preferred_element_type=F32)
    parts = _split3(c * LOG2E)
    for hd in range(FOX_HEADS):
        off = 64 if hd % 2 == 0 else 0
        cols = [p[:, hd:hd + 1] for p in parts]
        one_mask = (lane >= off + 3) & (lane < off + 6)
        aug_q = jnp.where(lane == off, cols[0], jnp.where(lane == off + 1, cols[1],
                          jnp.where(lane == off + 2, cols[2], jnp.where(one_mask, 1.0, 0.0))))
        aug_k = jnp.where(lane == off + 3, -cols[0], jnp.where(lane == off + 4, -cols[1],
                          jnp.where(lane == off + 5, -cols[2],
                                    jnp.where((lane >= off) & (lane < off + 3), 1.0, 0.0))))
        slab = hd // 2
        keep = (lane < 64) if hd % 2 == 0 else (lane >= 64)
        zq = z3[:, LANES * slab:LANES * (slab + 1)]
        zk = z3[:, HALF + LANES * slab:HALF + LANES * (slab + 1)]
        fq_ref[:, LANES * hd:LANES * (hd + 1)] = jnp.where(keep, zq, aug_q).astype(BF16)
        fk_ref[:, LANES * hd:LANES * (hd + 1)] = jnp.where(keep, zk, aug_k).astype(BF16)


def _inproj(h, g, w1, w2, w3, w4, bfp, rope, *, nq):
    R = h.shape[0]
    nt = R // TM
    n_real = nt - TQ // TM
    rows = lambda i: (jnp.where(i == 0, n_real, jnp.where(i <= n_real, i - 1, i)), 0)
    rope_rows = lambda i: (jnp.where(i == 0, nq, lax.rem(jnp.maximum(i - 1, 0), nq)), 0)
    const = lambda i: (0, 0)
    out_sd = lambda w: jax.ShapeDtypeStruct((R, w), BF16)
    return pl.pallas_call(
        functools.partial(_inproj_kernel, nq=nq),
        grid=(nt,),
        in_specs=[
            pl.BlockSpec((TM, D_MODEL), rows),
            pl.BlockSpec((1, D_MODEL), const),
            pl.BlockSpec((D_MODEL, 2 * HALF), const),
            pl.BlockSpec((2 * HALF, D_MODEL), const),
            pl.BlockSpec((D_MODEL, 2 * HALF), const),
            pl.BlockSpec((D_MODEL, LANES), const),
            pl.BlockSpec((1, LANES), const),
            pl.BlockSpec((TM, 3 * LANES), rope_rows),
        ],
        out_specs=[
            pl.BlockSpec((TM, HALF), rows), pl.BlockSpec((TM, HALF), rows),
            pl.BlockSpec((1, 2 * HALF, TM), lambda i: (rows(i)[0], 0, 0)),
            pl.BlockSpec((TM, 2 * HALF), rows), pl.BlockSpec((TM, 2 * HALF), rows),
        ],
        out_shape=[out_sd(HALF), out_sd(HALF), jax.ShapeDtypeStruct((nt, 2 * HALF, TM), BF16),
                   out_sd(2 * HALF), out_sd(2 * HALF)],
        scratch_shapes=[pltpu.VMEM((1, LANES), F32), pltpu.VMEM((1, LANES), F32)],
        compiler_params=_cparams(("arbitrary",)),
        name="inproj",
    )(h, g, w1, w2, w3, w4, bfp, rope)


def _attn_kernel(*refs, diff, nq, nb):
    if diff:
        q_ref, k_ref, vt_ref, km_ref, vtm_ref, par_ref, o_ref, acc_ref, m_ref, q_scr, sa_ref, sb_ref = refs
    else:
        q_ref, k_ref, vt_ref, km_ref, vtm_ref, o_ref, acc_ref, m_ref, q_scr, sa_ref, sb_ref = refs
    t = pl.program_id(1)
    nqt = nq // KEY_TILES_PER_Q
    is_real = t < nb * nqt
    jq = lax.rem(t, nqt)
    dv = acc_ref.shape[1] - ONES_ROWS

    for sub in range(2):
        if diff:
            q = q_ref[...]
            lane = lax.broadcasted_iota(jnp.int32, q.shape, 1)
            q_scr[sub] = jnp.where((lane < 64) if sub == 0 else (lane >= 64), q, jnp.zeros_like(q))
        else:
            q_scr[sub] = q_ref[:, LANES * sub:LANES * (sub + 1)]
        m_ref[sub] = jnp.full(m_ref.shape[1:], NEG_INF, F32)
        acc_ref[sub] = jnp.zeros(acc_ref.shape[1:], F32)

    def keys_of(sub, k_tile):
        return k_tile if diff else k_tile[:, LANES * sub:LANES * (sub + 1)]

    def values_of(sub, vt_tile):
        ones = jnp.ones((ONES_ROWS, vt_tile.shape[1]), BF16)
        vt = vt_tile if diff else vt_tile[dv * sub:dv * (sub + 1), :]
        return jnp.concatenate([vt, ones], axis=0)

    def scores(sub, k, q0):
        return lax.dot_general(k, q_scr[sub, q0:, :], (((1,), (1,)), ((), ())), preferred_element_type=F32)

    def update(sub, st, vt, mask, q0):
        if mask is not None:
            st = jnp.where(mask, st, NEG_INF)
        m_prev = m_ref[sub, :, q0:]
        m_new = jnp.maximum(m_prev, jnp.max(st, axis=0, keepdims=True))
        p = jnp.exp2(st - m_new)
        acc_ref[sub, :, q0:] = (jnp.exp2(m_prev - m_new) * acc_ref[sub, :, q0:]
                                + jnp.dot(vt, p.astype(BF16), preferred_element_type=F32))
        m_ref[sub, :, q0:] = m_new

    def scores_into(buf, tile, q0):
        k_tile = k_ref[pl.ds(pl.multiple_of(tile * TM, TM), TM), :]
        for sub in range(2):
            buf[sub, :, q0:] = scores(sub, keys_of(sub, k_tile), q0)

    def update_from(buf, tile, causal, q0):
        vt_tile = vt_ref[tile]
        mask = None
        if causal:
            key = lax.broadcasted_iota(jnp.int32, (TM, TQ - q0), 0)
            qry = lax.broadcasted_iota(jnp.int32, (TM, TQ - q0), 1)
            mask = key <= qry
        for sub in range(2):
            update(sub, buf[sub, :, q0:], values_of(sub, vt_tile), mask, q0)

    key = lax.broadcasted_iota(jnp.int32, (META_KEYS_BLOCK, TQ), 0)
    qry = lax.broadcasted_iota(jnp.int32, (META_KEYS_BLOCK, TQ), 1)
    meta_mask = key <= jnp.where(is_real, N_META - 1, jnp.minimum(qry, N_META - 1))
    km_tile = km_ref[...]
    vtm_tile = vtm_ref[0][:, :META_KEYS_BLOCK]
    meta_scores = [scores(sub, keys_of(sub, km_tile), 0) for sub in range(2)]

    scores_into(sa_ref, 0, 0)
    for sub in range(2):
        update(sub, meta_scores[sub], values_of(sub, vtm_tile), meta_mask, 0)

    def pair(i, carry):
        scores_into(sb_ref, 2 * i + 1, 0)
        update_from(sa_ref, 2 * i, False, 0)
        scores_into(sa_ref, 2 * i + 2, 0)
        update_from(sb_ref, 2 * i + 1, False, 0)
        return carry

    lax.fori_loop(0, jnp.where(is_real, jq, 0), pair, 0)

    @pl.when(is_real)
    def _():
        scores_into(sb_ref, 2 * jq + 1, TM)
        update_from(sa_ref, 2 * jq, True, 0)
        update_from(sb_ref, 2 * jq + 1, True, TM)

    a0 = acc_ref[0]
    a1 = acc_ref[1]
    o0 = a0[:dv] / a0[dv:dv + 1]
    o1 = a1[:dv] / a1[dv:dv + 1]
    if diff:
        d = (o0 - par_ref[3:4, 0:1] * o1).T
        o_ref[...] = (_rms(d, par_ref[1:2, :]) * par_ref[2:3, :]).astype(BF16)
    else:
        o_ref[...] = jnp.concatenate([o0, o1], axis=0).T.astype(BF16)


def _attention(q, k, vt, par, *, diff, nq, nb):
    R = q.shape[0]
    S = nq * TM
    ntq = R // TQ
    qw = LANES if diff else 2 * LANES
    voff = 0 if diff else HALF // LANES
    meta_blk = (nb * S) // META_KEYS_BLOCK
    batch_of = lambda t: jnp.minimum(t // (nq // KEY_TILES_PER_Q), nb - 1)
    in_specs = [
        pl.BlockSpec((TQ, qw), lambda p, t: (t, p)),
        pl.BlockSpec((S, qw), lambda p, t: (batch_of(t), p)),
        pl.BlockSpec((nq, LANES, TM), lambda p, t: (batch_of(t), p + voff, 0)),
        pl.BlockSpec((META_KEYS_BLOCK, qw), lambda p, t: (meta_blk, p)),
        pl.BlockSpec((1, LANES, TM), lambda p, t: (nb * nq, p + voff, 0)),
    ]
    args = [q, k, vt, k, vt]
    if diff:
        in_specs.append(pl.BlockSpec((8, LANES), lambda p, t: (0, 0)))
        args.append(par)
    acc_rows = (LANES if diff else LANES // 2) + ONES_ROWS
    return pl.pallas_call(
        functools.partial(_attn_kernel, diff=diff, nq=nq, nb=nb),
        grid=(4, ntq),
        in_specs=in_specs,
        out_specs=pl.BlockSpec((TQ, LANES), lambda p, t: (t, p)),
        out_shape=jax.ShapeDtypeStruct((R, HALF), BF16),
        scratch_shapes=[pltpu.VMEM((2, acc_rows, TQ), F32), pltpu.VMEM((2, 1, TQ), F32),
                        pltpu.VMEM((2, TQ, LANES), BF16),
                        pltpu.VMEM((2, TM, TQ), F32), pltpu.VMEM((2, TM, TQ), F32)],
        compiler_params=_cparams(("arbitrary", "arbitrary")),
        name="diff_attn" if diff else "fox_attn",
    )(*args)


def _mixout_kernel(h_ref, oa_ref, ob_ref, g_ref, wg_ref, wbd_ref, wbf_ref, wo_ref, out_ref):
    x = h_ref[...]
    hb = _rms(x, g_ref[...]).astype(BF16)
    gates = jax.nn.sigmoid(jnp.dot(hb, wg_ref[...], preferred_element_type=F32))
    a = jnp.dot(oa_ref[...], wbd_ref[...], preferred_element_type=F32)
    b = jnp.dot(ob_ref[...], wbf_ref[...], preferred_element_type=F32)
    merged = gates[:, :D_MODEL] * a + gates[:, D_MODEL:] * b
    out_ref[...] = x + jnp.dot(merged.astype(BF16), wo_ref[...], preferred_element_type=F32)


def _mixout(h, oa, ob, g, wg, wbd, wbf, wo):
    R = h.shape[0]
    rows = lambda i: (i, 0)
    const = lambda i: (0, 0)
    return pl.pallas_call(
        _mixout_kernel,
        grid=(R // TM,),
        in_specs=[
            pl.BlockSpec((TM, D_MODEL), rows), pl.BlockSpec((TM, HALF), rows), pl.BlockSpec((TM, HALF), rows),
            pl.BlockSpec((1, D_MODEL), const), pl.BlockSpec((D_MODEL, 2 * D_MODEL), const),
            pl.BlockSpec((HALF, D_MODEL), const), pl.BlockSpec((HALF, D_MODEL), const),
            pl.BlockSpec((D_MODEL, D_MODEL), const),
        ],
        out_specs=pl.BlockSpec((TM, D_MODEL), rows),
        out_shape=jax.ShapeDtypeStruct((R, D_MODEL), F32),
        compiler_params=_cparams(("arbitrary",)),
        name="mixout",
    )(h, oa, ob, g, wg, wbd, wbf, wo)


def _swiglu_acc(xb, wg_ref, wu_ref, wd_ref, acc):
    def gate_up(c):
        sl = slice(FF_CHUNK * c, FF_CHUNK * (c + 1))
        return (jnp.dot(xb, wg_ref[:, sl], preferred_element_type=F32),
                jnp.dot(xb, wu_ref[:, sl], preferred_element_type=F32))

    nf = wg_ref.shape[1] // FF_CHUNK
    nxt = gate_up(0)
    for c in range(nf):
        gate, up = nxt
        if c + 1 < nf:
            nxt = gate_up(c + 1)
        mid = (gate * jax.nn.sigmoid(gate) * up).astype(BF16)
        acc = acc + jnp.dot(mid, wd_ref[FF_CHUNK * c:FF_CHUNK * (c + 1), :], preferred_element_type=F32)
    return acc


def _dense_ffn_kernel(h_ref, g_ref, wg_ref, wu_ref, wd_ref, out_ref):
    x = h_ref[...]
    hb = _rms(x, g_ref[...]).astype(BF16)
    out_ref[...] = _swiglu_acc(hb, wg_ref, wu_ref, wd_ref, x)


def _dense_ffn(h, g, wg, wu, wd):
    R = h.shape[0]
    dff = wg.shape[1]
    rows = lambda i: (i, 0)
    const = lambda i: (0, 0)
    return pl.pallas_call(
        _dense_ffn_kernel,
        grid=(R // TM,),
        in_specs=[
            pl.BlockSpec((TM, D_MODEL), rows), pl.BlockSpec((1, D_MODEL), const),
            pl.BlockSpec((D_MODEL, dff), const), pl.BlockSpec((D_MODEL, dff), const),
            pl.BlockSpec((dff, D_MODEL), const),
        ],
        out_specs=pl.BlockSpec((TM, D_MODEL), rows),
        out_shape=jax.ShapeDtypeStruct((R, D_MODEL), F32),
        compiler_params=_cparams(("arbitrary",)),
        name="dense_ffn",
    )(h, g, wg, wu, wd)


def _route_kernel(h_ref, g_ref, wr_ref, hn_ref, info_ref, cnt_ref, carry_ref):
    i = pl.program_id(0)

    @pl.when(i == 0)
    def _():
        carry_ref[...] = jnp.zeros_like(carry_ref)

    hn = _rms(h_ref[...], g_ref[...])
    hn_ref[...] = hn
    h_hi, h_mid, _ = _split3(hn)
    logits = (jnp.dot(h_hi.astype(BF16), wr_ref[0], preferred_element_type=F32)
              + jnp.dot(h_mid.astype(BF16), wr_ref[0], preferred_element_type=F32)
              + jnp.dot(h_hi.astype(BF16), wr_ref[1], preferred_element_type=F32))
    lane = lax.broadcasted_iota(jnp.int32, logits.shape, 1)
    logits = jnp.where(lane < N_EXPERTS, logits, -jnp.inf)
    v1 = jnp.max(logits, axis=-1, keepdims=True)
    e1 = jnp.min(jnp.where(logits == v1, lane, LANES), axis=-1, keepdims=True)
    rest = jnp.where(lane == e1, -jnp.inf, logits)
    v2 = jnp.max(rest, axis=-1, keepdims=True)
    e2 = jnp.min(jnp.where(rest == v2, lane, LANES), axis=-1, keepdims=True)
    ex = jnp.exp(v2 - v1)
    w1 = 1.0 / (1.0 + ex)
    w2 = ex / (1.0 + ex)
    hot1 = jnp.where(lane == e1, 1.0, 0.0)
    hot2 = jnp.where(lane == e2, 1.0, 0.0)
    hot = hot1 + hot2
    before = _tri_cumsum(hot, inclusive=False) + carry_ref[...]
    r1 = jnp.sum(before * hot1, axis=-1, keepdims=True)
    r2 = jnp.sum(before * hot2, axis=-1, keepdims=True)
    total = before[TM - 1:TM, :] + hot[TM - 1:TM, :]
    carry_ref[...] = total
    cnt_ref[...] = jnp.broadcast_to(total, cnt_ref.shape)
    info_ref[...] = jnp.where(lane == 0, e1.astype(F32), jnp.where(lane == 1, e2.astype(F32),
                              jnp.where(lane == 2, r1, jnp.where(lane == 3, r2,
                                        jnp.where(lane == 4, w1, jnp.where(lane == 5, w2, 0.0))))))


def _route(h, g, wr):
    R = h.shape[0]
    rows = lambda i: (i, 0)
    return pl.pallas_call(
        _route_kernel,
        grid=(R // TM,),
        in_specs=[pl.BlockSpec((TM, D_MODEL), rows), pl.BlockSpec((1, D_MODEL), lambda i: (0, 0)),
                  pl.BlockSpec((2, D_MODEL, LANES), lambda i: (0, 0, 0))],
        out_specs=[pl.BlockSpec((TM, D_MODEL), rows), pl.BlockSpec((TM, LANES), rows),
                   pl.BlockSpec((8, LANES), lambda i: (0, 0))],
        out_shape=[jax.ShapeDtypeStruct((R, D_MODEL), F32), jax.ShapeDtypeStruct((R, LANES), F32),
                   jax.ShapeDtypeStruct((8, LANES), F32)],
        scratch_shapes=[pltpu.VMEM((1, LANES), F32)],
        compiler_params=_cparams(("arbitrary",)),
        name="route",
    )(h, g, wr)


def _scatter_kernel(dest_ref, last_ref, src_ref, out_ref, zero_ref, sem):
    @pl.when(pl.program_id(0) == 0)
    def _():
        zero_ref[...] = jnp.zeros_like(zero_ref)

        def zero_copy(e):
            row = pl.multiple_of(last_ref[0, e] * TME, TME)
            return pltpu.make_async_copy(zero_ref, out_ref.at[pl.ds(row, TME)], sem)

        for e in range(N_EXPERTS):
            @pl.when(last_ref[1, e] > 0)
            def _(e=e):
                zero_copy(e).start()
        for e in range(N_EXPERTS):
            @pl.when(last_ref[1, e] > 0)
            def _(e=e):
                zero_copy(e).wait()

        def spare_copy(tile):
            return pltpu.make_async_copy(zero_ref, out_ref.at[pl.ds(pl.multiple_of(tile * TME, TME), TME)], sem)

        def start_spare(tile, c):
            spare_copy(tile).start()
            return c

        def wait_spare(tile, c):
            spare_copy(tile).wait()
            return c

        n_tiles = out_ref.shape[0] // TME
        lax.fori_loop(last_ref[2, 0], n_tiles, start_spare, 0)
        lax.fori_loop(last_ref[2, 0], n_tiles, wait_spare, 0)

    def copy(r, k):
        return pltpu.make_async_copy(src_ref.at[pl.ds(r, 1)],
                                     out_ref.at[pl.ds(dest_ref[0, 0, 2 * r + k], 1)], sem)

    def issue(r, c):
        copy(r, 0).start()
        copy(r, 1).start()
        return c

    lax.fori_loop(0, TM, issue, 0, unroll=DMA_UNROLL)
    for _ in range(2):
        pltpu.make_async_copy(src_ref, out_ref.at[pl.ds(0, TM)], sem).wait()


def _scatter_rows(dest, last_tiles, src, n_rows):
    R = src.shape[0]
    return pl.pallas_call(
        _scatter_kernel,
        grid=(R // TM,),
        in_specs=[pl.BlockSpec((1, 1, 2 * TM), lambda i: (i, 0, 0), memory_space=pltpu.SMEM),
                  pl.BlockSpec(memory_space=pltpu.SMEM),
                  pl.BlockSpec((TM, D_MODEL), lambda i: (i, 0))],
        out_specs=pl.BlockSpec(memory_space=pl.ANY),
        out_shape=jax.ShapeDtypeStruct((n_rows, D_MODEL), src.dtype),
        scratch_shapes=[pltpu.VMEM((TME, D_MODEL), F32), pltpu.SemaphoreType.DMA(())],
        compiler_params=pltpu.CompilerParams(dimension_semantics=("arbitrary",), vmem_limit_bytes=VMEM_LIMIT,
                                             has_side_effects=True),
        name="scatter_rows",
    )(dest, last_tiles, src)


def _expert_kernel(te_ref, act_ref, x_ref, wg_ref, wu_ref, wd_ref, y_ref):
    i = pl.program_id(0)
    f = pl.program_id(1)
    del te_ref

    @pl.when(f == 0)
    def _():
        y_ref[...] = jnp.zeros_like(y_ref)

    @pl.when(act_ref[i] > 0)
    def _():
        y_ref[...] = _swiglu_acc(x_ref[...].astype(BF16), wg_ref, wu_ref, wd_ref, y_ref[...])


def _experts(tile_expert, tile_active, xs, wg, wu, wd):
    mt = tile_expert.shape[0]
    dffe = wg.shape[2]
    grid_spec = pltpu.PrefetchScalarGridSpec(
        num_scalar_prefetch=2,
        grid=(mt, dffe // TF_MOE),
        in_specs=[
            pl.BlockSpec((TME, D_MODEL), lambda i, f, te, act: (i, 0)),
            pl.BlockSpec((None, D_MODEL, TF_MOE), lambda i, f, te, act: (te[i], 0, f)),
            pl.BlockSpec((None, D_MODEL, TF_MOE), lambda i, f, te, act: (te[i], 0, f)),
            pl.BlockSpec((None, TF_MOE, D_MODEL), lambda i, f, te, act: (te[i], f, 0)),
        ],
        out_specs=pl.BlockSpec((TME, D_MODEL), lambda i, f, te, act: (i, 0)),
    )
    return pl.pallas_call(
        _expert_kernel,
        grid_spec=grid_spec,
        out_shape=jax.ShapeDtypeStruct((mt * TME, D_MODEL), F32),
        compiler_params=_cparams(("arbitrary", "arbitrary")),
        name="experts",
    )(tile_expert, tile_active, xs, wg, wu, wd)


def _combine_kernel(dest_ref, h_ref, info_ref, y_ref, out_ref, buf_ref, sem):
    def copy(r, k):
        return pltpu.make_async_copy(y_ref.at[pl.ds(dest_ref[0, 0, 2 * r + k], 1)],
                                     buf_ref.at[k, pl.ds(r, 1)], sem)

    def issue(r, c):
        copy(r, 0).start()
        copy(r, 1).start()
        return c

    lax.fori_loop(0, TM, issue, 0, unroll=DMA_UNROLL)
    for k in range(2):
        pltpu.make_async_copy(y_ref.at[pl.ds(0, TM)], buf_ref.at[k], sem).wait()
    info = info_ref[...]
    out_ref[...] = h_ref[...] + info[:, 4:5] * buf_ref[0] + info[:, 5:6] * buf_ref[1]


def _combine(dest, h, info, y):
    R = h.shape[0]
    rows = lambda i: (i, 0)
    return pl.pallas_call(
        _combine_kernel,
        grid=(R // TM,),
        in_specs=[pl.BlockSpec((1, 1, 2 * TM), lambda i: (i, 0, 0), memory_space=pltpu.SMEM),
                  pl.BlockSpec((TM, D_MODEL), rows), pl.BlockSpec((TM, LANES), rows),
                  pl.BlockSpec(memory_space=pl.ANY)],
        out_specs=pl.BlockSpec((TM, D_MODEL), rows),
        out_shape=jax.ShapeDtypeStruct((R, D_MODEL), F32),
        scratch_shapes=[pltpu.VMEM((2, TM, D_MODEL), F32), pltpu.SemaphoreType.DMA(())],
        compiler_params=_cparams(("arbitrary",)),
        name="combine",
    )(dest, h, info, y)


def _moe_ffn(h, g, wr, wg, wu, wd):
    R = h.shape[0]
    hn, info, counts = _route(h, g, wr)
    cnt = counts[0, :N_EXPERTS].astype(jnp.int32)
    padded = ((cnt + TME - 1) // TME) * TME
    ends = jnp.cumsum(padded)
    starts = ends - padded
    mt = (2 * R) // TME + N_EXPERTS
    tile_row = jnp.arange(mt, dtype=jnp.int32) * TME
    tile_expert = jnp.minimum(jnp.searchsorted(ends, tile_row, side="right"), N_EXPERTS - 1).astype(jnp.int32)
    tile_active = (tile_row < ends[-1]).astype(jnp.int32)
    e12 = info[:, 0:2].astype(jnp.int32)
    dest = (starts[e12] + info[:, 2:4].astype(jnp.int32)).reshape(R // TM, 1, 2 * TM)
    last_tiles = jnp.stack([jnp.maximum(ends // TME - 1, 0), (padded > 0).astype(jnp.int32),
                            jnp.broadcast_to(ends[-1] // TME, (N_EXPERTS,))]).astype(jnp.int32)
    xs = _scatter_rows(dest, last_tiles, hn, mt * TME)
    y = _experts(tile_expert, tile_active, xs, wg, wu, wd)
    return _combine(dest, h, info, y)


def _final_kernel(h_ref, g_ref, out_ref):
    out_ref[...] = _rms(h_ref[...], g_ref[...])


def _final_norm(h, g, n_rows):
    rows = lambda i: (i, 0)
    return pl.pallas_call(
        _final_kernel,
        grid=(n_rows // TM,),
        in_specs=[pl.BlockSpec((TM, D_MODEL), rows), pl.BlockSpec((1, D_MODEL), lambda i: (0, 0))],
        out_specs=pl.BlockSpec((TM, D_MODEL), rows),
        out_shape=jax.ShapeDtypeStruct((n_rows, D_MODEL), F32),
        compiler_params=_cparams(("arbitrary",)),
        name="final_norm",
    )(h, g)


def _rope_table(S):
    rd = DIFF_DH // 4
    inv = ROPE_THETA ** (-jnp.arange(0, rd, 2, dtype=F32) / rd)
    pos = jnp.concatenate([jnp.arange(N_META, N_META + S, dtype=F32), jnp.arange(TM, dtype=F32)])
    ang = pos[:, None] * inv[None, :]
    cos, sin = jnp.cos(ang), jnp.sin(ang)
    n = pos.shape[0]
    cos_t = jnp.tile(jnp.concatenate([cos, cos, jnp.ones((n, 48), F32)], axis=1), (1, 2))
    sin_lo = jnp.tile(jnp.concatenate([-sin, jnp.zeros((n, 56), F32)], axis=1), (1, 2))
    sin_hi = jnp.tile(jnp.concatenate([jnp.zeros((n, 8), F32), sin, jnp.zeros((n, 48), F32)], axis=1), (1, 2))
    return jnp.concatenate([cos_t, sin_lo, sin_hi], axis=1)


def kernel(x, meta_tokens, norm_mix_g, w_in, b_forget, diff_lambda, diff_subln_g, w_branch_diff, w_branch_fox,
           w_out, norm_ffn_g, ffn_w_gate, ffn_w_up, ffn_w_down, moe_router, moe_w_gate, moe_w_up, moe_w_down,
           final_norm_g):
    B, S, D = x.shape
    depth = w_in.shape[0]
    assert D == D_MODEL and S % TQ == 0 and meta_tokens.shape[0] == N_META
    nq = S // TM
    h = jnp.concatenate([x.reshape(B * S, D), meta_tokens.astype(x.dtype),
                         jnp.zeros((TQ - N_META, D), x.dtype)], axis=0)
    rope = _rope_table(S)
    scale = DIFF_DH ** -0.5 * LOG2E
    for layer in range(depth):
        lam_init = 0.8 - 0.6 * math.exp(-0.3 * layer)
        w = w_in[layer]
        dq, dk, dv, fq, fk, fv, ff, ga, gb = jnp.split(
            w, [HALF, 2 * HALF, 3 * HALF, 4 * HALF, 5 * HALF, 6 * HALF, 6 * HALF + FOX_HEADS,
                6 * HALF + FOX_HEADS + D_MODEL], axis=1)
        w1 = jnp.concatenate([dq * scale, dk], axis=1).astype(BF16)
        w2 = jnp.concatenate([dv, fv], axis=1).T.astype(BF16)
        w3 = jnp.concatenate([fq * scale, fk], axis=1).astype(BF16)
        w4 = jnp.pad(ff, ((0, 0), (0, LANES - FOX_HEADS))).astype(BF16)
        bfp = jnp.pad(b_forget[layer].astype(F32), (0, LANES - FOX_HEADS)).reshape(1, LANES)
        g_mix = norm_mix_g[layer].astype(F32).reshape(1, D)
        dq_a, dk_a, vt_a, fq_a, fk_a = _inproj(h, g_mix, w1, w2, w3, w4, bfp, rope, nq=nq)

        lp = diff_lambda[layer].astype(F32)
        lam = jnp.exp(jnp.sum(lp[0] * lp[1])) - jnp.exp(jnp.sum(lp[2] * lp[3])) + lam_init
        par = jnp.zeros((8, LANES), F32)
        par = par.at[1].set(diff_subln_g[layer].astype(F32)).at[2].set(1.0 - lam_init).at[3].set(lam)
        o_a = _attention(dq_a, dk_a, vt_a, par, diff=True, nq=nq, nb=B)
        o_b = _attention(fq_a, fk_a, vt_a, None, diff=False, nq=nq, nb=B)

        wgate = jnp.concatenate([ga, gb], axis=1).astype(BF16)
        h = _mixout(h, o_a, o_b, g_mix, wgate, w_branch_diff[layer].astype(BF16),
                    w_branch_fox[layer].astype(BF16), w_out[layer].astype(BF16))

        g_ffn = norm_ffn_g[layer].astype(F32).reshape(1, D)
        jj = layer // 2
        if layer % 2 == 0:
            h = _dense_ffn(h, g_ffn, ffn_w_gate[jj].astype(BF16), ffn_w_up[jj].astype(BF16),
                           ffn_w_down[jj].astype(BF16))
        else:
            r_hi, r_mid, _ = _split3(jnp.pad(moe_router[jj].astype(F32), ((0, 0), (0, LANES - N_EXPERTS))))
            wr = jnp.stack([r_hi, r_mid]).astype(BF16)
            h = _moe_ffn(h, g_ffn, wr, moe_w_gate[jj].astype(BF16), moe_w_up[jj].astype(BF16),
                         moe_w_down[jj].astype(BF16))
    out = _final_norm(h, final_norm_g.astype(F32).reshape(1, D), B * S)
    return out.reshape(B, S, D)
```

```python
import functools
import math

import jax
import jax.numpy as jnp
from jax import lax
from jax.experimental import pallas as pl
from jax.experimental.pallas import tpu as pltpu

D_MODEL = 1024
N_META = 16
ROPE_THETA = 500000.0
RMS_EPS = 1e-6
NEG_INF = -1e30

DIFF_HEADS = 4
DIFF_DH = 64
FOX_HEADS = 8
FOX_DH = 64
HALF = 512
N_EXPERTS = 8
LANES = 128

TM = 512
KEY_TILES_PER_Q = 2
TQ = KEY_TILES_PER_Q * TM
TME = 512
TF_MOE = 1792
FF_CHUNK = 256
DMA_UNROLL = 8
LOG2E = 1.4426950408889634
ONES_ROWS = 16
VMEM_LIMIT = 56 * 1024 * 1024

F32 = jnp.float32
BF16 = jnp.bfloat16


def _cparams(sem):
    return pltpu.CompilerParams(dimension_semantics=sem, vmem_limit_bytes=VMEM_LIMIT)


def _rms(x, g):
    ms = jnp.mean(x * x, axis=-1, keepdims=True)
    return x * lax.rsqrt(ms + RMS_EPS) * g


def _split3(x):
    hi = x.astype(BF16).astype(F32)
    r = x - hi
    mid = r.astype(BF16).astype(F32)
    lo = (r - mid).astype(BF16).astype(F32)
    return hi, mid, lo


def _tri_cumsum(x, inclusive):
    n = x.shape[0]
    row = lax.broadcasted_iota(jnp.int32, (n, n), 0)
    col = lax.broadcasted_iota(jnp.int32, (n, n), 1)
    tri = jnp.where((col <= row) if inclusive else (col < row), 1.0, 0.0).astype(BF16)
    out = jnp.zeros(x.shape, F32)
    for part in _split3(x):
        out = out + jnp.dot(tri, part.astype(BF16), preferred_element_type=F32)
    return out


def _inproj_kernel(h_ref, g_ref, w1_ref, w2_ref, w3_ref, w4_ref, bf_ref, rope_ref,
                   dq_ref, dk_ref, vt_ref, fq_ref, fk_ref, carry_ref, mcarry_ref, *, nq):
    i = pl.program_id(0)

    @pl.when(i == 0)
    def _():
        carry_ref[...] = jnp.zeros_like(carry_ref)
        mcarry_ref[...] = jnp.zeros_like(mcarry_ref)

    hb = _rms(h_ref[...], g_ref[...]).astype(BF16)

    z1 = jnp.dot(hb, w1_ref[...], preferred_element_type=F32)
    cos_t = rope_ref[:, 0:LANES]
    sin_lo = rope_ref[:, LANES:2 * LANES]
    sin_hi = rope_ref[:, 2 * LANES:3 * LANES]
    for j in range(8):
        zj = z1[:, LANES * j:LANES * (j + 1)]
        rot = zj * cos_t + pltpu.roll(zj, LANES - 8, 1) * sin_lo + pltpu.roll(zj, 8, 1) * sin_hi
        dst = dq_ref if j < 4 else dk_ref
        dst[:, LANES * (j % 4):LANES * (j % 4 + 1)] = rot.astype(BF16)

    vt_ref[0] = lax.dot_general(w2_ref[...], hb, (((1,), (1,)), ((), ())),
                                preferred_element_type=F32).astype(BF16)

    z4 = jnp.dot(hb, w4_ref[...], preferred_element_type=F32) + bf_ref[...]
    lane = lax.broadcasted_iota(jnp.int32, z4.shape, 1)
    logf = jnp.minimum(z4, 0.0) - jnp.log1p(jnp.exp(-jnp.abs(z4)))
    logf = jnp.where(lane < FOX_HEADS, logf, 0.0)
    j_in_batch = lax.rem(jnp.maximum(i - 1, 0), nq)
    base = jnp.where(i == 0, 0.0, jnp.where(j_in_batch == 0, mcarry_ref[...], carry_ref[...]))
    c = _tri_cumsum(logf, inclusive=True) + base

    @pl.when(i == 0)
    def _():
        mcarry_ref[...] = c[N_META - 1:N_META, :]

    carry_ref[...] = c[TM - 1:TM, :]

    z3 = jnp.dot(hb, w3_ref[...], preferred_element_type=F32)
    parts = _split3(c * LOG2E)
    for hd in range(FOX_HEADS):
        off = 64 if hd % 2 == 0 else 0
        cols = [p[:, hd:hd + 1] for p in parts]
        one_mask = (lane >= off + 3) & (lane < off + 6)
        aug_q = jnp.where(lane == off, cols[0], jnp.where(lane == off + 1, cols[1],
                          jnp.where(lane == off + 2, cols[2], jnp.where(one_mask, 1.0, 0.0))))
        aug_k = jnp.where(lane == off + 3, -cols[0], jnp.where(lane == off + 4, -cols[1],
                          jnp.where(lane == off + 5, -cols[2],
                                    jnp.where((lane >= off) & (lane < off + 3), 1.0, 0.0))))
        slab = hd // 2
        keep = (lane < 64) if hd % 2 == 0 else (lane >= 64)
        zq = z3[:, LANES * slab:LANES * (slab + 1)]
        zk = z3[:, HALF + LANES * slab:HALF + LANES * (slab + 1)]
        fq_ref[:, LANES * hd:LANES * (hd + 1)] = jnp.where(keep, zq, aug_q).astype(BF16)
        fk_ref[:, LANES * hd:LANES * (hd + 1)] = jnp.where(keep, zk, aug_k).astype(BF16)


def _inproj(h, g, w1, w2, w3, w4, bfp, rope, *, nq):
    R = h.shape[0]
    nt = R // TM
    n_real = nt - TQ // TM
    rows = lambda i: (jnp.where(i == 0, n_real, jnp.where(i <= n_real, i - 1, i)), 0)
    rope_rows = lambda i: (jnp.where(i == 0, nq, lax.rem(jnp.maximum(i - 1, 0), nq)), 0)
    const = lambda i: (0, 0)
    out_sd = lambda w: jax.ShapeDtypeStruct((R, w), BF16)
    return pl.pallas_call(
        functools.partial(_inproj_kernel, nq=nq),
        grid=(nt,),
        in_specs=[
            pl.BlockSpec((TM, D_MODEL), rows),
            pl.BlockSpec((1, D_MODEL), const),
            pl.BlockSpec((D_MODEL, 2 * HALF), const),
            pl.BlockSpec((2 * HALF, D_MODEL), const),
            pl.BlockSpec((D_MODEL, 2 * HALF), const),
            pl.BlockSpec((D_MODEL, LANES), const),
            pl.BlockSpec((1, LANES), const),
            pl.BlockSpec((TM, 3 * LANES), rope_rows),
        ],
        out_specs=[
            pl.BlockSpec((TM, HALF), rows), pl.BlockSpec((TM, HALF), rows),
            pl.BlockSpec((1, 2 * HALF, TM), lambda i: (rows(i)[0], 0, 0)),
            pl.BlockSpec((TM, 2 * HALF), rows), pl.BlockSpec((TM, 2 * HALF), rows),
        ],
        out_shape=[out_sd(HALF), out_sd(HALF), jax.ShapeDtypeStruct((nt, 2 * HALF, TM), BF16),
                   out_sd(2 * HALF), out_sd(2 * HALF)],
        scratch_shapes=[pltpu.VMEM((1, LANES), F32), pltpu.VMEM((1, LANES), F32)],
        compiler_params=_cparams(("arbitrary",)),
        name="inproj",
    )(h, g, w1, w2, w3, w4, bfp, rope)


def _attn_kernel(*refs, diff, nq, nb):
    if diff:
        q_ref, k_ref, vt_ref, km_ref, vtm_ref, par_ref, o_ref, acc_ref, m_ref, q_scr, sa_ref, sb_ref = refs
    else:
        q_ref, k_ref, vt_ref, km_ref, vtm_ref, o_ref, acc_ref, m_ref, q_scr, sa_ref, sb_ref = refs
    t = pl.program_id(1)
    nqt = nq // KEY_TILES_PER_Q
    is_real = t < nb * nqt
    jq = lax.rem(t, nqt)
    dv = acc_ref.shape[1] - ONES_ROWS

    for sub in range(2):
        if diff:
            q = q_ref[...]
            lane = lax.broadcasted_iota(jnp.int32, q.shape, 1)
            q_scr[sub] = jnp.where((lane < 64) if sub == 0 else (lane >= 64), q, jnp.zeros_like(q))
        else:
            q_scr[sub] = q_ref[:, LANES * sub:LANES * (sub + 1)]
        m_ref[sub] = jnp.full(m_ref.shape[1:], NEG_INF, F32)
        acc_ref[sub] = jnp.zeros(acc_ref.shape[1:], F32)

    def keys_of(sub, k_tile):
        return k_tile if diff else k_tile[:, LANES * sub:LANES * (sub + 1)]

    def values_of(sub, vt_tile):
        ones = jnp.ones((ONES_ROWS, vt_tile.shape[1]), BF16)
        vt = vt_tile if diff else vt_tile[dv * sub:dv * (sub + 1), :]
        return jnp.concatenate([vt, ones], axis=0)

    def scores(sub, k, q0):
        return lax.dot_general(k, q_scr[sub, q0:, :], (((1,), (1,)), ((), ())), preferred_element_type=F32)

    def update(sub, st, vt, mask, q0):
        if mask is not None:
            st = jnp.where(mask, st, NEG_INF)
        m_prev = m_ref[sub, :, q0:]
        m_new = jnp.maximum(m_prev, jnp.max(st, axis=0, keepdims=True))
        p = jnp.exp2(st - m_new).astype(BF16)
        if p.shape[0] < vt.shape[1]:
            p = jnp.concatenate([p, jnp.zeros((vt.shape[1] - p.shape[0], p.shape[1]), BF16)], axis=0)
        acc_ref[sub, :, q0:] = (jnp.exp2(m_prev - m_new) * acc_ref[sub, :, q0:]
                                + jnp.dot(vt, p, preferred_element_type=F32))
        m_ref[sub, :, q0:] = m_new

    def scores_into(buf, tile, q0):
        k_tile = k_ref[pl.ds(pl.multiple_of(tile * TM, TM), TM), :]
        for sub in range(2):
            buf[sub, :, q0:] = scores(sub, keys_of(sub, k_tile), q0)

    def update_from(buf, tile, causal, q0):
        vt_tile = vt_ref[tile]
        mask = None
        if causal:
            key = lax.broadcasted_iota(jnp.int32, (TM, TQ - q0), 0)
            qry = lax.broadcasted_iota(jnp.int32, (TM, TQ - q0), 1)
            mask = key <= qry
        for sub in range(2):
            update(sub, buf[sub, :, q0:], values_of(sub, vt_tile), mask, q0)

    key = lax.broadcasted_iota(jnp.int32, (N_META, TQ), 0)
    qry = lax.broadcasted_iota(jnp.int32, (N_META, TQ), 1)
    meta_mask = key <= jnp.where(is_real, N_META - 1, qry)
    km_tile = km_ref[...]
    vtm_tile = vtm_ref[0][:, :LANES]
    meta_scores = [scores(sub, keys_of(sub, km_tile), 0) for sub in range(2)]

    scores_into(sa_ref, 0, 0)
    for sub in range(2):
        update(sub, meta_scores[sub], values_of(sub, vtm_tile), meta_mask, 0)

    def pair(i, carry):
        scores_into(sb_ref, 2 * i + 1, 0)
        update_from(sa_ref, 2 * i, False, 0)
        scores_into(sa_ref, 2 * i + 2, 0)
        update_from(sb_ref, 2 * i + 1, False, 0)
        return carry

    lax.fori_loop(0, jnp.where(is_real, jq, 0), pair, 0)

    @pl.when(is_real)
    def _():
        scores_into(sb_ref, 2 * jq + 1, TM)
        update_from(sa_ref, 2 * jq, True, 0)
        update_from(sb_ref, 2 * jq + 1, True, TM)

    a0 = acc_ref[0]
    a1 = acc_ref[1]
    o0 = a0[:dv] / a0[dv:dv + 1]
    o1 = a1[:dv] / a1[dv:dv + 1]
    if diff:
        d = (o0 - par_ref[3:4, 0:1] * o1).T
        o_ref[...] = (_rms(d, par_ref[1:2, :]) * par_ref[2:3, :]).astype(BF16)
    else:
        o_ref[...] = jnp.concatenate([o0, o1], axis=0).T.astype(BF16)


def _attention(q, k, vt, par, *, diff, nq, nb):
    R = q.shape[0]
    S = nq * TM
    ntq = R // TQ
    qw = LANES if diff else 2 * LANES
    voff = 0 if diff else HALF // LANES
    meta_blk = (nb * S) // N_META
    batch_of = lambda t: jnp.minimum(t // (nq // KEY_TILES_PER_Q), nb - 1)
    in_specs = [
        pl.BlockSpec((TQ, qw), lambda p, t: (t, p)),
        pl.BlockSpec((S, qw), lambda p, t: (batch_of(t), p)),
        pl.BlockSpec((nq, LANES, TM), lambda p, t: (batch_of(t), p + voff, 0)),
        pl.BlockSpec((N_META, qw), lambda p, t: (meta_blk, p)),
        pl.BlockSpec((1, LANES, TM), lambda p, t: (nb * nq, p + voff, 0)),
    ]
    args = [q, k, vt, k, vt]
    if diff:
        in_specs.append(pl.BlockSpec((8, LANES), lambda p, t: (0, 0)))
        args.append(par)
    acc_rows = (LANES if diff else LANES // 2) + ONES_ROWS
    return pl.pallas_call(
        functools.partial(_attn_kernel, diff=diff, nq=nq, nb=nb),
        grid=(4, ntq),
        in_specs=in_specs,
        out_specs=pl.BlockSpec((TQ, LANES), lambda p, t: (t, p)),
        out_shape=jax.ShapeDtypeStruct((R, HALF), BF16),
        scratch_shapes=[pltpu.VMEM((2, acc_rows, TQ), F32), pltpu.VMEM((2, 1, TQ), F32),
                        pltpu.VMEM((2, TQ, LANES), BF16),
                        pltpu.VMEM((2, TM, TQ), F32), pltpu.VMEM((2, TM, TQ), F32)],
        compiler_params=_cparams(("arbitrary", "arbitrary")),
        name="diff_attn" if diff else "fox_attn",
    )(*args)


def _mixout_kernel(h_ref, oa_ref, ob_ref, g_ref, wg_ref, wbd_ref, wbf_ref, wo_ref, out_ref):
    x = h_ref[...]
    hb = _rms(x, g_ref[...]).astype(BF16)
    gates = jax.nn.sigmoid(jnp.dot(hb, wg_ref[...], preferred_element_type=F32))
    a = jnp.dot(oa_ref[...], wbd_ref[...], preferred_element_type=F32)
    b = jnp.dot(ob_ref[...], wbf_ref[...], preferred_element_type=F32)
    merged = gates[:, :D_MODEL] * a + gates[:, D_MODEL:] * b
    out_ref[...] = x + jnp.dot(merged.astype(BF16), wo_ref[...], preferred_element_type=F32)


def _mixout(h, oa, ob, g, wg, wbd, wbf, wo):
    R = h.shape[0]
    rows = lambda i: (i, 0)
    const = lambda i: (0, 0)
    return pl.pallas_call(
        _mixout_kernel,
        grid=(R // TM,),
        in_specs=[
            pl.BlockSpec((TM, D_MODEL), rows), pl.BlockSpec((TM, HALF), rows), pl.BlockSpec((TM, HALF), rows),
            pl.BlockSpec((1, D_MODEL), const), pl.BlockSpec((D_MODEL, 2 * D_MODEL), const),
            pl.BlockSpec((HALF, D_MODEL), const), pl.BlockSpec((HALF, D_MODEL), const),
            pl.BlockSpec((D_MODEL, D_MODEL), const),
        ],
        out_specs=pl.BlockSpec((TM, D_MODEL), rows),
        out_shape=jax.ShapeDtypeStruct((R, D_MODEL), F32),
        compiler_params=_cparams(("arbitrary",)),
        name="mixout",
    )(h, oa, ob, g, wg, wbd, wbf, wo)


def _swiglu_acc(xb, wg_ref, wu_ref, wd_ref, acc):
    def gate_up(c):
        sl = slice(FF_CHUNK * c, FF_CHUNK * (c + 1))
        return (jnp.dot(xb, wg_ref[:, sl], preferred_element_type=F32),
                jnp.dot(xb, wu_ref[:, sl], preferred_element_type=F32))

    nf = wg_ref.shape[1] // FF_CHUNK
    nxt = gate_up(0)
    for c in range(nf):
        gate, up = nxt
        if c + 1 < nf:
            nxt = gate_up(c + 1)
        mid = (gate * jax.nn.sigmoid(gate) * up).astype(BF16)
        acc = acc + jnp.dot(mid, wd_ref[FF_CHUNK * c:FF_CHUNK * (c + 1), :], preferred_element_type=F32)
    return acc


def _dense_ffn_kernel(h_ref, g_ref, wg_ref, wu_ref, wd_ref, out_ref):
    x = h_ref[...]
    hb = _rms(x, g_ref[...]).astype(BF16)
    out_ref[...] = _swiglu_acc(hb, wg_ref, wu_ref, wd_ref, x)


def _dense_ffn(h, g, wg, wu, wd):
    R = h.shape[0]
    dff = wg.shape[1]
    rows = lambda i: (i, 0)
    const = lambda i: (0, 0)
    return pl.pallas_call(
        _dense_ffn_kernel,
        grid=(R // TM,),
        in_specs=[
            pl.BlockSpec((TM, D_MODEL), rows), pl.BlockSpec((1, D_MODEL), const),
            pl.BlockSpec((D_MODEL, dff), const), pl.BlockSpec((D_MODEL, dff), const),
            pl.BlockSpec((dff, D_MODEL), const),
        ],
        out_specs=pl.BlockSpec((TM, D_MODEL), rows),
        out_shape=jax.ShapeDtypeStruct((R, D_MODEL), F32),
        compiler_params=_cparams(("arbitrary",)),
        name="dense_ffn",
    )(h, g, wg, wu, wd)


def _route_kernel(h_ref, g_ref, wr_ref, hn_ref, info_ref, cnt_ref, carry_ref):
    i = pl.program_id(0)

    @pl.when(i == 0)
    def _():
        carry_ref[...] = jnp.zeros_like(carry_ref)

    hn = _rms(h_ref[...], g_ref[...])
    hn_ref[...] = hn
    h_hi, h_mid, _ = _split3(hn)
    logits = (jnp.dot(h_hi.astype(BF16), wr_ref[0], preferred_element_type=F32)
              + jnp.dot(h_mid.astype(BF16), wr_ref[0], preferred_element_type=F32)
              + jnp.dot(h_hi.astype(BF16), wr_ref[1], preferred_element_type=F32))
    lane = lax.broadcasted_iota(jnp.int32, logits.shape, 1)
    logits = jnp.where(lane < N_EXPERTS, logits, -jnp.inf)
    v1 = jnp.max(logits, axis=-1, keepdims=True)
    e1 = jnp.min(jnp.where(logits == v1, lane, LANES), axis=-1, keepdims=True)
    rest = jnp.where(lane == e1, -jnp.inf, logits)
    v2 = jnp.max(rest, axis=-1, keepdims=True)
    e2 = jnp.min(jnp.where(rest == v2, lane, LANES), axis=-1, keepdims=True)
    ex = jnp.exp(v2 - v1)
    w1 = 1.0 / (1.0 + ex)
    w2 = ex / (1.0 + ex)
    hot1 = jnp.where(lane == e1, 1.0, 0.0)
    hot2 = jnp.where(lane == e2, 1.0, 0.0)
    hot = hot1 + hot2
    before = _tri_cumsum(hot, inclusive=False) + carry_ref[...]
    r1 = jnp.sum(before * hot1, axis=-1, keepdims=True)
    r2 = jnp.sum(before * hot2, axis=-1, keepdims=True)
    total = before[TM - 1:TM, :] + hot[TM - 1:TM, :]
    carry_ref[...] = total
    cnt_ref[...] = jnp.broadcast_to(total, cnt_ref.shape)
    info_ref[...] = jnp.where(lane == 0, e1.astype(F32), jnp.where(lane == 1, e2.astype(F32),
                              jnp.where(lane == 2, r1, jnp.where(lane == 3, r2,
                                        jnp.where(lane == 4, w1, jnp.where(lane == 5, w2, 0.0))))))


def _route(h, g, wr):
    R = h.shape[0]
    rows = lambda i: (i, 0)
    return pl.pallas_call(
        _route_kernel,
        grid=(R // TM,),
        in_specs=[pl.BlockSpec((TM, D_MODEL), rows), pl.BlockSpec((1, D_MODEL), lambda i: (0, 0)),
                  pl.BlockSpec((2, D_MODEL, LANES), lambda i: (0, 0, 0))],
        out_specs=[pl.BlockSpec((TM, D_MODEL), rows), pl.BlockSpec((TM, LANES), rows),
                   pl.BlockSpec((8, LANES), lambda i: (0, 0))],
        out_shape=[jax.ShapeDtypeStruct((R, D_MODEL), F32), jax.ShapeDtypeStruct((R, LANES), F32),
                   jax.ShapeDtypeStruct((8, LANES), F32)],
        scratch_shapes=[pltpu.VMEM((1, LANES), F32)],
        compiler_params=_cparams(("arbitrary",)),
        name="route",
    )(h, g, wr)


def _scatter_kernel(dest_ref, last_ref, src_ref, out_ref, zero_ref, sem):
    @pl.when(pl.program_id(0) == 0)
    def _():
        zero_ref[...] = jnp.zeros_like(zero_ref)

        def zero_copy(e):
            row = pl.multiple_of(last_ref[0, e] * TME, TME)
            return pltpu.make_async_copy(zero_ref, out_ref.at[pl.ds(row, TME)], sem)

        for e in range(N_EXPERTS):
            @pl.when(last_ref[1, e] > 0)
            def _(e=e):
                zero_copy(e).start()
        for e in range(N_EXPERTS):
            @pl.when(last_ref[1, e] > 0)
            def _(e=e):
                zero_copy(e).wait()

        def spare_copy(tile):
            return pltpu.make_async_copy(zero_ref, out_ref.at[pl.ds(pl.multiple_of(tile * TME, TME), TME)], sem)

        def start_spare(tile, c):
            spare_copy(tile).start()
            return c

        def wait_spare(tile, c):
            spare_copy(tile).wait()
            return c

        n_tiles = out_ref.shape[0] // TME
        lax.fori_loop(last_ref[2, 0], n_tiles, start_spare, 0)
        lax.fori_loop(last_ref[2, 0], n_tiles, wait_spare, 0)

    def copy(r, k):
        return pltpu.make_async_copy(src_ref.at[pl.ds(r, 1)],
                                     out_ref.at[pl.ds(dest_ref[0, 0, 2 * r + k], 1)], sem)

    def issue(r, c):
        copy(r, 0).start(priority=0)
        copy(r, 1).start(priority=1)
        return c

    lax.fori_loop(0, TM, issue, 0, unroll=DMA_UNROLL)
    for _ in range(2):
        pltpu.make_async_copy(src_ref, out_ref.at[pl.ds(0, TM)], sem).wait()


def _scatter_rows(dest, last_tiles, src, n_rows):
    R = src.shape[0]
    return pl.pallas_call(
        _scatter_kernel,
        grid=(R // TM,),
        in_specs=[pl.BlockSpec((1, 1, 2 * TM), lambda i: (i, 0, 0), memory_space=pltpu.SMEM),
                  pl.BlockSpec(memory_space=pltpu.SMEM),
                  pl.BlockSpec((TM, D_MODEL), lambda i: (i, 0))],
        out_specs=pl.BlockSpec(memory_space=pl.ANY),
        out_shape=jax.ShapeDtypeStruct((n_rows, D_MODEL), src.dtype),
        scratch_shapes=[pltpu.VMEM((TME, D_MODEL), F32), pltpu.SemaphoreType.DMA(())],
        compiler_params=pltpu.CompilerParams(dimension_semantics=("arbitrary",), vmem_limit_bytes=VMEM_LIMIT,
                                             has_side_effects=True),
        name="scatter_rows",
    )(dest, last_tiles, src)


def _expert_kernel(te_ref, act_ref, x_ref, wg_ref, wu_ref, wd_ref, y_ref):
    i = pl.program_id(0)
    f = pl.program_id(1)
    del te_ref

    @pl.when(f == 0)
    def _():
        y_ref[...] = jnp.zeros_like(y_ref)

    @pl.when(act_ref[i] > 0)
    def _():
        y_ref[...] = _swiglu_acc(x_ref[...].astype(BF16), wg_ref, wu_ref, wd_ref, y_ref[...])


def _experts(tile_expert, tile_active, xs, wg, wu, wd):
    mt = tile_expert.shape[0]
    dffe = wg.shape[2]
    grid_spec = pltpu.PrefetchScalarGridSpec(
        num_scalar_prefetch=2,
        grid=(mt, dffe // TF_MOE),
        in_specs=[
            pl.BlockSpec((TME, D_MODEL), lambda i, f, te, act: (i, 0)),
            pl.BlockSpec((None, D_MODEL, TF_MOE), lambda i, f, te, act: (te[i], 0, f)),
            pl.BlockSpec((None, D_MODEL, TF_MOE), lambda i, f, te, act: (te[i], 0, f)),
            pl.BlockSpec((None, TF_MOE, D_MODEL), lambda i, f, te, act: (te[i], f, 0)),
        ],
        out_specs=pl.BlockSpec((TME, D_MODEL), lambda i, f, te, act: (i, 0)),
    )
    return pl.pallas_call(
        _expert_kernel,
        grid_spec=grid_spec,
        out_shape=jax.ShapeDtypeStruct((mt * TME, D_MODEL), F32),
        compiler_params=_cparams(("arbitrary", "arbitrary")),
        name="experts",
    )(tile_expert, tile_active, xs, wg, wu, wd)


def _combine_kernel(dest_ref, h_ref, info_ref, g_ref, y_ref, out_ref, buf_ref, sem, *, final):
    def copy(r, k):
        return pltpu.make_async_copy(y_ref.at[pl.ds(dest_ref[0, 0, 2 * r + k], 1)],
                                     buf_ref.at[k, pl.ds(r, 1)], sem)

    def issue(r, c):
        copy(r, 0).start(priority=0)
        copy(r, 1).start(priority=1)
        return c

    lax.fori_loop(0, TM, issue, 0, unroll=DMA_UNROLL)
    for k in range(2):
        pltpu.make_async_copy(y_ref.at[pl.ds(0, TM)], buf_ref.at[k], sem).wait()
    info = info_ref[...]
    out = h_ref[...] + info[:, 4:5] * buf_ref[0] + info[:, 5:6] * buf_ref[1]
    out_ref[...] = _rms(out, g_ref[...]) if final else out


def _combine(dest, h, info, y, final_g=None, n_rows=None):
    final = final_g is not None
    n_rows = n_rows if final else h.shape[0]
    g = final_g if final else jnp.ones((1, D_MODEL), F32)
    rows = lambda i: (i, 0)
    return pl.pallas_call(
        functools.partial(_combine_kernel, final=final),
        grid=(n_rows // TM,),
        in_specs=[pl.BlockSpec((1, 1, 2 * TM), lambda i: (i, 0, 0), memory_space=pltpu.SMEM),
                  pl.BlockSpec((TM, D_MODEL), rows), pl.BlockSpec((TM, LANES), rows),
                  pl.BlockSpec((1, D_MODEL), lambda i: (0, 0)), pl.BlockSpec(memory_space=pl.ANY)],
        out_specs=pl.BlockSpec((TM, D_MODEL), rows),
        out_shape=jax.ShapeDtypeStruct((n_rows, D_MODEL), F32),
        scratch_shapes=[pltpu.VMEM((2, TM, D_MODEL), F32), pltpu.SemaphoreType.DMA(())],
        compiler_params=_cparams(("arbitrary",)),
        name="combine",
    )(dest, h, info, g, y)


def _moe_ffn(h, g, wr, wg, wu, wd, final_g=None, n_rows=None):
    R = h.shape[0]
    hn, info, counts = _route(h, g, wr)
    cnt = counts[0, :N_EXPERTS].astype(jnp.int32)
    padded = ((cnt + TME - 1) // TME) * TME
    ends = jnp.cumsum(padded)
    starts = ends - padded
    mt = (2 * R) // TME + N_EXPERTS
    tile_row = jnp.arange(mt, dtype=jnp.int32) * TME
    tile_expert = jnp.minimum(jnp.searchsorted(ends, tile_row, side="right"), N_EXPERTS - 1).astype(jnp.int32)
    tile_active = (tile_row < ends[-1]).astype(jnp.int32)
    e12 = info[:, 0:2].astype(jnp.int32)
    dest = (starts[e12] + info[:, 2:4].astype(jnp.int32)).reshape(R // TM, 1, 2 * TM)
    last_tiles = jnp.stack([jnp.maximum(ends // TME - 1, 0), (padded > 0).astype(jnp.int32),
                            jnp.broadcast_to(ends[-1] // TME, (N_EXPERTS,))]).astype(jnp.int32)
    xs = _scatter_rows(dest, last_tiles, hn, mt * TME)
    y = _experts(tile_expert, tile_active, xs, wg, wu, wd)
    return _combine(dest, h, info, y, final_g, n_rows)


def _final_kernel(h_ref, g_ref, out_ref):
    out_ref[...] = _rms(h_ref[...], g_ref[...])


def _final_norm(h, g, n_rows):
    rows = lambda i: (i, 0)
    return pl.pallas_call(
        _final_kernel,
        grid=(n_rows // TM,),
        in_specs=[pl.BlockSpec((TM, D_MODEL), rows), pl.BlockSpec((1, D_MODEL), lambda i: (0, 0))],
        out_specs=pl.BlockSpec((TM, D_MODEL), rows),
        out_shape=jax.ShapeDtypeStruct((n_rows, D_MODEL), F32),
        compiler_params=_cparams(("arbitrary",)),
        name="final_norm",
    )(h, g)


def _rope_table(S):
    rd = DIFF_DH // 4
    inv = ROPE_THETA ** (-jnp.arange(0, rd, 2, dtype=F32) / rd)
    pos = jnp.concatenate([jnp.arange(N_META, N_META + S, dtype=F32), jnp.arange(TM, dtype=F32)])
    ang = pos[:, None] * inv[None, :]
    cos, sin = jnp.cos(ang), jnp.sin(ang)
    n = pos.shape[0]
    cos_t = jnp.tile(jnp.concatenate([cos, cos, jnp.ones((n, 48), F32)], axis=1), (1, 2))
    sin_lo = jnp.tile(jnp.concatenate([-sin, jnp.zeros((n, 56), F32)], axis=1), (1, 2))
    sin_hi = jnp.tile(jnp.concatenate([jnp.zeros((n, 8), F32), sin, jnp.zeros((n, 48), F32)], axis=1), (1, 2))
    return jnp.concatenate([cos_t, sin_lo, sin_hi], axis=1)


def kernel(x, meta_tokens, norm_mix_g, w_in, b_forget, diff_lambda, diff_subln_g, w_branch_diff, w_branch_fox,
           w_out, norm_ffn_g, ffn_w_gate, ffn_w_up, ffn_w_down, moe_router, moe_w_gate, moe_w_up, moe_w_down,
           final_norm_g):
    B, S, D = x.shape
    depth = w_in.shape[0]
    assert D == D_MODEL and S % TQ == 0 and meta_tokens.shape[0] == N_META
    nq = S // TM
    h = jnp.concatenate([x.reshape(B * S, D), meta_tokens.astype(x.dtype),
                         jnp.zeros((TQ - N_META, D), x.dtype)], axis=0)
    rope = _rope_table(S)
    g_final = final_norm_g.astype(F32).reshape(1, D)
    scale = DIFF_DH ** -0.5 * LOG2E
    for layer in range(depth):
        lam_init = 0.8 - 0.6 * math.exp(-0.3 * layer)
        w = w_in[layer]
        dq, dk, dv, fq, fk, fv, ff, ga, gb = jnp.split(
            w, [HALF, 2 * HALF, 3 * HALF, 4 * HALF, 5 * HALF, 6 * HALF, 6 * HALF + FOX_HEADS,
                6 * HALF + FOX_HEADS + D_MODEL], axis=1)
        w1 = jnp.concatenate([dq * scale, dk], axis=1).astype(BF16)
        w2 = jnp.concatenate([dv, fv], axis=1).T.astype(BF16)
        w3 = jnp.concatenate([fq * scale, fk], axis=1).astype(BF16)
        w4 = jnp.pad(ff, ((0, 0), (0, LANES - FOX_HEADS))).astype(BF16)
        bfp = jnp.pad(b_forget[layer].astype(F32), (0, LANES - FOX_HEADS)).reshape(1, LANES)
        g_mix = norm_mix_g[layer].astype(F32).reshape(1, D)
        dq_a, dk_a, vt_a, fq_a, fk_a = _inproj(h, g_mix, w1, w2, w3, w4, bfp, rope, nq=nq)

        lp = diff_lambda[layer].astype(F32)
        lam = jnp.exp(jnp.sum(lp[0] * lp[1])) - jnp.exp(jnp.sum(lp[2] * lp[3])) + lam_init
        par = jnp.zeros((8, LANES), F32)
        par = par.at[1].set(diff_subln_g[layer].astype(F32)).at[2].set(1.0 - lam_init).at[3].set(lam)
        o_a = _attention(dq_a, dk_a, vt_a, par, diff=True, nq=nq, nb=B)
        o_b = _attention(fq_a, fk_a, vt_a, None, diff=False, nq=nq, nb=B)

        wgate = jnp.concatenate([ga, gb], axis=1).astype(BF16)
        h = _mixout(h, o_a, o_b, g_mix, wgate, w_branch_diff[layer].astype(BF16),
                    w_branch_fox[layer].astype(BF16), w_out[layer].astype(BF16))

        g_ffn = norm_ffn_g[layer].astype(F32).reshape(1, D)
        jj = layer // 2
        if layer % 2 == 0:
            h = _dense_ffn(h, g_ffn, ffn_w_gate[jj].astype(BF16), ffn_w_up[jj].astype(BF16),
                           ffn_w_down[jj].astype(BF16))
        else:
            r_hi, r_mid, _ = _split3(jnp.pad(moe_router[jj].astype(F32), ((0, 0), (0, LANES - N_EXPERTS))))
            wr = jnp.stack([r_hi, r_mid]).astype(BF16)
            last = layer == depth - 1
            h = _moe_ffn(h, g_ffn, wr, moe_w_gate[jj].astype(BF16), moe_w_up[jj].astype(BF16),
                         moe_w_down[jj].astype(BF16), g_final if last else None, B * S)
    out = h if depth % 2 == 0 else _final_norm(h, g_final, B * S)
    return out.reshape(B, S, D)
```

```python
import functools
import math

import jax
import jax.numpy as jnp
from jax import lax
from jax.experimental import pallas as pl
from jax.experimental.pallas import tpu as pltpu

D_MODEL = 1024
N_META = 16
ROPE_THETA = 500000.0
RMS_EPS = 1e-6
NEG_INF = -1e30

DIFF_HEADS = 4
DIFF_DH = 64
FOX_HEADS = 8
FOX_DH = 64
HALF = 512
N_EXPERTS = 8
LANES = 128

TM = 512
KEY_TILES_PER_Q = 2
TQ = KEY_TILES_PER_Q * TM
TME = 512
TF_MOE = 1792
FF_CHUNK = 256
DMA_UNROLL = 8
LOG2E = 1.4426950408889634
ONES_ROWS = 16
VMEM_LIMIT = 56 * 1024 * 1024

F32 = jnp.float32
BF16 = jnp.bfloat16


def _cparams(sem):
    return pltpu.CompilerParams(dimension_semantics=sem, vmem_limit_bytes=VMEM_LIMIT)


def _rms(x, g):
    ms = jnp.mean(x * x, axis=-1, keepdims=True)
    return x * lax.rsqrt(ms + RMS_EPS) * g


def _split3(x):
    hi = x.astype(BF16).astype(F32)
    r = x - hi
    mid = r.astype(BF16).astype(F32)
    lo = (r - mid).astype(BF16).astype(F32)
    return hi, mid, lo


def _tri_cumsum(x, inclusive):
    n = x.shape[0]
    row = lax.broadcasted_iota(jnp.int32, (n, n), 0)
    col = lax.broadcasted_iota(jnp.int32, (n, n), 1)
    tri = jnp.where((col <= row) if inclusive else (col < row), 1.0, 0.0).astype(BF16)
    parts = jnp.concatenate([p.astype(BF16) for p in _split3(x)], axis=1)
    out = jnp.dot(tri, parts, preferred_element_type=F32)
    return out[:, :LANES] + out[:, LANES:2 * LANES] + out[:, 2 * LANES:]


def _inproj_kernel(h_ref, g_ref, w1_ref, w2_ref, w3_ref, e6_ref, bf_ref, rope_ref,
                   dq_ref, dk_ref, vt_ref, fq_ref, fk_ref, carry_ref, mcarry_ref, *, nq):
    i = pl.program_id(0)

    @pl.when(i == 0)
    def _():
        carry_ref[...] = jnp.zeros_like(carry_ref)
        mcarry_ref[...] = jnp.zeros_like(mcarry_ref)

    hb = _rms(h_ref[...], g_ref[...]).astype(BF16)

    z1 = jnp.dot(hb, w1_ref[...], preferred_element_type=F32)

    z4 = z1[:, :LANES] + bf_ref[...]
    lane = lax.broadcasted_iota(jnp.int32, z4.shape, 1)
    logf = jnp.minimum(z4, 0.0) - jnp.log1p(jnp.exp(-jnp.abs(z4)))
    logf = jnp.where(lane < FOX_HEADS, logf, 0.0)
    j_in_batch = lax.rem(jnp.maximum(i - 1, 0), nq)
    base = jnp.where(i == 0, 0.0, jnp.where(j_in_batch == 0, mcarry_ref[...], carry_ref[...]))
    c = _tri_cumsum(logf, inclusive=True) + base

    vt_ref[0] = lax.dot_general(w2_ref[...], hb, (((1,), (1,)), ((), ())),
                                preferred_element_type=F32).astype(BF16)

    parts = jnp.concatenate([p.astype(BF16) for p in _split3(c * LOG2E)], axis=1)
    c6 = jnp.dot(parts, e6_ref[...], preferred_element_type=F32)
    z3 = jnp.dot(hb, w3_ref[...], preferred_element_type=F32)

    cos_t = rope_ref[:, 0:LANES]
    sin_lo = rope_ref[:, LANES:2 * LANES]
    sin_hi = rope_ref[:, 2 * LANES:3 * LANES]
    for j in range(8):
        zj = z1[:, LANES * (j + 1):LANES * (j + 2)]
        rot = zj * cos_t + pltpu.roll(zj, LANES - 8, 1) * sin_lo + pltpu.roll(zj, 8, 1) * sin_hi
        dst = dq_ref if j < 4 else dk_ref
        dst[:, LANES * (j % 4):LANES * (j % 4 + 1)] = rot.astype(BF16)

    @pl.when(i == 0)
    def _():
        mcarry_ref[...] = c[N_META - 1:N_META, :]

    carry_ref[...] = c[TM - 1:TM, :]

    for hd in range(FOX_HEADS):
        off = 64 if hd % 2 == 0 else 0
        moved = pltpu.roll(c6, (off - 6 * hd) % LANES, 1)
        first3 = (lane >= off) & (lane < off + 3)
        last3 = (lane >= off + 3) & (lane < off + 6)
        aug_q = jnp.where(first3, moved, jnp.where(last3, 1.0, 0.0))
        aug_k = jnp.where(last3, moved, jnp.where(first3, 1.0, 0.0))
        slab = hd // 2
        keep = (lane < 64) if hd % 2 == 0 else (lane >= 64)
        zq = z3[:, LANES * slab:LANES * (slab + 1)]
        zk = z3[:, HALF + LANES * slab:HALF + LANES * (slab + 1)]
        fq_ref[:, LANES * hd:LANES * (hd + 1)] = jnp.where(keep, zq, aug_q).astype(BF16)
        fk_ref[:, LANES * hd:LANES * (hd + 1)] = jnp.where(keep, zk, aug_k).astype(BF16)


def _inproj(h, g, w1, w2, w3, e6, bfp, rope, *, nq):
    R = h.shape[0]
    nt = R // TM
    n_real = nt - TQ // TM
    rows = lambda i: (jnp.where(i == 0, n_real, jnp.where(i <= n_real, i - 1, i)), 0)
    rope_rows = lambda i: (jnp.where(i == 0, nq, lax.rem(jnp.maximum(i - 1, 0), nq)), 0)
    const = lambda i: (0, 0)
    out_sd = lambda w: jax.ShapeDtypeStruct((R, w), BF16)
    return pl.pallas_call(
        functools.partial(_inproj_kernel, nq=nq),
        grid=(nt,),
        in_specs=[
            pl.BlockSpec((TM, D_MODEL), rows),
            pl.BlockSpec((1, D_MODEL), const),
            pl.BlockSpec((D_MODEL, LANES + 2 * HALF), const),
            pl.BlockSpec((2 * HALF, D_MODEL), const),
            pl.BlockSpec((D_MODEL, 2 * HALF), const),
            pl.BlockSpec((3 * LANES, LANES), const),
            pl.BlockSpec((1, LANES), const),
            pl.BlockSpec((TM, 3 * LANES), rope_rows),
        ],
        out_specs=[
            pl.BlockSpec((TM, HALF), rows), pl.BlockSpec((TM, HALF), rows),
            pl.BlockSpec((1, 2 * HALF, TM), lambda i: (rows(i)[0], 0, 0)),
            pl.BlockSpec((TM, 2 * HALF), rows), pl.BlockSpec((TM, 2 * HALF), rows),
        ],
        out_shape=[out_sd(HALF), out_sd(HALF), jax.ShapeDtypeStruct((nt, 2 * HALF, TM), BF16),
                   out_sd(2 * HALF), out_sd(2 * HALF)],
        scratch_shapes=[pltpu.VMEM((1, LANES), F32), pltpu.VMEM((1, LANES), F32)],
        compiler_params=_cparams(("arbitrary",)),
        name="inproj",
    )(h, g, w1, w2, w3, e6, bfp, rope)


def _attn_kernel(*refs, diff, nq, nb):
    if diff:
        q_ref, k_ref, vt_ref, km_ref, vtm_ref, par_ref, o_ref, acc_ref, m_ref, q_scr, sa_ref, sb_ref = refs
    else:
        q_ref, k_ref, vt_ref, km_ref, vtm_ref, o_ref, acc_ref, m_ref, q_scr, sa_ref, sb_ref = refs
    t = pl.program_id(1)
    nqt = nq // KEY_TILES_PER_Q
    is_real = t < nb * nqt
    jq = lax.rem(t, nqt)
    dv = acc_ref.shape[1] - ONES_ROWS

    for sub in range(2):
        if diff:
            q = q_ref[...]
            lane = lax.broadcasted_iota(jnp.int32, q.shape, 1)
            q_scr[sub] = jnp.where((lane < 64) if sub == 0 else (lane >= 64), q, jnp.zeros_like(q))
        else:
            q_scr[sub] = q_ref[:, LANES * sub:LANES * (sub + 1)]
        m_ref[sub] = jnp.full(m_ref.shape[1:], NEG_INF, F32)
        acc_ref[sub] = jnp.zeros(acc_ref.shape[1:], F32)

    def keys_of(sub, k_tile):
        return k_tile if diff else k_tile[:, LANES * sub:LANES * (sub + 1)]

    def values_of(sub, vt_tile):
        ones = jnp.ones((ONES_ROWS, vt_tile.shape[1]), BF16)
        vt = vt_tile if diff else vt_tile[dv * sub:dv * (sub + 1), :]
        return jnp.concatenate([vt, ones], axis=0)

    def scores(sub, k, q0):
        return lax.dot_general(k, q_scr[sub, q0:, :], (((1,), (1,)), ((), ())), preferred_element_type=F32)

    def update(sub, st, vt, mask, q0):
        if mask is not None:
            st = jnp.where(mask, st, NEG_INF)
        m_prev = m_ref[sub, :, q0:]
        m_new = jnp.maximum(m_prev, jnp.max(st, axis=0, keepdims=True))
        p = jnp.exp2(st - m_new).astype(BF16)
        if p.shape[0] < vt.shape[1]:
            p = jnp.concatenate([p, jnp.zeros((vt.shape[1] - p.shape[0], p.shape[1]), BF16)], axis=0)
        acc_ref[sub, :, q0:] = (jnp.exp2(m_prev - m_new) * acc_ref[sub, :, q0:]
                                + jnp.dot(vt, p, preferred_element_type=F32))
        m_ref[sub, :, q0:] = m_new

    def scores_into(buf, tile, q0):
        k_tile = k_ref[pl.ds(pl.multiple_of(tile * TM, TM), TM), :]
        for sub in range(2):
            buf[sub, :, q0:] = scores(sub, keys_of(sub, k_tile), q0)

    def update_from(buf, tile, causal, q0):
        vt_tile = vt_ref[tile]
        mask = None
        if causal:
            key = lax.broadcasted_iota(jnp.int32, (TM, TQ - q0), 0)
            qry = lax.broadcasted_iota(jnp.int32, (TM, TQ - q0), 1)
            mask = key <= qry
        for sub in range(2):
            update(sub, buf[sub, :, q0:], values_of(sub, vt_tile), mask, q0)

    key = lax.broadcasted_iota(jnp.int32, (N_META, TQ), 0)
    qry = lax.broadcasted_iota(jnp.int32, (N_META, TQ), 1)
    meta_mask = key <= jnp.where(is_real, N_META - 1, qry)
    km_tile = km_ref[...]
    vtm_tile = vtm_ref[0][:, :LANES]
    meta_scores = [scores(sub, keys_of(sub, km_tile), 0) for sub in range(2)]

    scores_into(sa_ref, 0, 0)
    for sub in range(2):
        update(sub, meta_scores[sub], values_of(sub, vtm_tile), meta_mask, 0)

    def pair(i, carry):
        scores_into(sb_ref, 2 * i + 1, 0)
        update_from(sa_ref, 2 * i, False, 0)
        scores_into(sa_ref, 2 * i + 2, 0)
        update_from(sb_ref, 2 * i + 1, False, 0)
        return carry

    lax.fori_loop(0, jnp.where(is_real, jq, 0), pair, 0)

    @pl.when(is_real)
    def _():
        scores_into(sb_ref, 2 * jq + 1, TM)
        update_from(sa_ref, 2 * jq, True, 0)
        update_from(sb_ref, 2 * jq + 1, True, TM)

    a0 = acc_ref[0]
    a1 = acc_ref[1]
    o0 = a0[:dv] / a0[dv:dv + 1]
    o1 = a1[:dv] / a1[dv:dv + 1]
    if diff:
        d = (o0 - par_ref[3:4, 0:1] * o1).T
        o_ref[...] = (_rms(d, par_ref[1:2, :]) * par_ref[2:3, :]).astype(BF16)
    else:
        o_ref[...] = jnp.concatenate([o0, o1], axis=0).T.astype(BF16)


def _attention(q, k, vt, par, *, diff, nq, nb):
    R = q.shape[0]
    S = nq * TM
    ntq = R // TQ
    qw = LANES if diff else 2 * LANES
    voff = 0 if diff else HALF // LANES
    meta_blk = (nb * S) // N_META
    batch_of = lambda t: jnp.minimum(t // (nq // KEY_TILES_PER_Q), nb - 1)
    in_specs = [
        pl.BlockSpec((TQ, qw), lambda p, t: (t, p)),
        pl.BlockSpec((S, qw), lambda p, t: (batch_of(t), p)),
        pl.BlockSpec((nq, LANES, TM), lambda p, t: (batch_of(t), p + voff, 0)),
        pl.BlockSpec((N_META, qw), lambda p, t: (meta_blk, p)),
        pl.BlockSpec((1, LANES, TM), lambda p, t: (nb * nq, p + voff, 0)),
    ]
    args = [q, k, vt, k, vt]
    if diff:
        in_specs.append(pl.BlockSpec((8, LANES), lambda p, t: (0, 0)))
        args.append(par)
    acc_rows = (LANES if diff else LANES // 2) + ONES_ROWS
    return pl.pallas_call(
        functools.partial(_attn_kernel, diff=diff, nq=nq, nb=nb),
        grid=(4, ntq),
        in_specs=in_specs,
        out_specs=pl.BlockSpec((TQ, LANES), lambda p, t: (t, p)),
        out_shape=jax.ShapeDtypeStruct((R, HALF), BF16),
        scratch_shapes=[pltpu.VMEM((2, acc_rows, TQ), F32), pltpu.VMEM((2, 1, TQ), F32),
                        pltpu.VMEM((2, TQ, LANES), BF16),
                        pltpu.VMEM((2, TM, TQ), F32), pltpu.VMEM((2, TM, TQ), F32)],
        compiler_params=_cparams(("arbitrary", "arbitrary")),
        name="diff_attn" if diff else "fox_attn",
    )(*args)


def _mixout_kernel(h_ref, oa_ref, ob_ref, g_ref, wg_ref, wbd_ref, wbf_ref, wo_ref, out_ref):
    x = h_ref[...]
    hb = _rms(x, g_ref[...]).astype(BF16)
    gates = jax.nn.sigmoid(jnp.dot(hb, wg_ref[...], preferred_element_type=F32))
    a = jnp.dot(oa_ref[...], wbd_ref[...], preferred_element_type=F32)
    b = jnp.dot(ob_ref[...], wbf_ref[...], preferred_element_type=F32)
    merged = gates[:, :D_MODEL] * a + gates[:, D_MODEL:] * b
    out_ref[...] = x + jnp.dot(merged.astype(BF16), wo_ref[...], preferred_element_type=F32)


def _mixout(h, oa, ob, g, wg, wbd, wbf, wo):
    R = h.shape[0]
    rows = lambda i: (i, 0)
    const = lambda i: (0, 0)
    return pl.pallas_call(
        _mixout_kernel,
        grid=(R // TM,),
        in_specs=[
            pl.BlockSpec((TM, D_MODEL), rows), pl.BlockSpec((TM, HALF), rows), pl.BlockSpec((TM, HALF), rows),
            pl.BlockSpec((1, D_MODEL), const), pl.BlockSpec((D_MODEL, 2 * D_MODEL), const),
            pl.BlockSpec((HALF, D_MODEL), const), pl.BlockSpec((HALF, D_MODEL), const),
            pl.BlockSpec((D_MODEL, D_MODEL), const),
        ],
        out_specs=pl.BlockSpec((TM, D_MODEL), rows),
        out_shape=jax.ShapeDtypeStruct((R, D_MODEL), F32),
        compiler_params=_cparams(("arbitrary",)),
        name="mixout",
    )(h, oa, ob, g, wg, wbd, wbf, wo)


def _swiglu_acc(xb, wg_ref, wu_ref, wd_ref, acc):
    def gate_up(c):
        sl = slice(FF_CHUNK * c, FF_CHUNK * (c + 1))
        return (jnp.dot(xb, wg_ref[:, sl], preferred_element_type=F32),
                jnp.dot(xb, wu_ref[:, sl], preferred_element_type=F32))

    nf = wg_ref.shape[1] // FF_CHUNK
    nxt = gate_up(0)
    for c in range(nf):
        gate, up = nxt
        if c + 1 < nf:
            nxt = gate_up(c + 1)
        mid = (gate * jax.nn.sigmoid(gate) * up).astype(BF16)
        acc = acc + jnp.dot(mid, wd_ref[FF_CHUNK * c:FF_CHUNK * (c + 1), :], preferred_element_type=F32)
    return acc


def _dense_ffn_kernel(h_ref, g_ref, wg_ref, wu_ref, wd_ref, out_ref):
    x = h_ref[...]
    hb = _rms(x, g_ref[...]).astype(BF16)
    out_ref[...] = _swiglu_acc(hb, wg_ref, wu_ref, wd_ref, x)


def _dense_ffn(h, g, wg, wu, wd):
    R = h.shape[0]
    dff = wg.shape[1]
    rows = lambda i: (i, 0)
    const = lambda i: (0, 0)
    return pl.pallas_call(
        _dense_ffn_kernel,
        grid=(R // TM,),
        in_specs=[
            pl.BlockSpec((TM, D_MODEL), rows), pl.BlockSpec((1, D_MODEL), const),
            pl.BlockSpec((D_MODEL, dff), const), pl.BlockSpec((D_MODEL, dff), const),
            pl.BlockSpec((dff, D_MODEL), const),
        ],
        out_specs=pl.BlockSpec((TM, D_MODEL), rows),
        out_shape=jax.ShapeDtypeStruct((R, D_MODEL), F32),
        compiler_params=_cparams(("arbitrary",)),
        name="dense_ffn",
    )(h, g, wg, wu, wd)


def _route_kernel(h_ref, g_ref, wr_ref, hn_ref, info_ref, cnt_ref, carry_ref):
    i = pl.program_id(0)

    @pl.when(i == 0)
    def _():
        carry_ref[...] = jnp.zeros_like(carry_ref)

    hn = _rms(h_ref[...], g_ref[...])
    hn_ref[...] = hn
    h_hi, h_mid, _ = _split3(hn)
    logits = (jnp.dot(h_hi.astype(BF16), wr_ref[0], preferred_element_type=F32)
              + jnp.dot(h_mid.astype(BF16), wr_ref[0], preferred_element_type=F32)
              + jnp.dot(h_hi.astype(BF16), wr_ref[1], preferred_element_type=F32))
    lane = lax.broadcasted_iota(jnp.int32, logits.shape, 1)
    logits = jnp.where(lane < N_EXPERTS, logits, -jnp.inf)
    v1 = jnp.max(logits, axis=-1, keepdims=True)
    e1 = jnp.min(jnp.where(logits == v1, lane, LANES), axis=-1, keepdims=True)
    rest = jnp.where(lane == e1, -jnp.inf, logits)
    v2 = jnp.max(rest, axis=-1, keepdims=True)
    e2 = jnp.min(jnp.where(rest == v2, lane, LANES), axis=-1, keepdims=True)
    ex = jnp.exp(v2 - v1)
    w1 = 1.0 / (1.0 + ex)
    w2 = ex / (1.0 + ex)
    hot1 = jnp.where(lane == e1, 1.0, 0.0)
    hot2 = jnp.where(lane == e2, 1.0, 0.0)
    hot = hot1 + hot2
    before = _tri_cumsum(hot, inclusive=False) + carry_ref[...]
    r1 = jnp.sum(before * hot1, axis=-1, keepdims=True)
    r2 = jnp.sum(before * hot2, axis=-1, keepdims=True)
    total = before[TM - 1:TM, :] + hot[TM - 1:TM, :]
    carry_ref[...] = total
    cnt_ref[...] = jnp.broadcast_to(total, cnt_ref.shape)
    info_ref[...] = jnp.where(lane == 0, e1.astype(F32), jnp.where(lane == 1, e2.astype(F32),
                              jnp.where(lane == 2, r1, jnp.where(lane == 3, r2,
                                        jnp.where(lane == 4, w1, jnp.where(lane == 5, w2, 0.0))))))


def _route(h, g, wr):
    R = h.shape[0]
    rows = lambda i: (i, 0)
    return pl.pallas_call(
        _route_kernel,
        grid=(R // TM,),
        in_specs=[pl.BlockSpec((TM, D_MODEL), rows), pl.BlockSpec((1, D_MODEL), lambda i: (0, 0)),
                  pl.BlockSpec((2, D_MODEL, LANES), lambda i: (0, 0, 0))],
        out_specs=[pl.BlockSpec((TM, D_MODEL), rows), pl.BlockSpec((TM, LANES), rows),
                   pl.BlockSpec((8, LANES), lambda i: (0, 0))],
        out_shape=[jax.ShapeDtypeStruct((R, D_MODEL), F32), jax.ShapeDtypeStruct((R, LANES), F32),
                   jax.ShapeDtypeStruct((8, LANES), F32)],
        scratch_shapes=[pltpu.VMEM((1, LANES), F32)],
        compiler_params=_cparams(("arbitrary",)),
        name="route",
    )(h, g, wr)


def _scatter_kernel(dest_ref, last_ref, src_ref, out_ref, zero_ref, sem):
    @pl.when(pl.program_id(0) == 0)
    def _():
        zero_ref[...] = jnp.zeros_like(zero_ref)

        def zero_copy(e):
            row = pl.multiple_of(last_ref[0, e] * TME, TME)
            return pltpu.make_async_copy(zero_ref, out_ref.at[pl.ds(row, TME)], sem)

        for e in range(N_EXPERTS):
            @pl.when(last_ref[1, e] > 0)
            def _(e=e):
                zero_copy(e).start()
        for e in range(N_EXPERTS):
            @pl.when(last_ref[1, e] > 0)
            def _(e=e):
                zero_copy(e).wait()

        def spare_copy(tile):
            return pltpu.make_async_copy(zero_ref, out_ref.at[pl.ds(pl.multiple_of(tile * TME, TME), TME)], sem)

        def start_spare(tile, c):
            spare_copy(tile).start()
            return c

        def wait_spare(tile, c):
            spare_copy(tile).wait()
            return c

        n_tiles = out_ref.shape[0] // TME
        lax.fori_loop(last_ref[2, 0], n_tiles, start_spare, 0)
        lax.fori_loop(last_ref[2, 0], n_tiles, wait_spare, 0)

    def copy(r, k):
        return pltpu.make_async_copy(src_ref.at[pl.ds(r, 1)],
                                     out_ref.at[pl.ds(dest_ref[0, 0, 2 * r + k], 1)], sem)

    def issue(r, c):
        copy(r, 0).start(priority=0)
        copy(r, 1).start(priority=1)
        return c

    lax.fori_loop(0, TM, issue, 0, unroll=DMA_UNROLL)
    for _ in range(2):
        pltpu.make_async_copy(src_ref, out_ref.at[pl.ds(0, TM)], sem).wait()


def _scatter_rows(dest, last_tiles, src, n_rows):
    R = src.shape[0]
    return pl.pallas_call(
        _scatter_kernel,
        grid=(R // TM,),
        in_specs=[pl.BlockSpec((1, 1, 2 * TM), lambda i: (i, 0, 0), memory_space=pltpu.SMEM),
                  pl.BlockSpec(memory_space=pltpu.SMEM),
                  pl.BlockSpec((TM, D_MODEL), lambda i: (i, 0))],
        out_specs=pl.BlockSpec(memory_space=pl.ANY),
        out_shape=jax.ShapeDtypeStruct((n_rows, D_MODEL), src.dtype),
        scratch_shapes=[pltpu.VMEM((TME, D_MODEL), F32), pltpu.SemaphoreType.DMA(())],
        compiler_params=pltpu.CompilerParams(dimension_semantics=("arbitrary",), vmem_limit_bytes=VMEM_LIMIT,
                                             has_side_effects=True),
        name="scatter_rows",
    )(dest, last_tiles, src)


def _expert_kernel(te_ref, act_ref, x_ref, wg_ref, wu_ref, wd_ref, y_ref):
    i = pl.program_id(0)
    f = pl.program_id(1)
    del te_ref

    @pl.when(f == 0)
    def _():
        y_ref[...] = jnp.zeros_like(y_ref)

    @pl.when(act_ref[i] > 0)
    def _():
        y_ref[...] = _swiglu_acc(x_ref[...].astype(BF16), wg_ref, wu_ref, wd_ref, y_ref[...])


def _experts(tile_expert, tile_active, xs, wg, wu, wd):
    mt = tile_expert.shape[0]
    dffe = wg.shape[2]
    grid_spec = pltpu.PrefetchScalarGridSpec(
        num_scalar_prefetch=2,
        grid=(mt, dffe // TF_MOE),
        in_specs=[
            pl.BlockSpec((TME, D_MODEL), lambda i, f, te, act: (i, 0)),
            pl.BlockSpec((None, D_MODEL, TF_MOE), lambda i, f, te, act: (te[i], 0, f)),
            pl.BlockSpec((None, D_MODEL, TF_MOE), lambda i, f, te, act: (te[i], 0, f)),
            pl.BlockSpec((None, TF_MOE, D_MODEL), lambda i, f, te, act: (te[i], f, 0)),
        ],
        out_specs=pl.BlockSpec((TME, D_MODEL), lambda i, f, te, act: (i, 0)),
    )
    return pl.pallas_call(
        _expert_kernel,
        grid_spec=grid_spec,
        out_shape=jax.ShapeDtypeStruct((mt * TME, D_MODEL), F32),
        compiler_params=_cparams(("arbitrary", "arbitrary")),
        name="experts",
    )(tile_expert, tile_active, xs, wg, wu, wd)


def _combine_kernel(dest_ref, h_ref, info_ref, g_ref, y_ref, out_ref, buf_ref, sem, *, final):
    def copy(r, k):
        return pltpu.make_async_copy(y_ref.at[pl.ds(dest_ref[0, 0, 2 * r + k], 1)],
                                     buf_ref.at[k, pl.ds(r, 1)], sem)

    def issue(r, c):
        copy(r, 0).start(priority=0)
        copy(r, 1).start(priority=1)
        return c

    lax.fori_loop(0, TM, issue, 0, unroll=DMA_UNROLL)
    for k in range(2):
        pltpu.make_async_copy(y_ref.at[pl.ds(0, TM)], buf_ref.at[k], sem).wait()
    info = info_ref[...]
    out = h_ref[...] + info[:, 4:5] * buf_ref[0] + info[:, 5:6] * buf_ref[1]
    out_ref[...] = _rms(out, g_ref[...]) if final else out


def _combine(dest, h, info, y, final_g=None, n_rows=None):
    final = final_g is not None
    n_rows = n_rows if final else h.shape[0]
    g = final_g if final else jnp.ones((1, D_MODEL), F32)
    rows = lambda i: (i, 0)
    return pl.pallas_call(
        functools.partial(_combine_kernel, final=final),
        grid=(n_rows // TM,),
        in_specs=[pl.BlockSpec((1, 1, 2 * TM), lambda i: (i, 0, 0), memory_space=pltpu.SMEM),
                  pl.BlockSpec((TM, D_MODEL), rows), pl.BlockSpec((TM, LANES), rows),
                  pl.BlockSpec((1, D_MODEL), lambda i: (0, 0)), pl.BlockSpec(memory_space=pl.ANY)],
        out_specs=pl.BlockSpec((TM, D_MODEL), rows),
        out_shape=jax.ShapeDtypeStruct((n_rows, D_MODEL), F32),
        scratch_shapes=[pltpu.VMEM((2, TM, D_MODEL), F32), pltpu.SemaphoreType.DMA(())],
        compiler_params=_cparams(("arbitrary",)),
        name="combine",
    )(dest, h, info, g, y)


def _moe_ffn(h, g, wr, wg, wu, wd, final_g=None, n_rows=None):
    R = h.shape[0]
    hn, info, counts = _route(h, g, wr)
    cnt = counts[0, :N_EXPERTS].astype(jnp.int32)
    padded = ((cnt + TME - 1) // TME) * TME
    ends = jnp.cumsum(padded)
    starts = ends - padded
    mt = (2 * R) // TME + N_EXPERTS
    tile_row = jnp.arange(mt, dtype=jnp.int32) * TME
    tile_expert = jnp.minimum(jnp.searchsorted(ends, tile_row, side="right"), N_EXPERTS - 1).astype(jnp.int32)
    tile_active = (tile_row < ends[-1]).astype(jnp.int32)
    e12 = info[:, 0:2].astype(jnp.int32)
    dest = (starts[e12] + info[:, 2:4].astype(jnp.int32)).reshape(R // TM, 1, 2 * TM)
    last_tiles = jnp.stack([jnp.maximum(ends // TME - 1, 0), (padded > 0).astype(jnp.int32),
                            jnp.broadcast_to(ends[-1] // TME, (N_EXPERTS,))]).astype(jnp.int32)
    xs = _scatter_rows(dest, last_tiles, hn, mt * TME)
    y = _experts(tile_expert, tile_active, xs, wg, wu, wd)
    return _combine(dest, h, info, y, final_g, n_rows)


def _final_kernel(h_ref, g_ref, out_ref):
    out_ref[...] = _rms(h_ref[...], g_ref[...])


def _final_norm(h, g, n_rows):
    rows = lambda i: (i, 0)
    return pl.pallas_call(
        _final_kernel,
        grid=(n_rows // TM,),
        in_specs=[pl.BlockSpec((TM, D_MODEL), rows), pl.BlockSpec((1, D_MODEL), lambda i: (0, 0))],
        out_specs=pl.BlockSpec((TM, D_MODEL), rows),
        out_shape=jax.ShapeDtypeStruct((n_rows, D_MODEL), F32),
        compiler_params=_cparams(("arbitrary",)),
        name="final_norm",
    )(h, g)


def _forget_column_placement():
    row = jnp.arange(3 * LANES)[:, None]
    col = jnp.arange(LANES)[None, :]
    term, head = row // LANES, row % LANES
    valid = head < FOX_HEADS
    plus = valid & (col == 6 * head + term)
    minus = valid & (col == 6 * head + 3 + term)
    return (plus.astype(F32) - minus.astype(F32)).astype(BF16)


def _rope_table(S):
    rd = DIFF_DH // 4
    inv = ROPE_THETA ** (-jnp.arange(0, rd, 2, dtype=F32) / rd)
    pos = jnp.concatenate([jnp.arange(N_META, N_META + S, dtype=F32), jnp.arange(TM, dtype=F32)])
    ang = pos[:, None] * inv[None, :]
    cos, sin = jnp.cos(ang), jnp.sin(ang)
    n = pos.shape[0]
    cos_t = jnp.tile(jnp.concatenate([cos, cos, jnp.ones((n, 48), F32)], axis=1), (1, 2))
    sin_lo = jnp.tile(jnp.concatenate([-sin, jnp.zeros((n, 56), F32)], axis=1), (1, 2))
    sin_hi = jnp.tile(jnp.concatenate([jnp.zeros((n, 8), F32), sin, jnp.zeros((n, 48), F32)], axis=1), (1, 2))
    return jnp.concatenate([cos_t, sin_lo, sin_hi], axis=1)


def kernel(x, meta_tokens, norm_mix_g, w_in, b_forget, diff_lambda, diff_subln_g, w_branch_diff, w_branch_fox,
           w_out, norm_ffn_g, ffn_w_gate, ffn_w_up, ffn_w_down, moe_router, moe_w_gate, moe_w_up, moe_w_down,
           final_norm_g):
    B, S, D = x.shape
    depth = w_in.shape[0]
    assert D == D_MODEL and S % TQ == 0 and meta_tokens.shape[0] == N_META
    nq = S // TM
    h = jnp.concatenate([x.reshape(B * S, D), meta_tokens.astype(x.dtype),
                         jnp.zeros((TQ - N_META, D), x.dtype)], axis=0)
    rope = _rope_table(S)
    g_final = final_norm_g.astype(F32).reshape(1, D)
    e6 = _forget_column_placement()
    scale = DIFF_DH ** -0.5 * LOG2E
    for layer in range(depth):
        lam_init = 0.8 - 0.6 * math.exp(-0.3 * layer)
        w = w_in[layer]
        dq, dk, dv, fq, fk, fv, ff, ga, gb = jnp.split(
            w, [HALF, 2 * HALF, 3 * HALF, 4 * HALF, 5 * HALF, 6 * HALF, 6 * HALF + FOX_HEADS,
                6 * HALF + FOX_HEADS + D_MODEL], axis=1)
        w1 = jnp.concatenate([jnp.pad(ff, ((0, 0), (0, LANES - FOX_HEADS))), dq * scale, dk], axis=1).astype(BF16)
        w2 = jnp.concatenate([dv, fv], axis=1).T.astype(BF16)
        w3 = jnp.concatenate([fq * scale, fk], axis=1).astype(BF16)
        bfp = jnp.pad(b_forget[layer].astype(F32), (0, LANES - FOX_HEADS)).reshape(1, LANES)
        g_mix = norm_mix_g[layer].astype(F32).reshape(1, D)
        dq_a, dk_a, vt_a, fq_a, fk_a = _inproj(h, g_mix, w1, w2, w3, e6, bfp, rope, nq=nq)

        lp = diff_lambda[layer].astype(F32)
        lam = jnp.exp(jnp.sum(lp[0] * lp[1])) - jnp.exp(jnp.sum(lp[2] * lp[3])) + lam_init
        par = jnp.zeros((8, LANES), F32)
        par = par.at[1].set(diff_subln_g[layer].astype(F32)).at[2].set(1.0 - lam_init).at[3].set(lam)
        o_a = _attention(dq_a, dk_a, vt_a, par, diff=True, nq=nq, nb=B)
        o_b = _attention(fq_a, fk_a, vt_a, None, diff=False, nq=nq, nb=B)

        wgate = jnp.concatenate([ga, gb], axis=1).astype(BF16)
        h = _mixout(h, o_a, o_b, g_mix, wgate, w_branch_diff[layer].astype(BF16),
                    w_branch_fox[layer].astype(BF16), w_out[layer].astype(BF16))

        g_ffn = norm_ffn_g[layer].astype(F32).reshape(1, D)
        jj = layer // 2
        if layer % 2 == 0:
            h = _dense_ffn(h, g_ffn, ffn_w_gate[jj].astype(BF16), ffn_w_up[jj].astype(BF16),
                           ffn_w_down[jj].astype(BF16))
        else:
            r_hi, r_mid, _ = _split3(jnp.pad(moe_router[jj].astype(F32), ((0, 0), (0, LANES - N_EXPERTS))))
            wr = jnp.stack([r_hi, r_mid]).astype(BF16)
            last = layer == depth - 1
            h = _moe_ffn(h, g_ffn, wr, moe_w_gate[jj].astype(BF16), moe_w_up[jj].astype(BF16),
                         moe_w_down[jj].astype(BF16), g_final if last else None, B * S)
    out = h if depth % 2 == 0 else _final_norm(h, g_final, B * S)
    return out.reshape(B, S, D)
```

```python
import functools
import math

import jax
import jax.numpy as jnp
from jax import lax
from jax.experimental import pallas as pl
from jax.experimental.pallas import tpu as pltpu

D_MODEL = 1024
N_META = 16
ROPE_THETA = 500000.0
RMS_EPS = 1e-6
NEG_INF = -1e30

DIFF_HEADS = 4
DIFF_DH = 64
FOX_HEADS = 8
FOX_DH = 64
HALF = 512
N_EXPERTS = 8
LANES = 128

TM = 512
KEY_TILES_PER_Q = 2
TQ = KEY_TILES_PER_Q * TM
TME = 512
TF_MOE = 1792
FF_CHUNK = 256
ROW_TILE = 8
DMA_UNROLL = 8
LOG2E = 1.4426950408889634
ONES_ROWS = 16
VMEM_LIMIT = 56 * 1024 * 1024

F32 = jnp.float32
BF16 = jnp.bfloat16


def _cparams(sem):
    return pltpu.CompilerParams(dimension_semantics=sem, vmem_limit_bytes=VMEM_LIMIT)


def _rms(x, g):
    ms = jnp.mean(x * x, axis=-1, keepdims=True)
    return x * lax.rsqrt(ms + RMS_EPS) * g


def _split3(x):
    hi = x.astype(BF16).astype(F32)
    r = x - hi
    mid = r.astype(BF16).astype(F32)
    lo = (r - mid).astype(BF16).astype(F32)
    return hi, mid, lo


def _tri_cumsum(x, inclusive):
    n = x.shape[0]
    row = lax.broadcasted_iota(jnp.int32, (n, n), 0)
    col = lax.broadcasted_iota(jnp.int32, (n, n), 1)
    tri = jnp.where((col <= row) if inclusive else (col < row), 1.0, 0.0).astype(BF16)
    parts = jnp.concatenate([p.astype(BF16) for p in _split3(x)], axis=1)
    out = jnp.dot(tri, parts, preferred_element_type=F32)
    return out[:, :LANES] + out[:, LANES:2 * LANES] + out[:, 2 * LANES:]


def _store_row_tiles(ref, x):
    rows = x.shape[0]
    for s in range(ROW_TILE):
        ref[pl.ds(s, rows, stride=ROW_TILE), :] = x[:, LANES * s:LANES * (s + 1)]


def _load_row_tiles(ref, rows):
    return jnp.concatenate([ref[pl.ds(s, rows, stride=ROW_TILE), :] for s in range(ROW_TILE)], axis=1)


def _inproj_kernel(h_ref, g_ref, w1_ref, w2_ref, w3_ref, e6_ref, bf_ref, rope_ref,
                   dq_ref, dk_ref, vt_ref, fq_ref, fk_ref, carry_ref, mcarry_ref, *, nq):
    i = pl.program_id(0)

    @pl.when(i == 0)
    def _():
        carry_ref[...] = jnp.zeros_like(carry_ref)
        mcarry_ref[...] = jnp.zeros_like(mcarry_ref)

    hb = _rms(h_ref[...], g_ref[...]).astype(BF16)

    z1 = jnp.dot(hb, w1_ref[...], preferred_element_type=F32)

    z4 = z1[:, :LANES] + bf_ref[...]
    lane = lax.broadcasted_iota(jnp.int32, z4.shape, 1)
    logf = jnp.minimum(z4, 0.0) - jnp.log1p(jnp.exp(-jnp.abs(z4)))
    logf = jnp.where(lane < FOX_HEADS, logf, 0.0)
    j_in_batch = lax.rem(jnp.maximum(i - 1, 0), nq)
    base = jnp.where(i == 0, 0.0, jnp.where(j_in_batch == 0, mcarry_ref[...], carry_ref[...]))
    c = _tri_cumsum(logf, inclusive=True) + base

    vt_ref[0] = lax.dot_general(w2_ref[...], hb, (((1,), (1,)), ((), ())),
                                preferred_element_type=F32).astype(BF16)

    parts = jnp.concatenate([p.astype(BF16) for p in _split3(c * LOG2E)], axis=1)
    c6 = jnp.dot(parts, e6_ref[...], preferred_element_type=F32)
    z3 = jnp.dot(hb, w3_ref[...], preferred_element_type=F32)

    cos_t = rope_ref[:, 0:LANES]
    sin_lo = rope_ref[:, LANES:2 * LANES]
    sin_hi = rope_ref[:, 2 * LANES:3 * LANES]
    for j in range(8):
        zj = z1[:, LANES * (j + 1):LANES * (j + 2)]
        rot = zj * cos_t + pltpu.roll(zj, LANES - 8, 1) * sin_lo + pltpu.roll(zj, 8, 1) * sin_hi
        dst = dq_ref if j < 4 else dk_ref
        dst[:, LANES * (j % 4):LANES * (j % 4 + 1)] = rot.astype(BF16)

    @pl.when(i == 0)
    def _():
        mcarry_ref[...] = c[N_META - 1:N_META, :]

    carry_ref[...] = c[TM - 1:TM, :]

    for hd in range(FOX_HEADS):
        off = 64 if hd % 2 == 0 else 0
        moved = pltpu.roll(c6, (off - 6 * hd) % LANES, 1)
        first3 = (lane >= off) & (lane < off + 3)
        last3 = (lane >= off + 3) & (lane < off + 6)
        aug_q = jnp.where(first3, moved, jnp.where(last3, 1.0, 0.0))
        aug_k = jnp.where(last3, moved, jnp.where(first3, 1.0, 0.0))
        slab = hd // 2
        keep = (lane < 64) if hd % 2 == 0 else (lane >= 64)
        zq = z3[:, LANES * slab:LANES * (slab + 1)]
        zk = z3[:, HALF + LANES * slab:HALF + LANES * (slab + 1)]
        fq_ref[:, LANES * hd:LANES * (hd + 1)] = jnp.where(keep, zq, aug_q).astype(BF16)
        fk_ref[:, LANES * hd:LANES * (hd + 1)] = jnp.where(keep, zk, aug_k).astype(BF16)


def _inproj(h, g, w1, w2, w3, e6, bfp, rope, *, nq):
    R = h.shape[0]
    nt = R // TM
    n_real = nt - TQ // TM
    rows = lambda i: (jnp.where(i == 0, n_real, jnp.where(i <= n_real, i - 1, i)), 0)
    rope_rows = lambda i: (jnp.where(i == 0, nq, lax.rem(jnp.maximum(i - 1, 0), nq)), 0)
    const = lambda i: (0, 0)
    out_sd = lambda w: jax.ShapeDtypeStruct((R, w), BF16)
    return pl.pallas_call(
        functools.partial(_inproj_kernel, nq=nq),
        grid=(nt,),
        in_specs=[
            pl.BlockSpec((TM, D_MODEL), rows),
            pl.BlockSpec((1, D_MODEL), const),
            pl.BlockSpec((D_MODEL, LANES + 2 * HALF), const),
            pl.BlockSpec((2 * HALF, D_MODEL), const),
            pl.BlockSpec((D_MODEL, 2 * HALF), const),
            pl.BlockSpec((3 * LANES, LANES), const),
            pl.BlockSpec((1, LANES), const),
            pl.BlockSpec((TM, 3 * LANES), rope_rows),
        ],
        out_specs=[
            pl.BlockSpec((TM, HALF), rows), pl.BlockSpec((TM, HALF), rows),
            pl.BlockSpec((1, 2 * HALF, TM), lambda i: (rows(i)[0], 0, 0)),
            pl.BlockSpec((TM, 2 * HALF), rows), pl.BlockSpec((TM, 2 * HALF), rows),
        ],
        out_shape=[out_sd(HALF), out_sd(HALF), jax.ShapeDtypeStruct((nt, 2 * HALF, TM), BF16),
                   out_sd(2 * HALF), out_sd(2 * HALF)],
        scratch_shapes=[pltpu.VMEM((1, LANES), F32), pltpu.VMEM((1, LANES), F32)],
        compiler_params=_cparams(("arbitrary",)),
        name="inproj",
    )(h, g, w1, w2, w3, e6, bfp, rope)


def _attn_kernel(*refs, diff, nq, nb):
    if diff:
        q_ref, k_ref, vt_ref, km_ref, vtm_ref, par_ref, o_ref, acc_ref, m_ref, q_scr, sa_ref, sb_ref = refs
    else:
        q_ref, k_ref, vt_ref, km_ref, vtm_ref, o_ref, acc_ref, m_ref, q_scr, sa_ref, sb_ref = refs
    t = pl.program_id(1)
    nqt = nq // KEY_TILES_PER_Q
    is_real = t < nb * nqt
    jq = lax.rem(t, nqt)
    dv = acc_ref.shape[1] - ONES_ROWS

    for sub in range(2):
        if diff:
            q = q_ref[...]
            lane = lax.broadcasted_iota(jnp.int32, q.shape, 1)
            q_scr[sub] = jnp.where((lane < 64) if sub == 0 else (lane >= 64), q, jnp.zeros_like(q))
        else:
            q_scr[sub] = q_ref[:, LANES * sub:LANES * (sub + 1)]
        m_ref[sub] = jnp.full(m_ref.shape[1:], NEG_INF, F32)
        acc_ref[sub] = jnp.zeros(acc_ref.shape[1:], F32)

    def keys_of(sub, k_tile):
        return k_tile if diff else k_tile[:, LANES * sub:LANES * (sub + 1)]

    def values_of(sub, vt_tile):
        ones = jnp.ones((ONES_ROWS, vt_tile.shape[1]), BF16)
        vt = vt_tile if diff else vt_tile[dv * sub:dv * (sub + 1), :]
        return jnp.concatenate([vt, ones], axis=0)

    def scores(sub, k, q0):
        return lax.dot_general(k, q_scr[sub, q0:, :], (((1,), (1,)), ((), ())), preferred_element_type=F32)

    def update(sub, st, vt, mask, q0):
        if mask is not None:
            st = jnp.where(mask, st, NEG_INF)
        m_prev = m_ref[sub, :, q0:]
        m_new = jnp.maximum(m_prev, jnp.max(st, axis=0, keepdims=True))
        p = jnp.exp2(st - m_new).astype(BF16)
        if p.shape[0] < vt.shape[1]:
            p = jnp.concatenate([p, jnp.zeros((vt.shape[1] - p.shape[0], p.shape[1]), BF16)], axis=0)
        acc_ref[sub, :, q0:] = (jnp.exp2(m_prev - m_new) * acc_ref[sub, :, q0:]
                                + jnp.dot(vt, p, preferred_element_type=F32))
        m_ref[sub, :, q0:] = m_new

    def scores_into(buf, tile, q0):
        k_tile = k_ref[pl.ds(pl.multiple_of(tile * TM, TM), TM), :]
        for sub in range(2):
            buf[sub, :, q0:] = scores(sub, keys_of(sub, k_tile), q0)

    def update_from(buf, tile, causal, q0):
        vt_tile = vt_ref[tile]
        mask = None
        if causal:
            key = lax.broadcasted_iota(jnp.int32, (TM, TQ - q0), 0)
            qry = lax.broadcasted_iota(jnp.int32, (TM, TQ - q0), 1)
            mask = key <= qry
        for sub in range(2):
            update(sub, buf[sub, :, q0:], values_of(sub, vt_tile), mask, q0)

    key = lax.broadcasted_iota(jnp.int32, (N_META, TQ), 0)
    qry = lax.broadcasted_iota(jnp.int32, (N_META, TQ), 1)
    meta_mask = key <= jnp.where(is_real, N_META - 1, qry)
    km_tile = km_ref[...]
    vtm_tile = vtm_ref[0][:, :LANES]
    meta_scores = [scores(sub, keys_of(sub, km_tile), 0) for sub in range(2)]

    scores_into(sa_ref, 0, 0)
    for sub in range(2):
        update(sub, meta_scores[sub], values_of(sub, vtm_tile), meta_mask, 0)

    def pair(i, carry):
        scores_into(sb_ref, 2 * i + 1, 0)
        update_from(sa_ref, 2 * i, False, 0)
        scores_into(sa_ref, 2 * i + 2, 0)
        update_from(sb_ref, 2 * i + 1, False, 0)
        return carry

    lax.fori_loop(0, jnp.where(is_real, jq, 0), pair, 0)

    @pl.when(is_real)
    def _():
        scores_into(sb_ref, 2 * jq + 1, TM)
        update_from(sa_ref, 2 * jq, True, 0)
        update_from(sb_ref, 2 * jq + 1, True, TM)

    a0 = acc_ref[0]
    a1 = acc_ref[1]
    o0 = a0[:dv] / a0[dv:dv + 1]
    o1 = a1[:dv] / a1[dv:dv + 1]
    if diff:
        d = (o0 - par_ref[3:4, 0:1] * o1).T
        o_ref[...] = (_rms(d, par_ref[1:2, :]) * par_ref[2:3, :]).astype(BF16)
    else:
        o_ref[...] = jnp.concatenate([o0, o1], axis=0).T.astype(BF16)


def _attention(q, k, vt, par, *, diff, nq, nb):
    R = q.shape[0]
    S = nq * TM
    ntq = R // TQ
    qw = LANES if diff else 2 * LANES
    voff = 0 if diff else HALF // LANES
    meta_blk = (nb * S) // N_META
    batch_of = lambda t: jnp.minimum(t // (nq // KEY_TILES_PER_Q), nb - 1)
    in_specs = [
        pl.BlockSpec((TQ, qw), lambda p, t: (t, p)),
        pl.BlockSpec((S, qw), lambda p, t: (batch_of(t), p)),
        pl.BlockSpec((nq, LANES, TM), lambda p, t: (batch_of(t), p + voff, 0)),
        pl.BlockSpec((N_META, qw), lambda p, t: (meta_blk, p)),
        pl.BlockSpec((1, LANES, TM), lambda p, t: (nb * nq, p + voff, 0)),
    ]
    args = [q, k, vt, k, vt]
    if diff:
        in_specs.append(pl.BlockSpec((8, LANES), lambda p, t: (0, 0)))
        args.append(par)
    acc_rows = (LANES if diff else LANES // 2) + ONES_ROWS
    return pl.pallas_call(
        functools.partial(_attn_kernel, diff=diff, nq=nq, nb=nb),
        grid=(4, ntq),
        in_specs=in_specs,
        out_specs=pl.BlockSpec((TQ, LANES), lambda p, t: (t, p)),
        out_shape=jax.ShapeDtypeStruct((R, HALF), BF16),
        scratch_shapes=[pltpu.VMEM((2, acc_rows, TQ), F32), pltpu.VMEM((2, 1, TQ), F32),
                        pltpu.VMEM((2, TQ, LANES), BF16),
                        pltpu.VMEM((2, TM, TQ), F32), pltpu.VMEM((2, TM, TQ), F32)],
        compiler_params=_cparams(("arbitrary", "arbitrary")),
        name="diff_attn" if diff else "fox_attn",
    )(*args)


def _mixout_kernel(h_ref, oa_ref, ob_ref, g_ref, wg_ref, wbd_ref, wbf_ref, wo_ref, out_ref):
    x = h_ref[...]
    hb = _rms(x, g_ref[...]).astype(BF16)
    gates = jax.nn.sigmoid(jnp.dot(hb, wg_ref[...], preferred_element_type=F32))
    a = jnp.dot(oa_ref[...], wbd_ref[...], preferred_element_type=F32)
    b = jnp.dot(ob_ref[...], wbf_ref[...], preferred_element_type=F32)
    merged = gates[:, :D_MODEL] * a + gates[:, D_MODEL:] * b
    out_ref[...] = x + jnp.dot(merged.astype(BF16), wo_ref[...], preferred_element_type=F32)


def _mixout(h, oa, ob, g, wg, wbd, wbf, wo):
    R = h.shape[0]
    rows = lambda i: (i, 0)
    const = lambda i: (0, 0)
    return pl.pallas_call(
        _mixout_kernel,
        grid=(R // TM,),
        in_specs=[
            pl.BlockSpec((TM, D_MODEL), rows), pl.BlockSpec((TM, HALF), rows), pl.BlockSpec((TM, HALF), rows),
            pl.BlockSpec((1, D_MODEL), const), pl.BlockSpec((D_MODEL, 2 * D_MODEL), const),
            pl.BlockSpec((HALF, D_MODEL), const), pl.BlockSpec((HALF, D_MODEL), const),
            pl.BlockSpec((D_MODEL, D_MODEL), const),
        ],
        out_specs=pl.BlockSpec((TM, D_MODEL), rows),
        out_shape=jax.ShapeDtypeStruct((R, D_MODEL), F32),
        compiler_params=_cparams(("arbitrary",)),
        name="mixout",
    )(h, oa, ob, g, wg, wbd, wbf, wo)


def _swiglu_acc(xb, wg_ref, wu_ref, wd_ref, acc):
    def gate_up(c):
        sl = slice(FF_CHUNK * c, FF_CHUNK * (c + 1))
        return (jnp.dot(xb, wg_ref[:, sl], preferred_element_type=F32),
                jnp.dot(xb, wu_ref[:, sl], preferred_element_type=F32))

    nf = wg_ref.shape[1] // FF_CHUNK
    nxt = gate_up(0)
    for c in range(nf):
        gate, up = nxt
        if c + 1 < nf:
            nxt = gate_up(c + 1)
        mid = (gate * jax.nn.sigmoid(gate) * up).astype(BF16)
        acc = acc + jnp.dot(mid, wd_ref[FF_CHUNK * c:FF_CHUNK * (c + 1), :], preferred_element_type=F32)
    return acc


def _dense_ffn_kernel(h_ref, g_ref, wg_ref, wu_ref, wd_ref, out_ref):
    x = h_ref[...]
    hb = _rms(x, g_ref[...]).astype(BF16)
    out_ref[...] = _swiglu_acc(hb, wg_ref, wu_ref, wd_ref, x)


def _dense_ffn(h, g, wg, wu, wd):
    R = h.shape[0]
    dff = wg.shape[1]
    rows = lambda i: (i, 0)
    const = lambda i: (0, 0)
    return pl.pallas_call(
        _dense_ffn_kernel,
        grid=(R // TM,),
        in_specs=[
            pl.BlockSpec((TM, D_MODEL), rows), pl.BlockSpec((1, D_MODEL), const),
            pl.BlockSpec((D_MODEL, dff), const), pl.BlockSpec((D_MODEL, dff), const),
            pl.BlockSpec((dff, D_MODEL), const),
        ],
        out_specs=pl.BlockSpec((TM, D_MODEL), rows),
        out_shape=jax.ShapeDtypeStruct((R, D_MODEL), F32),
        compiler_params=_cparams(("arbitrary",)),
        name="dense_ffn",
    )(h, g, wg, wu, wd)


def _route_kernel(h_ref, g_ref, wr_ref, hn_ref, info_ref, cnt_ref, carry_ref):
    i = pl.program_id(0)

    @pl.when(i == 0)
    def _():
        carry_ref[...] = jnp.zeros_like(carry_ref)

    hn = _rms(h_ref[...], g_ref[...])
    _store_row_tiles(hn_ref, hn)
    h_hi, h_mid, _ = _split3(hn)
    logits = (jnp.dot(h_hi.astype(BF16), wr_ref[0], preferred_element_type=F32)
              + jnp.dot(h_mid.astype(BF16), wr_ref[0], preferred_element_type=F32)
              + jnp.dot(h_hi.astype(BF16), wr_ref[1], preferred_element_type=F32))
    lane = lax.broadcasted_iota(jnp.int32, logits.shape, 1)
    logits = jnp.where(lane < N_EXPERTS, logits, -jnp.inf)
    v1 = jnp.max(logits, axis=-1, keepdims=True)
    e1 = jnp.min(jnp.where(logits == v1, lane, LANES), axis=-1, keepdims=True)
    rest = jnp.where(lane == e1, -jnp.inf, logits)
    v2 = jnp.max(rest, axis=-1, keepdims=True)
    e2 = jnp.min(jnp.where(rest == v2, lane, LANES), axis=-1, keepdims=True)
    ex = jnp.exp(v2 - v1)
    w1 = 1.0 / (1.0 + ex)
    w2 = ex / (1.0 + ex)
    hot1 = jnp.where(lane == e1, 1.0, 0.0)
    hot2 = jnp.where(lane == e2, 1.0, 0.0)
    hot = hot1 + hot2
    before = _tri_cumsum(hot, inclusive=False) + carry_ref[...]
    r1 = jnp.sum(before * hot1, axis=-1, keepdims=True)
    r2 = jnp.sum(before * hot2, axis=-1, keepdims=True)
    total = before[TM - 1:TM, :] + hot[TM - 1:TM, :]
    carry_ref[...] = total
    cnt_ref[...] = jnp.broadcast_to(total, cnt_ref.shape)
    info_ref[...] = jnp.where(lane == 0, e1.astype(F32), jnp.where(lane == 1, e2.astype(F32),
                              jnp.where(lane == 2, r1, jnp.where(lane == 3, r2,
                                        jnp.where(lane == 4, w1, jnp.where(lane == 5, w2, 0.0))))))


def _route(h, g, wr):
    R = h.shape[0]
    rows = lambda i: (i, 0)
    return pl.pallas_call(
        _route_kernel,
        grid=(R // TM,),
        in_specs=[pl.BlockSpec((TM, D_MODEL), rows), pl.BlockSpec((1, D_MODEL), lambda i: (0, 0)),
                  pl.BlockSpec((2, D_MODEL, LANES), lambda i: (0, 0, 0))],
        out_specs=[pl.BlockSpec((TM * ROW_TILE, LANES), rows), pl.BlockSpec((TM, LANES), rows),
                   pl.BlockSpec((8, LANES), lambda i: (0, 0))],
        out_shape=[jax.ShapeDtypeStruct((R * ROW_TILE, LANES), F32), jax.ShapeDtypeStruct((R, LANES), F32),
                   jax.ShapeDtypeStruct((8, LANES), F32)],
        scratch_shapes=[pltpu.VMEM((1, LANES), F32)],
        compiler_params=_cparams(("arbitrary",)),
        name="route",
    )(h, g, wr)


def _scatter_kernel(dest_ref, last_ref, src_ref, out_ref, zero_ref, sem):
    @pl.when(pl.program_id(0) == 0)
    def _():
        zero_ref[...] = jnp.zeros_like(zero_ref)

        def zero_copy(e):
            row = pl.multiple_of(last_ref[0, e] * (TME * ROW_TILE), TME * ROW_TILE)
            return pltpu.make_async_copy(zero_ref, out_ref.at[pl.ds(row, TME * ROW_TILE)], sem)

        for e in range(N_EXPERTS):
            @pl.when(last_ref[1, e] > 0)
            def _(e=e):
                zero_copy(e).start()
        for e in range(N_EXPERTS):
            @pl.when(last_ref[1, e] > 0)
            def _(e=e):
                zero_copy(e).wait()

        def spare_copy(tile):
            row = pl.multiple_of(tile * (TME * ROW_TILE), TME * ROW_TILE)
            return pltpu.make_async_copy(zero_ref, out_ref.at[pl.ds(row, TME * ROW_TILE)], sem)

        def start_spare(tile, c):
            spare_copy(tile).start()
            return c

        def wait_spare(tile, c):
            spare_copy(tile).wait()
            return c

        n_tiles = out_ref.shape[0] // (TME * ROW_TILE)
        lax.fori_loop(last_ref[2, 0], n_tiles, start_spare, 0)
        lax.fori_loop(last_ref[2, 0], n_tiles, wait_spare, 0)

    def copy(r, k):
        return pltpu.make_async_copy(
            src_ref.at[pl.ds(pl.multiple_of(r * ROW_TILE, ROW_TILE), ROW_TILE)],
            out_ref.at[pl.ds(pl.multiple_of(dest_ref[0, 0, 2 * r + k], ROW_TILE), ROW_TILE)], sem)

    def issue(r, c):
        copy(r, 0).start(priority=0)
        copy(r, 1).start(priority=1)
        return c

    lax.fori_loop(0, TM, issue, 0, unroll=DMA_UNROLL)
    for _ in range(2):
        pltpu.make_async_copy(src_ref, out_ref.at[pl.ds(0, TM * ROW_TILE)], sem).wait()


def _scatter_rows(dest, last_tiles, src, n_rows):
    R = src.shape[0] // ROW_TILE
    return pl.pallas_call(
        _scatter_kernel,
        grid=(R // TM,),
        in_specs=[pl.BlockSpec((1, 1, 2 * TM), lambda i: (i, 0, 0), memory_space=pltpu.SMEM),
                  pl.BlockSpec(memory_space=pltpu.SMEM),
                  pl.BlockSpec((TM * ROW_TILE, LANES), lambda i: (i, 0))],
        out_specs=pl.BlockSpec(memory_space=pl.ANY),
        out_shape=jax.ShapeDtypeStruct((n_rows * ROW_TILE, LANES), src.dtype),
        scratch_shapes=[pltpu.VMEM((TME * ROW_TILE, LANES), F32), pltpu.SemaphoreType.DMA(())],
        compiler_params=pltpu.CompilerParams(dimension_semantics=("arbitrary",), vmem_limit_bytes=VMEM_LIMIT,
                                             has_side_effects=True),
        name="scatter_rows",
    )(dest, last_tiles, src)


def _expert_kernel(te_ref, act_ref, x_ref, wg_ref, wu_ref, wd_ref, y_ref, xb_ref, acc_ref):
    i = pl.program_id(0)
    f = pl.program_id(1)
    del te_ref

    @pl.when(f == 0)
    def _():
        xb_ref[...] = _load_row_tiles(x_ref, TME).astype(BF16)
        acc_ref[...] = jnp.zeros_like(acc_ref)

    @pl.when(act_ref[i] > 0)
    def _():
        acc_ref[...] = _swiglu_acc(xb_ref[...], wg_ref, wu_ref, wd_ref, acc_ref[...])

    @pl.when(f == pl.num_programs(1) - 1)
    def _():
        _store_row_tiles(y_ref, acc_ref[...])


def _experts(tile_expert, tile_active, xs, wg, wu, wd):
    mt = tile_expert.shape[0]
    dffe = wg.shape[2]
    grid_spec = pltpu.PrefetchScalarGridSpec(
        num_scalar_prefetch=2,
        grid=(mt, dffe // TF_MOE),
        in_specs=[
            pl.BlockSpec((TME * ROW_TILE, LANES), lambda i, f, te, act: (i, 0)),
            pl.BlockSpec((None, D_MODEL, TF_MOE), lambda i, f, te, act: (te[i], 0, f)),
            pl.BlockSpec((None, D_MODEL, TF_MOE), lambda i, f, te, act: (te[i], 0, f)),
            pl.BlockSpec((None, TF_MOE, D_MODEL), lambda i, f, te, act: (te[i], f, 0)),
        ],
        out_specs=pl.BlockSpec((TME * ROW_TILE, LANES), lambda i, f, te, act: (i, 0)),
        scratch_shapes=[pltpu.VMEM((TME, D_MODEL), BF16), pltpu.VMEM((TME, D_MODEL), F32)],
    )
    return pl.pallas_call(
        _expert_kernel,
        grid_spec=grid_spec,
        out_shape=jax.ShapeDtypeStruct((mt * TME * ROW_TILE, LANES), F32),
        compiler_params=_cparams(("arbitrary", "arbitrary")),
        name="experts",
    )(tile_expert, tile_active, xs, wg, wu, wd)


def _combine_kernel(dest_ref, h_ref, info_ref, g_ref, y_ref, out_ref, buf_ref, sem, *, final):
    def copy(r, k):
        return pltpu.make_async_copy(
            y_ref.at[pl.ds(pl.multiple_of(dest_ref[0, 0, 2 * r + k], ROW_TILE), ROW_TILE)],
            buf_ref.at[k, pl.ds(pl.multiple_of(r * ROW_TILE, ROW_TILE), ROW_TILE)], sem)

    def issue(r, c):
        copy(r, 0).start(priority=0)
        copy(r, 1).start(priority=1)
        return c

    lax.fori_loop(0, TM, issue, 0, unroll=DMA_UNROLL)
    for k in range(2):
        pltpu.make_async_copy(y_ref.at[pl.ds(0, TM * ROW_TILE)], buf_ref.at[k], sem).wait()
    info = info_ref[...]
    out = (h_ref[...] + info[:, 4:5] * _load_row_tiles(buf_ref.at[0], TM)
           + info[:, 5:6] * _load_row_tiles(buf_ref.at[1], TM))
    out_ref[...] = _rms(out, g_ref[...]) if final else out


def _combine(dest, h, info, y, final_g=None, n_rows=None):
    final = final_g is not None
    n_rows = n_rows if final else h.shape[0]
    g = final_g if final else jnp.ones((1, D_MODEL), F32)
    rows = lambda i: (i, 0)
    return pl.pallas_call(
        functools.partial(_combine_kernel, final=final),
        grid=(n_rows // TM,),
        in_specs=[pl.BlockSpec((1, 1, 2 * TM), lambda i: (i, 0, 0), memory_space=pltpu.SMEM),
                  pl.BlockSpec((TM, D_MODEL), rows), pl.BlockSpec((TM, LANES), rows),
                  pl.BlockSpec((1, D_MODEL), lambda i: (0, 0)), pl.BlockSpec(memory_space=pl.ANY)],
        out_specs=pl.BlockSpec((TM, D_MODEL), rows),
        out_shape=jax.ShapeDtypeStruct((n_rows, D_MODEL), F32),
        scratch_shapes=[pltpu.VMEM((2, TM * ROW_TILE, LANES), F32), pltpu.SemaphoreType.DMA(())],
        compiler_params=_cparams(("arbitrary",)),
        name="combine",
    )(dest, h, info, g, y)


def _moe_ffn(h, g, wr, wg, wu, wd, final_g=None, n_rows=None):
    R = h.shape[0]
    hn, info, counts = _route(h, g, wr)
    cnt = counts[0, :N_EXPERTS].astype(jnp.int32)
    padded = ((cnt + TME - 1) // TME) * TME
    ends = jnp.cumsum(padded)
    starts = ends - padded
    mt = (2 * R) // TME + N_EXPERTS
    tile_row = jnp.arange(mt, dtype=jnp.int32) * TME
    tile_expert = jnp.minimum(jnp.searchsorted(ends, tile_row, side="right"), N_EXPERTS - 1).astype(jnp.int32)
    tile_active = (tile_row < ends[-1]).astype(jnp.int32)
    e12 = info[:, 0:2].astype(jnp.int32)
    dest = ((starts[e12] + info[:, 2:4].astype(jnp.int32)) * ROW_TILE).reshape(R // TM, 1, 2 * TM)
    last_tiles = jnp.stack([jnp.maximum(ends // TME - 1, 0), (padded > 0).astype(jnp.int32),
                            jnp.broadcast_to(ends[-1] // TME, (N_EXPERTS,))]).astype(jnp.int32)
    xs = _scatter_rows(dest, last_tiles, hn, mt * TME)
    y = _experts(tile_expert, tile_active, xs, wg, wu, wd)
    return _combine(dest, h, info, y, final_g, n_rows)


def _final_kernel(h_ref, g_ref, out_ref):
    out_ref[...] = _rms(h_ref[...], g_ref[...])


def _final_norm(h, g, n_rows):
    rows = lambda i: (i, 0)
    return pl.pallas_call(
        _final_kernel,
        grid=(n_rows // TM,),
        in_specs=[pl.BlockSpec((TM, D_MODEL), rows), pl.BlockSpec((1, D_MODEL), lambda i: (0, 0))],
        out_specs=pl.BlockSpec((TM, D_MODEL), rows),
        out_shape=jax.ShapeDtypeStruct((n_rows, D_MODEL), F32),
        compiler_params=_cparams(("arbitrary",)),
        name="final_norm",
    )(h, g)


def _forget_column_placement():
    row = jnp.arange(3 * LANES)[:, None]
    col = jnp.arange(LANES)[None, :]
    term, head = row // LANES, row % LANES
    valid = head < FOX_HEADS
    plus = valid & (col == 6 * head + term)
    minus = valid & (col == 6 * head + 3 + term)
    return (plus.astype(F32) - minus.astype(F32)).astype(BF16)


def _rope_table(S):
    rd = DIFF_DH // 4
    inv = ROPE_THETA ** (-jnp.arange(0, rd, 2, dtype=F32) / rd)
    pos = jnp.concatenate([jnp.arange(N_META, N_META + S, dtype=F32), jnp.arange(TM, dtype=F32)])
    ang = pos[:, None] * inv[None, :]
    cos, sin = jnp.cos(ang), jnp.sin(ang)
    n = pos.shape[0]
    cos_t = jnp.tile(jnp.concatenate([cos, cos, jnp.ones((n, 48), F32)], axis=1), (1, 2))
    sin_lo = jnp.tile(jnp.concatenate([-sin, jnp.zeros((n, 56), F32)], axis=1), (1, 2))
    sin_hi = jnp.tile(jnp.concatenate([jnp.zeros((n, 8), F32), sin, jnp.zeros((n, 48), F32)], axis=1), (1, 2))
    return jnp.concatenate([cos_t, sin_lo, sin_hi], axis=1)


def kernel(x, meta_tokens, norm_mix_g, w_in, b_forget, diff_lambda, diff_subln_g, w_branch_diff, w_branch_fox,
           w_out, norm_ffn_g, ffn_w_gate, ffn_w_up, ffn_w_down, moe_router, moe_w_gate, moe_w_up, moe_w_down,
           final_norm_g):
    B, S, D = x.shape
    depth = w_in.shape[0]
    assert D == D_MODEL and S % TQ == 0 and meta_tokens.shape[0] == N_META
    nq = S // TM
    h = jnp.concatenate([x.reshape(B * S, D), meta_tokens.astype(x.dtype),
                         jnp.zeros((TQ - N_META, D), x.dtype)], axis=0)
    rope = _rope_table(S)
    g_final = final_norm_g.astype(F32).reshape(1, D)
    e6 = _forget_column_placement()
    scale = DIFF_DH ** -0.5 * LOG2E
    for layer in range(depth):
        lam_init = 0.8 - 0.6 * math.exp(-0.3 * layer)
        w = w_in[layer]
        dq, dk, dv, fq, fk, fv, ff, ga, gb = jnp.split(
            w, [HALF, 2 * HALF, 3 * HALF, 4 * HALF, 5 * HALF, 6 * HALF, 6 * HALF + FOX_HEADS,
                6 * HALF + FOX_HEADS + D_MODEL], axis=1)
        w1 = jnp.concatenate([jnp.pad(ff, ((0, 0), (0, LANES - FOX_HEADS))), dq * scale, dk], axis=1).astype(BF16)
        w2 = jnp.concatenate([dv, fv], axis=1).T.astype(BF16)
        w3 = jnp.concatenate([fq * scale, fk], axis=1).astype(BF16)
        bfp = jnp.pad(b_forget[layer].astype(F32), (0, LANES - FOX_HEADS)).reshape(1, LANES)
        g_mix = norm_mix_g[layer].astype(F32).reshape(1, D)
        dq_a, dk_a, vt_a, fq_a, fk_a = _inproj(h, g_mix, w1, w2, w3, e6, bfp, rope, nq=nq)

        lp = diff_lambda[layer].astype(F32)
        lam = jnp.exp(jnp.sum(lp[0] * lp[1])) - jnp.exp(jnp.sum(lp[2] * lp[3])) + lam_init
        par = jnp.zeros((8, LANES), F32)
        par = par.at[1].set(diff_subln_g[layer].astype(F32)).at[2].set(1.0 - lam_init).at[3].set(lam)
        o_a = _attention(dq_a, dk_a, vt_a, par, diff=True, nq=nq, nb=B)
        o_b = _attention(fq_a, fk_a, vt_a, None, diff=False, nq=nq, nb=B)

        wgate = jnp.concatenate([ga, gb], axis=1).astype(BF16)
        h = _mixout(h, o_a, o_b, g_mix, wgate, w_branch_diff[layer].astype(BF16),
                    w_branch_fox[layer].astype(BF16), w_out[layer].astype(BF16))

        g_ffn = norm_ffn_g[layer].astype(F32).reshape(1, D)
        jj = layer // 2
        if layer % 2 == 0:
            h = _dense_ffn(h, g_ffn, ffn_w_gate[jj].astype(BF16), ffn_w_up[jj].astype(BF16),
                           ffn_w_down[jj].astype(BF16))
        else:
            r_hi, r_mid, _ = _split3(jnp.pad(moe_router[jj].astype(F32), ((0, 0), (0, LANES - N_EXPERTS))))
            wr = jnp.stack([r_hi, r_mid]).astype(BF16)
            last = layer == depth - 1
            h = _moe_ffn(h, g_ffn, wr, moe_w_gate[jj].astype(BF16), moe_w_up[jj].astype(BF16),
                         moe_w_down[jj].astype(BF16), g_final if last else None, B * S)
    out = h if depth % 2 == 0 else _final_norm(h, g_final, B * S)
    return out.reshape(B, S, D)
```

```python
import functools
import math

import jax
import jax.numpy as jnp
from jax import lax
from jax.experimental import pallas as pl
from jax.experimental.pallas import tpu as pltpu

D_MODEL = 1024
N_META = 16
ROPE_THETA = 500000.0
RMS_EPS = 1e-6
NEG_INF = -1e30

DIFF_HEADS = 4
DIFF_DH = 64
FOX_HEADS = 8
FOX_DH = 64
HALF = 512
N_EXPERTS = 8
LANES = 128

TM = 512
KEY_TILES_PER_Q = 2
TQ = KEY_TILES_PER_Q * TM
TME = 512
TF_MOE = 1792
FF_CHUNK = 256
ROW_TILE = 8
DMA_UNROLL = 8
LOG2E = 1.4426950408889634
ONES_ROWS = 16
VMEM_LIMIT = 56 * 1024 * 1024

F32 = jnp.float32
BF16 = jnp.bfloat16


def _cparams(sem):
    return pltpu.CompilerParams(dimension_semantics=sem, vmem_limit_bytes=VMEM_LIMIT)


def _rms(x, g):
    ms = jnp.mean(x * x, axis=-1, keepdims=True)
    return x * lax.rsqrt(ms + RMS_EPS) * g


def _split3(x):
    hi = x.astype(BF16).astype(F32)
    r = x - hi
    mid = r.astype(BF16).astype(F32)
    lo = (r - mid).astype(BF16).astype(F32)
    return hi, mid, lo


def _tri_cumsum(x, inclusive):
    n = x.shape[0]
    row = lax.broadcasted_iota(jnp.int32, (n, n), 0)
    col = lax.broadcasted_iota(jnp.int32, (n, n), 1)
    tri = jnp.where((col <= row) if inclusive else (col < row), 1.0, 0.0).astype(BF16)
    parts = jnp.concatenate([p.astype(BF16) for p in _split3(x)], axis=1)
    out = jnp.dot(tri, parts, preferred_element_type=F32)
    return out[:, :LANES] + out[:, LANES:2 * LANES] + out[:, 2 * LANES:]


def _store_row_tiles(ref, x):
    rows = x.shape[0]
    for s in range(ROW_TILE):
        ref[pl.ds(s, rows, stride=ROW_TILE), :] = x[:, LANES * s:LANES * (s + 1)]


def _load_row_tiles(ref, rows):
    return jnp.concatenate([ref[pl.ds(s, rows, stride=ROW_TILE), :] for s in range(ROW_TILE)], axis=1)


def _inproj_kernel(h_ref, g_ref, w1_ref, w2_ref, w3_ref, e6_ref, bf_ref, rope_ref,
                   dq_ref, dk_ref, vt_ref, fq_ref, fk_ref, carry_ref, mcarry_ref, *, nq):
    i = pl.program_id(0)

    @pl.when(i == 0)
    def _():
        carry_ref[...] = jnp.zeros_like(carry_ref)
        mcarry_ref[...] = jnp.zeros_like(mcarry_ref)

    hb = _rms(h_ref[...], g_ref[...]).astype(BF16)

    z1 = jnp.dot(hb, w1_ref[...], preferred_element_type=F32)

    z4 = z1[:, :LANES] + bf_ref[...]
    lane = lax.broadcasted_iota(jnp.int32, z4.shape, 1)
    logf = jnp.minimum(z4, 0.0) - jnp.log1p(jnp.exp(-jnp.abs(z4)))
    logf = jnp.where(lane < FOX_HEADS, logf, 0.0)
    j_in_batch = lax.rem(jnp.maximum(i - 1, 0), nq)
    base = jnp.where(i == 0, 0.0, jnp.where(j_in_batch == 0, mcarry_ref[...], carry_ref[...]))
    c = _tri_cumsum(logf, inclusive=True) + base

    vt_ref[0] = lax.dot_general(w2_ref[...], hb, (((1,), (1,)), ((), ())),
                                preferred_element_type=F32).astype(BF16)

    parts = jnp.concatenate([p.astype(BF16) for p in _split3(c * LOG2E)], axis=1)
    c6 = jnp.dot(parts, e6_ref[...], preferred_element_type=F32)
    z3 = jnp.dot(hb, w3_ref[...], preferred_element_type=F32)

    cos_t = rope_ref[:, 0:LANES]
    sin_lo = rope_ref[:, LANES:2 * LANES]
    sin_hi = rope_ref[:, 2 * LANES:3 * LANES]
    for j in range(8):
        zj = z1[:, LANES * (j + 1):LANES * (j + 2)]
        rot = zj * cos_t + pltpu.roll(zj, LANES - 8, 1) * sin_lo + pltpu.roll(zj, 8, 1) * sin_hi
        dst = dq_ref if j < 4 else dk_ref
        dst[:, LANES * (j % 4):LANES * (j % 4 + 1)] = rot.astype(BF16)

    @pl.when(i == 0)
    def _():
        mcarry_ref[...] = c[N_META - 1:N_META, :]

    carry_ref[...] = c[TM - 1:TM, :]

    for hd in range(FOX_HEADS):
        off = 64 if hd % 2 == 0 else 0
        moved = pltpu.roll(c6, (off - 6 * hd) % LANES, 1)
        first3 = (lane >= off) & (lane < off + 3)
        last3 = (lane >= off + 3) & (lane < off + 6)
        aug_q = jnp.where(first3, moved, jnp.where(last3, 1.0, 0.0))
        aug_k = jnp.where(last3, moved, jnp.where(first3, 1.0, 0.0))
        slab = hd // 2
        keep = (lane < 64) if hd % 2 == 0 else (lane >= 64)
        zq = z3[:, LANES * slab:LANES * (slab + 1)]
        zk = z3[:, HALF + LANES * slab:HALF + LANES * (slab + 1)]
        fq_ref[:, LANES * hd:LANES * (hd + 1)] = jnp.where(keep, zq, aug_q).astype(BF16)
        fk_ref[:, LANES * hd:LANES * (hd + 1)] = jnp.where(keep, zk, aug_k).astype(BF16)


def _inproj(h, g, w1, w2, w3, e6, bfp, rope, *, nq):
    R = h.shape[0]
    nt = R // TM
    n_real = nt - TQ // TM
    rows = lambda i: (jnp.where(i == 0, n_real, jnp.where(i <= n_real, i - 1, i)), 0)
    rope_rows = lambda i: (jnp.where(i == 0, nq, lax.rem(jnp.maximum(i - 1, 0), nq)), 0)
    const = lambda i: (0, 0)
    out_sd = lambda w: jax.ShapeDtypeStruct((R, w), BF16)
    return pl.pallas_call(
        functools.partial(_inproj_kernel, nq=nq),
        grid=(nt,),
        in_specs=[
            pl.BlockSpec((TM, D_MODEL), rows),
            pl.BlockSpec((1, D_MODEL), const),
            pl.BlockSpec((D_MODEL, LANES + 2 * HALF), const),
            pl.BlockSpec((2 * HALF, D_MODEL), const),
            pl.BlockSpec((D_MODEL, 2 * HALF), const),
            pl.BlockSpec((3 * LANES, LANES), const),
            pl.BlockSpec((1, LANES), const),
            pl.BlockSpec((TM, 3 * LANES), rope_rows),
        ],
        out_specs=[
            pl.BlockSpec((TM, HALF), rows), pl.BlockSpec((TM, HALF), rows),
            pl.BlockSpec((1, 2 * HALF, TM), lambda i: (rows(i)[0], 0, 0)),
            pl.BlockSpec((TM, 2 * HALF), rows), pl.BlockSpec((TM, 2 * HALF), rows),
        ],
        out_shape=[out_sd(HALF), out_sd(HALF), jax.ShapeDtypeStruct((nt, 2 * HALF, TM), BF16),
                   out_sd(2 * HALF), out_sd(2 * HALF)],
        scratch_shapes=[pltpu.VMEM((1, LANES), F32), pltpu.VMEM((1, LANES), F32)],
        compiler_params=_cparams(("arbitrary",)),
        name="inproj",
    )(h, g, w1, w2, w3, e6, bfp, rope)


def _attn_kernel(*refs, diff, nq, nb):
    if diff:
        q_ref, k_ref, vt_ref, km_ref, vtm_ref, par_ref, o_ref, acc_ref, m_ref, q_scr, sa_ref, sb_ref = refs
    else:
        q_ref, k_ref, vt_ref, km_ref, vtm_ref, o_ref, acc_ref, m_ref, q_scr, sa_ref, sb_ref = refs
    t = pl.program_id(1)
    nqt = nq // KEY_TILES_PER_Q
    is_real = t < nb * nqt
    jq = lax.rem(t, nqt)
    dv = acc_ref.shape[1] - ONES_ROWS

    for sub in range(2):
        if diff:
            q = q_ref[...]
            lane = lax.broadcasted_iota(jnp.int32, q.shape, 1)
            q_scr[sub] = jnp.where((lane < 64) if sub == 0 else (lane >= 64), q, jnp.zeros_like(q))
        else:
            q_scr[sub] = q_ref[:, LANES * sub:LANES * (sub + 1)]
        m_ref[sub] = jnp.full(m_ref.shape[1:], NEG_INF, F32)
        acc_ref[sub] = jnp.zeros(acc_ref.shape[1:], F32)

    def keys_of(sub, k_tile):
        return k_tile if diff else k_tile[:, LANES * sub:LANES * (sub + 1)]

    def values_of(sub, vt_tile):
        ones = jnp.ones((ONES_ROWS, vt_tile.shape[1]), BF16)
        vt = vt_tile if diff else vt_tile[dv * sub:dv * (sub + 1), :]
        return jnp.concatenate([vt, ones], axis=0)

    def scores(sub, k, q0):
        return lax.dot_general(k, q_scr[sub, q0:, :], (((1,), (1,)), ((), ())), preferred_element_type=F32)

    def update(sub, st, vt, mask, q0):
        if mask is not None:
            st = jnp.where(mask, st, NEG_INF)
        m_prev = m_ref[sub, :, q0:]
        m_new = jnp.maximum(m_prev, jnp.max(st, axis=0, keepdims=True))
        p = jnp.exp2(st - m_new).astype(BF16)
        if p.shape[0] < vt.shape[1]:
            p = jnp.concatenate([p, jnp.zeros((vt.shape[1] - p.shape[0], p.shape[1]), BF16)], axis=0)
        acc_ref[sub, :, q0:] = (jnp.exp2(m_prev - m_new) * acc_ref[sub, :, q0:]
                                + jnp.dot(vt, p, preferred_element_type=F32))
        m_ref[sub, :, q0:] = m_new

    def scores_into(buf, tile, q0):
        k_tile = k_ref[pl.ds(pl.multiple_of(tile * TM, TM), TM), :]
        for sub in range(2):
            buf[sub, :, q0:] = scores(sub, keys_of(sub, k_tile), q0)

    def update_from(buf, tile, causal, q0):
        vt_tile = vt_ref[tile]
        mask = None
        if causal:
            key = lax.broadcasted_iota(jnp.int32, (TM, TQ - q0), 0)
            qry = lax.broadcasted_iota(jnp.int32, (TM, TQ - q0), 1)
            mask = key <= qry
        for sub in range(2):
            update(sub, buf[sub, :, q0:], values_of(sub, vt_tile), mask, q0)

    key = lax.broadcasted_iota(jnp.int32, (N_META, TQ), 0)
    qry = lax.broadcasted_iota(jnp.int32, (N_META, TQ), 1)
    meta_mask = key <= jnp.where(is_real, N_META - 1, qry)
    km_tile = km_ref[...]
    vtm_tile = vtm_ref[0][:, :LANES]
    meta_scores = [scores(sub, keys_of(sub, km_tile), 0) for sub in range(2)]

    scores_into(sa_ref, 0, 0)
    for sub in range(2):
        update(sub, meta_scores[sub], values_of(sub, vtm_tile), meta_mask, 0)

    def pair(i, carry):
        scores_into(sb_ref, 2 * i + 1, 0)
        update_from(sa_ref, 2 * i, False, 0)
        scores_into(sa_ref, 2 * i + 2, 0)
        update_from(sb_ref, 2 * i + 1, False, 0)
        return carry

    lax.fori_loop(0, jnp.where(is_real, jq, 0), pair, 0)

    @pl.when(is_real)
    def _():
        scores_into(sb_ref, 2 * jq + 1, TM)
        update_from(sa_ref, 2 * jq, True, 0)
        update_from(sb_ref, 2 * jq + 1, True, TM)

    a0 = acc_ref[0]
    a1 = acc_ref[1]
    o0 = a0[:dv] / a0[dv:dv + 1]
    o1 = a1[:dv] / a1[dv:dv + 1]
    if diff:
        d = (o0 - par_ref[3:4, 0:1] * o1).T
        o_ref[...] = (_rms(d, par_ref[1:2, :]) * par_ref[2:3, :]).astype(BF16)
    else:
        o_ref[...] = jnp.concatenate([o0, o1], axis=0).T.astype(BF16)


def _attention(q, k, vt, par, *, diff, nq, nb):
    R = q.shape[0]
    S = nq * TM
    ntq = R // TQ
    qw = LANES if diff else 2 * LANES
    voff = 0 if diff else HALF // LANES
    meta_blk = (nb * S) // N_META
    batch_of = lambda t: jnp.minimum(t // (nq // KEY_TILES_PER_Q), nb - 1)
    in_specs = [
        pl.BlockSpec((TQ, qw), lambda p, t: (t, p)),
        pl.BlockSpec((S, qw), lambda p, t: (batch_of(t), p)),
        pl.BlockSpec((nq, LANES, TM), lambda p, t: (batch_of(t), p + voff, 0)),
        pl.BlockSpec((N_META, qw), lambda p, t: (meta_blk, p)),
        pl.BlockSpec((1, LANES, TM), lambda p, t: (nb * nq, p + voff, 0)),
    ]
    args = [q, k, vt, k, vt]
    if diff:
        in_specs.append(pl.BlockSpec((8, LANES), lambda p, t: (0, 0)))
        args.append(par)
    acc_rows = (LANES if diff else LANES // 2) + ONES_ROWS
    return pl.pallas_call(
        functools.partial(_attn_kernel, diff=diff, nq=nq, nb=nb),
        grid=(4, ntq),
        in_specs=in_specs,
        out_specs=pl.BlockSpec((TQ, LANES), lambda p, t: (t, p)),
        out_shape=jax.ShapeDtypeStruct((R, HALF), BF16),
        scratch_shapes=[pltpu.VMEM((2, acc_rows, TQ), F32), pltpu.VMEM((2, 1, TQ), F32),
                        pltpu.VMEM((2, TQ, LANES), BF16),
                        pltpu.VMEM((2, TM, TQ), F32), pltpu.VMEM((2, TM, TQ), F32)],
        compiler_params=_cparams(("arbitrary", "arbitrary")),
        name="diff_attn" if diff else "fox_attn",
    )(*args)


def _mixout_kernel(h_ref, oa_ref, ob_ref, g_ref, wg_ref, wbd_ref, wbf_ref, wo_ref, out_ref):
    x = h_ref[...]
    hb = _rms(x, g_ref[...]).astype(BF16)
    gates = jax.nn.sigmoid(jnp.dot(hb, wg_ref[...], preferred_element_type=F32))
    a = jnp.dot(oa_ref[...], wbd_ref[...], preferred_element_type=F32)
    b = jnp.dot(ob_ref[...], wbf_ref[...], preferred_element_type=F32)
    merged = gates[:, :D_MODEL] * a + gates[:, D_MODEL:] * b
    out_ref[...] = x + jnp.dot(merged.astype(BF16), wo_ref[...], preferred_element_type=F32)


def _mixout(h, oa, ob, g, wg, wbd, wbf, wo):
    R = h.shape[0]
    rows = lambda i: (i, 0)
    const = lambda i: (0, 0)
    return pl.pallas_call(
        _mixout_kernel,
        grid=(R // TM,),
        in_specs=[
            pl.BlockSpec((TM, D_MODEL), rows), pl.BlockSpec((TM, HALF), rows), pl.BlockSpec((TM, HALF), rows),
            pl.BlockSpec((1, D_MODEL), const), pl.BlockSpec((D_MODEL, 2 * D_MODEL), const),
            pl.BlockSpec((HALF, D_MODEL), const), pl.BlockSpec((HALF, D_MODEL), const),
            pl.BlockSpec((D_MODEL, D_MODEL), const),
        ],
        out_specs=pl.BlockSpec((TM, D_MODEL), rows),
        out_shape=jax.ShapeDtypeStruct((R, D_MODEL), F32),
        compiler_params=_cparams(("arbitrary",)),
        name="mixout",
    )(h, oa, ob, g, wg, wbd, wbf, wo)


def _swiglu_acc(xb, wg_ref, wu_ref, wd_ref, acc):
    def gate_up(c):
        sl = slice(FF_CHUNK * c, FF_CHUNK * (c + 1))
        return (jnp.dot(xb, wg_ref[:, sl], preferred_element_type=F32),
                jnp.dot(xb, wu_ref[:, sl], preferred_element_type=F32))

    nf = wg_ref.shape[1] // FF_CHUNK
    nxt = gate_up(0)
    for c in range(nf):
        gate, up = nxt
        if c + 1 < nf:
            nxt = gate_up(c + 1)
        mid = (gate * jax.nn.sigmoid(gate) * up).astype(BF16)
        acc = acc + jnp.dot(mid, wd_ref[FF_CHUNK * c:FF_CHUNK * (c + 1), :], preferred_element_type=F32)
    return acc


def _dense_ffn_kernel(h_ref, g_ref, wg_ref, wu_ref, wd_ref, out_ref):
    x = h_ref[...]
    hb = _rms(x, g_ref[...]).astype(BF16)
    out_ref[...] = _swiglu_acc(hb, wg_ref, wu_ref, wd_ref, x)


def _dense_ffn(h, g, wg, wu, wd):
    R = h.shape[0]
    dff = wg.shape[1]
    rows = lambda i: (i, 0)
    const = lambda i: (0, 0)
    return pl.pallas_call(
        _dense_ffn_kernel,
        grid=(R // TM,),
        in_specs=[
            pl.BlockSpec((TM, D_MODEL), rows), pl.BlockSpec((1, D_MODEL), const),
            pl.BlockSpec((D_MODEL, dff), const), pl.BlockSpec((D_MODEL, dff), const),
            pl.BlockSpec((dff, D_MODEL), const),
        ],
        out_specs=pl.BlockSpec((TM, D_MODEL), rows),
        out_shape=jax.ShapeDtypeStruct((R, D_MODEL), F32),
        compiler_params=_cparams(("arbitrary",)),
        name="dense_ffn",
    )(h, g, wg, wu, wd)


def _route_kernel(h_ref, g_ref, wr_ref, hn_ref, info_ref, cnt_ref, carry_ref):
    i = pl.program_id(0)

    @pl.when(i == 0)
    def _():
        carry_ref[...] = jnp.zeros_like(carry_ref)

    hn = _rms(h_ref[...], g_ref[...])
    _store_row_tiles(hn_ref, hn)
    h_hi, h_mid, _ = _split3(hn)
    logits = (jnp.dot(h_hi.astype(BF16), wr_ref[0], preferred_element_type=F32)
              + jnp.dot(h_mid.astype(BF16), wr_ref[0], preferred_element_type=F32)
              + jnp.dot(h_hi.astype(BF16), wr_ref[1], preferred_element_type=F32))
    lane = lax.broadcasted_iota(jnp.int32, logits.shape, 1)
    logits = jnp.where(lane < N_EXPERTS, logits, -jnp.inf)
    v1 = jnp.max(logits, axis=-1, keepdims=True)
    e1 = jnp.min(jnp.where(logits == v1, lane, LANES), axis=-1, keepdims=True)
    rest = jnp.where(lane == e1, -jnp.inf, logits)
    v2 = jnp.max(rest, axis=-1, keepdims=True)
    e2 = jnp.min(jnp.where(rest == v2, lane, LANES), axis=-1, keepdims=True)
    ex = jnp.exp(v2 - v1)
    w1 = 1.0 / (1.0 + ex)
    w2 = ex / (1.0 + ex)
    hot1 = jnp.where(lane == e1, 1.0, 0.0)
    hot2 = jnp.where(lane == e2, 1.0, 0.0)
    hot = hot1 + hot2
    before = _tri_cumsum(hot, inclusive=False) + carry_ref[...]
    r1 = jnp.sum(before * hot1, axis=-1, keepdims=True)
    r2 = jnp.sum(before * hot2, axis=-1, keepdims=True)
    total = before[TM - 1:TM, :] + hot[TM - 1:TM, :]
    carry_ref[...] = total
    cnt_ref[...] = jnp.broadcast_to(total, cnt_ref.shape)
    info_ref[...] = jnp.where(lane == 0, e1.astype(F32), jnp.where(lane == 1, e2.astype(F32),
                              jnp.where(lane == 2, r1, jnp.where(lane == 3, r2,
                                        jnp.where(lane == 4, w1, jnp.where(lane == 5, w2, 0.0))))))


def _route(h, g, wr):
    R = h.shape[0]
    rows = lambda i: (i, 0)
    return pl.pallas_call(
        _route_kernel,
        grid=(R // TM,),
        in_specs=[pl.BlockSpec((TM, D_MODEL), rows), pl.BlockSpec((1, D_MODEL), lambda i: (0, 0)),
                  pl.BlockSpec((2, D_MODEL, LANES), lambda i: (0, 0, 0))],
        out_specs=[pl.BlockSpec((TM * ROW_TILE, LANES), rows), pl.BlockSpec((TM, LANES), rows),
                   pl.BlockSpec((8, LANES), lambda i: (0, 0))],
        out_shape=[jax.ShapeDtypeStruct((R * ROW_TILE, LANES), F32), jax.ShapeDtypeStruct((R, LANES), F32),
                   jax.ShapeDtypeStruct((8, LANES), F32)],
        scratch_shapes=[pltpu.VMEM((1, LANES), F32)],
        compiler_params=_cparams(("arbitrary",)),
        name="route",
    )(h, g, wr)


def _scatter_kernel(dest_ref, last_ref, src_ref, out_ref, zero_ref, sem):
    @pl.when(pl.program_id(0) == 0)
    def _():
        zero_ref[...] = jnp.zeros_like(zero_ref)

        def zero_copy(e):
            row = pl.multiple_of(last_ref[0, e] * (TME * ROW_TILE), TME * ROW_TILE)
            return pltpu.make_async_copy(zero_ref, out_ref.at[pl.ds(row, TME * ROW_TILE)], sem)

        for e in range(N_EXPERTS):
            @pl.when(last_ref[1, e] > 0)
            def _(e=e):
                zero_copy(e).start()
        for e in range(N_EXPERTS):
            @pl.when(last_ref[1, e] > 0)
            def _(e=e):
                zero_copy(e).wait()

        def spare_copy(tile):
            row = pl.multiple_of(tile * (TME * ROW_TILE), TME * ROW_TILE)
            return pltpu.make_async_copy(zero_ref, out_ref.at[pl.ds(row, TME * ROW_TILE)], sem)

        def start_spare(tile, c):
            spare_copy(tile).start()
            return c

        def wait_spare(tile, c):
            spare_copy(tile).wait()
            return c

        n_tiles = out_ref.shape[0] // (TME * ROW_TILE)
        lax.fori_loop(last_ref[2, 0], n_tiles, start_spare, 0)
        lax.fori_loop(last_ref[2, 0], n_tiles, wait_spare, 0)

    def copy(r, k):
        return pltpu.make_async_copy(
            src_ref.at[pl.ds(pl.multiple_of(r * ROW_TILE, ROW_TILE), ROW_TILE)],
            out_ref.at[pl.ds(pl.multiple_of(dest_ref[0, 0, 2 * r + k], ROW_TILE), ROW_TILE)], sem)

    def issue(r, c):
        copy(r, 0).start(priority=0)
        copy(r, 1).start(priority=1)
        return c

    lax.fori_loop(0, TM, issue, 0, unroll=DMA_UNROLL)
    for _ in range(2):
        pltpu.make_async_copy(src_ref, out_ref.at[pl.ds(0, TM * ROW_TILE)], sem).wait()


def _scatter_rows(dest, last_tiles, src, n_rows):
    R = src.shape[0] // ROW_TILE
    return pl.pallas_call(
        _scatter_kernel,
        grid=(R // TM,),
        in_specs=[pl.BlockSpec((1, 1, 2 * TM), lambda i: (i, 0, 0), memory_space=pltpu.SMEM),
                  pl.BlockSpec(memory_space=pltpu.SMEM),
                  pl.BlockSpec((TM * ROW_TILE, LANES), lambda i: (i, 0))],
        out_specs=pl.BlockSpec(memory_space=pl.ANY),
        out_shape=jax.ShapeDtypeStruct((n_rows * ROW_TILE, LANES), src.dtype),
        scratch_shapes=[pltpu.VMEM((TME * ROW_TILE, LANES), F32), pltpu.SemaphoreType.DMA(())],
        compiler_params=pltpu.CompilerParams(dimension_semantics=("arbitrary",), vmem_limit_bytes=VMEM_LIMIT,
                                             has_side_effects=True),
        name="scatter_rows",
    )(dest, last_tiles, src)


def _expert_kernel(te_ref, act_ref, x_ref, wg_ref, wu_ref, wd_ref, y_ref, acc_ref):
    i = pl.program_id(0)
    f = pl.program_id(1)
    active = act_ref[i] > 0
    del te_ref

    def step(first):
        xb = _load_row_tiles(x_ref, TME).astype(BF16)
        prev = jnp.zeros(acc_ref.shape, F32) if first else acc_ref[...]
        acc = _swiglu_acc(xb, wg_ref, wu_ref, wd_ref, prev)
        acc_ref[...] = acc
        _store_row_tiles(y_ref, acc)

    @pl.when(active & (f == 0))
    def _():
        step(True)

    @pl.when(active & (f > 0))
    def _():
        step(False)

    @pl.when(jnp.logical_not(active))
    def _():
        y_ref[...] = jnp.zeros_like(y_ref)


def _experts(tile_expert, tile_active, xs, wg, wu, wd):
    mt = tile_expert.shape[0]
    dffe = wg.shape[2]
    grid_spec = pltpu.PrefetchScalarGridSpec(
        num_scalar_prefetch=2,
        grid=(mt, dffe // TF_MOE),
        in_specs=[
            pl.BlockSpec((TME * ROW_TILE, LANES), lambda i, f, te, act: (i, 0)),
            pl.BlockSpec((None, D_MODEL, TF_MOE), lambda i, f, te, act: (te[i], 0, f)),
            pl.BlockSpec((None, D_MODEL, TF_MOE), lambda i, f, te, act: (te[i], 0, f)),
            pl.BlockSpec((None, TF_MOE, D_MODEL), lambda i, f, te, act: (te[i], f, 0)),
        ],
        out_specs=pl.BlockSpec((TME * ROW_TILE, LANES), lambda i, f, te, act: (i, 0)),
        scratch_shapes=[pltpu.VMEM((TME, D_MODEL), F32)],
    )
    return pl.pallas_call(
        _expert_kernel,
        grid_spec=grid_spec,
        out_shape=jax.ShapeDtypeStruct((mt * TME * ROW_TILE, LANES), F32),
        compiler_params=_cparams(("arbitrary", "arbitrary")),
        name="experts",
    )(tile_expert, tile_active, xs, wg, wu, wd)


def _combine_kernel(dest_ref, h_ref, info_ref, g_ref, y_ref, out_ref, buf_ref, sem, *, final):
    def copy(r, k):
        return pltpu.make_async_copy(
            y_ref.at[pl.ds(pl.multiple_of(dest_ref[0, 0, 2 * r + k], ROW_TILE), ROW_TILE)],
            buf_ref.at[k, pl.ds(pl.multiple_of(r * ROW_TILE, ROW_TILE), ROW_TILE)], sem)

    def issue(r, c):
        copy(r, 0).start(priority=0)
        copy(r, 1).start(priority=1)
        return c

    lax.fori_loop(0, TM, issue, 0, unroll=DMA_UNROLL)
    for k in range(2):
        pltpu.make_async_copy(y_ref.at[pl.ds(0, TM * ROW_TILE)], buf_ref.at[k], sem).wait()
    info = info_ref[...]
    out = (h_ref[...] + info[:, 4:5] * _load_row_tiles(buf_ref.at[0], TM)
           + info[:, 5:6] * _load_row_tiles(buf_ref.at[1], TM))
    out_ref[...] = _rms(out, g_ref[...]) if final else out


def _combine(dest, h, info, y, final_g=None, n_rows=None):
    final = final_g is not None
    n_rows = n_rows if final else h.shape[0]
    g = final_g if final else jnp.ones((1, D_MODEL), F32)
    rows = lambda i: (i, 0)
    return pl.pallas_call(
        functools.partial(_combine_kernel, final=final),
        grid=(n_rows // TM,),
        in_specs=[pl.BlockSpec((1, 1, 2 * TM), lambda i: (i, 0, 0), memory_space=pltpu.SMEM),
                  pl.BlockSpec((TM, D_MODEL), rows), pl.BlockSpec((TM, LANES), rows),
                  pl.BlockSpec((1, D_MODEL), lambda i: (0, 0)), pl.BlockSpec(memory_space=pl.ANY)],
        out_specs=pl.BlockSpec((TM, D_MODEL), rows),
        out_shape=jax.ShapeDtypeStruct((n_rows, D_MODEL), F32),
        scratch_shapes=[pltpu.VMEM((2, TM * ROW_TILE, LANES), F32), pltpu.SemaphoreType.DMA(())],
        compiler_params=_cparams(("arbitrary",)),
        name="combine",
    )(dest, h, info, g, y)


def _moe_ffn(h, g, wr, wg, wu, wd, final_g=None, n_rows=None):
    R = h.shape[0]
    hn, info, counts = _route(h, g, wr)
    cnt = counts[0, :N_EXPERTS].astype(jnp.int32)
    padded = ((cnt + TME - 1) // TME) * TME
    ends = jnp.cumsum(padded)
    starts = ends - padded
    mt = (2 * R) // TME + N_EXPERTS
    tile_row = jnp.arange(mt, dtype=jnp.int32) * TME
    tile_expert = jnp.minimum(jnp.searchsorted(ends, tile_row, side="right"), N_EXPERTS - 1).astype(jnp.int32)
    tile_active = (tile_row < ends[-1]).astype(jnp.int32)
    e12 = info[:, 0:2].astype(jnp.int32)
    dest = ((starts[e12] + info[:, 2:4].astype(jnp.int32)) * ROW_TILE).reshape(R // TM, 1, 2 * TM)
    last_tiles = jnp.stack([jnp.maximum(ends // TME - 1, 0), (padded > 0).astype(jnp.int32),
                            jnp.broadcast_to(ends[-1] // TME, (N_EXPERTS,))]).astype(jnp.int32)
    xs = _scatter_rows(dest, last_tiles, hn, mt * TME)
    y = _experts(tile_expert, tile_active, xs, wg, wu, wd)
    return _combine(dest, h, info, y, final_g, n_rows)


def _final_kernel(h_ref, g_ref, out_ref):
    out_ref[...] = _rms(h_ref[...], g_ref[...])


def _final_norm(h, g, n_rows):
    rows = lambda i: (i, 0)
    return pl.pallas_call(
        _final_kernel,
        grid=(n_rows // TM,),
        in_specs=[pl.BlockSpec((TM, D_MODEL), rows), pl.BlockSpec((1, D_MODEL), lambda i: (0, 0))],
        out_specs=pl.BlockSpec((TM, D_MODEL), rows),
        out_shape=jax.ShapeDtypeStruct((n_rows, D_MODEL), F32),
        compiler_params=_cparams(("arbitrary",)),
        name="final_norm",
    )(h, g)


def _forget_column_placement():
    row = jnp.arange(3 * LANES)[:, None]
    col = jnp.arange(LANES)[None, :]
    term, head = row // LANES, row % LANES
    valid = head < FOX_HEADS
    plus = valid & (col == 6 * head + term)
    minus = valid & (col == 6 * head + 3 + term)
    return (plus.astype(F32) - minus.astype(F32)).astype(BF16)


def _rope_table(S):
    rd = DIFF_DH // 4
    inv = ROPE_THETA ** (-jnp.arange(0, rd, 2, dtype=F32) / rd)
    pos = jnp.concatenate([jnp.arange(N_META, N_META + S, dtype=F32), jnp.arange(TM, dtype=F32)])
    ang = pos[:, None] * inv[None, :]
    cos, sin = jnp.cos(ang), jnp.sin(ang)
    n = pos.shape[0]
    cos_t = jnp.tile(jnp.concatenate([cos, cos, jnp.ones((n, 48), F32)], axis=1), (1, 2))
    sin_lo = jnp.tile(jnp.concatenate([-sin, jnp.zeros((n, 56), F32)], axis=1), (1, 2))
    sin_hi = jnp.tile(jnp.concatenate([jnp.zeros((n, 8), F32), sin, jnp.zeros((n, 48), F32)], axis=1), (1, 2))
    return jnp.concatenate([cos_t, sin_lo, sin_hi], axis=1)


def kernel(x, meta_tokens, norm_mix_g, w_in, b_forget, diff_lambda, diff_subln_g, w_branch_diff, w_branch_fox,
           w_out, norm_ffn_g, ffn_w_gate, ffn_w_up, ffn_w_down, moe_router, moe_w_gate, moe_w_up, moe_w_down,
           final_norm_g):
    B, S, D = x.shape
    depth = w_in.shape[0]
    assert D == D_MODEL and S % TQ == 0 and meta_tokens.shape[0] == N_META
    nq = S // TM
    h = jnp.concatenate([x.reshape(B * S, D), meta_tokens.astype(x.dtype),
                         jnp.zeros((TQ - N_META, D), x.dtype)], axis=0)
    rope = _rope_table(S)
    g_final = final_norm_g.astype(F32).reshape(1, D)
    e6 = _forget_column_placement()
    scale = DIFF_DH ** -0.5 * LOG2E
    for layer in range(depth):
        lam_init = 0.8 - 0.6 * math.exp(-0.3 * layer)
        w = w_in[layer]
        dq, dk, dv, fq, fk, fv, ff, ga, gb = jnp.split(
            w, [HALF, 2 * HALF, 3 * HALF, 4 * HALF, 5 * HALF, 6 * HALF, 6 * HALF + FOX_HEADS,
                6 * HALF + FOX_HEADS + D_MODEL], axis=1)
        w1 = jnp.concatenate([jnp.pad(ff, ((0, 0), (0, LANES - FOX_HEADS))), dq * scale, dk], axis=1).astype(BF16)
        w2 = jnp.concatenate([dv, fv], axis=1).T.astype(BF16)
        w3 = jnp.concatenate([fq * scale, fk], axis=1).astype(BF16)
        bfp = jnp.pad(b_forget[layer].astype(F32), (0, LANES - FOX_HEADS)).reshape(1, LANES)
        g_mix = norm_mix_g[layer].astype(F32).reshape(1, D)
        dq_a, dk_a, vt_a, fq_a, fk_a = _inproj(h, g_mix, w1, w2, w3, e6, bfp, rope, nq=nq)

        lp = diff_lambda[layer].astype(F32)
        lam = jnp.exp(jnp.sum(lp[0] * lp[1])) - jnp.exp(jnp.sum(lp[2] * lp[3])) + lam_init
        par = jnp.zeros((8, LANES), F32)
        par = par.at[1].set(diff_subln_g[layer].astype(F32)).at[2].set(1.0 - lam_init).at[3].set(lam)
        o_a = _attention(dq_a, dk_a, vt_a, par, diff=True, nq=nq, nb=B)
        o_b = _attention(fq_a, fk_a, vt_a, None, diff=False, nq=nq, nb=B)

        wgate = jnp.concatenate([ga, gb], axis=1).astype(BF16)
        h = _mixout(h, o_a, o_b, g_mix, wgate, w_branch_diff[layer].astype(BF16),
                    w_branch_fox[layer].astype(BF16), w_out[layer].astype(BF16))

        g_ffn = norm_ffn_g[layer].astype(F32).reshape(1, D)
        jj = layer // 2
        if layer % 2 == 0:
            h = _dense_ffn(h, g_ffn, ffn_w_gate[jj].astype(BF16), ffn_w_up[jj].astype(BF16),
                           ffn_w_down[jj].astype(BF16))
        else:
            r_hi, r_mid, _ = _split3(jnp.pad(moe_router[jj].astype(F32), ((0, 0), (0, LANES - N_EXPERTS))))
            wr = jnp.stack([r_hi, r_mid]).astype(BF16)
            last = layer == depth - 1
            h = _moe_ffn(h, g_ffn, wr, moe_w_gate[jj].astype(BF16), moe_w_up[jj].astype(BF16),
                         moe_w_down[jj].astype(BF16), g_final if last else None, B * S)
    out = h if depth % 2 == 0 else _final_norm(h, g_final, B * S)
    return out.reshape(B, S, D)
```

```python
import functools
import math

import jax
import jax.numpy as jnp
from jax import lax
from jax.experimental import pallas as pl
from jax.experimental.pallas import tpu as pltpu

D_MODEL = 1024
N_META = 16
ROPE_THETA = 500000.0
RMS_EPS = 1e-6
NEG_INF = -1e30

DIFF_HEADS = 4
DIFF_DH = 64
FOX_HEADS = 8
FOX_DH = 64
HALF = 512
N_EXPERTS = 8
LANES = 128

TM = 512
KEY_TILES_PER_Q = 2
TQ = KEY_TILES_PER_Q * TM
TME = 512
TF_MOE = 1792
FF_CHUNK = 256
ROW_TILE = 8
DMA_UNROLL = 8
LOG2E = 1.4426950408889634
ONES_ROWS = 16
VMEM_LIMIT = 56 * 1024 * 1024

F32 = jnp.float32
BF16 = jnp.bfloat16


def _cparams(sem):
    return pltpu.CompilerParams(dimension_semantics=sem, vmem_limit_bytes=VMEM_LIMIT)


def _rms(x, g):
    ms = jnp.mean(x * x, axis=-1, keepdims=True)
    return x * lax.rsqrt(ms + RMS_EPS) * g


def _split3(x):
    hi = x.astype(BF16).astype(F32)
    r = x - hi
    mid = r.astype(BF16).astype(F32)
    lo = (r - mid).astype(BF16).astype(F32)
    return hi, mid, lo


def _tri_cumsum(x, inclusive):
    n = x.shape[0]
    row = lax.broadcasted_iota(jnp.int32, (n, n), 0)
    col = lax.broadcasted_iota(jnp.int32, (n, n), 1)
    tri = jnp.where((col <= row) if inclusive else (col < row), 1.0, 0.0).astype(BF16)
    parts = jnp.concatenate([p.astype(BF16) for p in _split3(x)], axis=1)
    out = jnp.dot(tri, parts, preferred_element_type=F32)
    return out[:, :LANES] + out[:, LANES:2 * LANES] + out[:, 2 * LANES:]


def _store_row_tiles(ref, x):
    rows = x.shape[0]
    for s in range(ROW_TILE):
        ref[pl.ds(s, rows, stride=ROW_TILE), :] = x[:, LANES * s:LANES * (s + 1)]


def _load_row_tiles(ref, rows):
    return jnp.concatenate([ref[pl.ds(s, rows, stride=ROW_TILE), :] for s in range(ROW_TILE)], axis=1)


def _inproj_kernel(h_ref, g_ref, w1_ref, w2_ref, w3_ref, e6_ref, bf_ref, rope_ref,
                   dq_ref, dk_ref, vt_ref, fq_ref, fk_ref, carry_ref, mcarry_ref, *, nq):
    i = pl.program_id(0)

    @pl.when(i == 0)
    def _():
        carry_ref[...] = jnp.zeros_like(carry_ref)
        mcarry_ref[...] = jnp.zeros_like(mcarry_ref)

    hb = _rms(h_ref[...], g_ref[...]).astype(BF16)

    z1 = jnp.dot(hb, w1_ref[...], preferred_element_type=F32)

    z4 = z1[:, :LANES] + bf_ref[...]
    lane = lax.broadcasted_iota(jnp.int32, z4.shape, 1)
    logf = jnp.minimum(z4, 0.0) - jnp.log1p(jnp.exp(-jnp.abs(z4)))
    logf = jnp.where(lane < FOX_HEADS, logf, 0.0)
    j_in_batch = lax.rem(jnp.maximum(i - 1, 0), nq)
    base = jnp.where(i == 0, 0.0, jnp.where(j_in_batch == 0, mcarry_ref[...], carry_ref[...]))
    c = _tri_cumsum(logf, inclusive=True) + base

    vt_ref[0] = lax.dot_general(w2_ref[...], hb, (((1,), (1,)), ((), ())),
                                preferred_element_type=F32).astype(BF16)

    parts = jnp.concatenate([p.astype(BF16) for p in _split3(c * LOG2E)], axis=1)
    c6 = jnp.dot(parts, e6_ref[...], preferred_element_type=F32)
    z3 = jnp.dot(hb, w3_ref[...], preferred_element_type=F32)

    cos_t = rope_ref[:, 0:LANES]
    sin_lo = rope_ref[:, LANES:2 * LANES]
    sin_hi = rope_ref[:, 2 * LANES:3 * LANES]
    for j in range(8):
        zj = z1[:, LANES * (j + 1):LANES * (j + 2)]
        rot = zj * cos_t + pltpu.roll(zj, LANES - 8, 1) * sin_lo + pltpu.roll(zj, 8, 1) * sin_hi
        dst = dq_ref if j < 4 else dk_ref
        dst[:, LANES * (j % 4):LANES * (j % 4 + 1)] = rot.astype(BF16)

    @pl.when(i == 0)
    def _():
        mcarry_ref[...] = c[N_META - 1:N_META, :]

    carry_ref[...] = c[TM - 1:TM, :]

    for hd in range(FOX_HEADS):
        off = 64 if hd % 2 == 0 else 0
        moved = pltpu.roll(c6, (off - 6 * hd) % LANES, 1)
        first3 = (lane >= off) & (lane < off + 3)
        last3 = (lane >= off + 3) & (lane < off + 6)
        aug_q = jnp.where(first3, moved, jnp.where(last3, 1.0, 0.0))
        aug_k = jnp.where(last3, moved, jnp.where(first3, 1.0, 0.0))
        slab = hd // 2
        keep = (lane < 64) if hd % 2 == 0 else (lane >= 64)
        zq = z3[:, LANES * slab:LANES * (slab + 1)]
        zk = z3[:, HALF + LANES * slab:HALF + LANES * (slab + 1)]
        fq_ref[:, LANES * hd:LANES * (hd + 1)] = jnp.where(keep, zq, aug_q).astype(BF16)
        fk_ref[:, LANES * hd:LANES * (hd + 1)] = jnp.where(keep, zk, aug_k).astype(BF16)


def _inproj(h, g, w1, w2, w3, e6, bfp, rope, *, nq):
    R = h.shape[0]
    nt = R // TM
    n_real = nt - TQ // TM
    rows = lambda i: (jnp.where(i == 0, n_real, jnp.where(i <= n_real, i - 1, i)), 0)
    rope_rows = lambda i: (jnp.where(i == 0, nq, lax.rem(jnp.maximum(i - 1, 0), nq)), 0)
    const = lambda i: (0, 0)
    out_sd = lambda w: jax.ShapeDtypeStruct((R, w), BF16)
    return pl.pallas_call(
        functools.partial(_inproj_kernel, nq=nq),
        grid=(nt,),
        in_specs=[
            pl.BlockSpec((TM, D_MODEL), rows),
            pl.BlockSpec((1, D_MODEL), const),
            pl.BlockSpec((D_MODEL, LANES + 2 * HALF), const),
            pl.BlockSpec((2 * HALF, D_MODEL), const),
            pl.BlockSpec((D_MODEL, 2 * HALF), const),
            pl.BlockSpec((3 * LANES, LANES), const),
            pl.BlockSpec((1, LANES), const),
            pl.BlockSpec((TM, 3 * LANES), rope_rows),
        ],
        out_specs=[
            pl.BlockSpec((TM, HALF), rows), pl.BlockSpec((TM, HALF), rows),
            pl.BlockSpec((1, 2 * HALF, TM), lambda i: (rows(i)[0], 0, 0)),
            pl.BlockSpec((TM, 2 * HALF), rows), pl.BlockSpec((TM, 2 * HALF), rows),
        ],
        out_shape=[out_sd(HALF), out_sd(HALF), jax.ShapeDtypeStruct((nt, 2 * HALF, TM), BF16),
                   out_sd(2 * HALF), out_sd(2 * HALF)],
        scratch_shapes=[pltpu.VMEM((1, LANES), F32), pltpu.VMEM((1, LANES), F32)],
        compiler_params=_cparams(("arbitrary",)),
        name="inproj",
    )(h, g, w1, w2, w3, e6, bfp, rope)


def _attn_kernel(*refs, diff, nq, nb):
    if diff:
        q_ref, k_ref, vt_ref, km_ref, vtm_ref, par_ref, o_ref, acc_ref, m_ref, q_scr, sa_ref, sb_ref = refs
    else:
        q_ref, k_ref, vt_ref, km_ref, vtm_ref, o_ref, acc_ref, m_ref, q_scr, sa_ref, sb_ref = refs
    t = pl.program_id(1)
    nqt = nq // KEY_TILES_PER_Q
    is_real = t < nb * nqt
    jq = lax.rem(t, nqt)
    dv = acc_ref.shape[1] - ONES_ROWS

    for sub in range(2):
        if diff:
            q = q_ref[...]
            lane = lax.broadcasted_iota(jnp.int32, q.shape, 1)
            q_scr[sub] = jnp.where((lane < 64) if sub == 0 else (lane >= 64), q, jnp.zeros_like(q))
        else:
            q_scr[sub] = q_ref[:, LANES * sub:LANES * (sub + 1)]
        m_ref[sub] = jnp.full(m_ref.shape[1:], NEG_INF, F32)
        acc_ref[sub] = jnp.zeros(acc_ref.shape[1:], F32)

    def keys_of(sub, k_tile):
        return k_tile if diff else k_tile[:, LANES * sub:LANES * (sub + 1)]

    def values_of(sub, vt_tile):
        ones = jnp.ones((ONES_ROWS, vt_tile.shape[1]), BF16)
        vt = vt_tile if diff else vt_tile[dv * sub:dv * (sub + 1), :]
        return jnp.concatenate([vt, ones], axis=0)

    def scores(sub, k, q0):
        return lax.dot_general(k, q_scr[sub, q0:, :], (((1,), (1,)), ((), ())), preferred_element_type=F32)

    def update(sub, st, vt, mask, q0):
        if mask is not None:
            st = jnp.where(mask, st, NEG_INF)
        m_prev = m_ref[sub, :, q0:]
        m_new = jnp.maximum(m_prev, jnp.max(st, axis=0, keepdims=True))
        p = jnp.exp2(st - m_new).astype(BF16)
        if p.shape[0] < vt.shape[1]:
            p = jnp.concatenate([p, jnp.zeros((vt.shape[1] - p.shape[0], p.shape[1]), BF16)], axis=0)
        acc_ref[sub, :, q0:] = (jnp.exp2(m_prev - m_new) * acc_ref[sub, :, q0:]
                                + jnp.dot(vt, p, preferred_element_type=F32))
        m_ref[sub, :, q0:] = m_new

    def scores_into(buf, tile, q0):
        k_tile = k_ref[pl.ds(pl.multiple_of(tile * TM, TM), TM), :]
        for sub in range(2):
            buf[sub, :, q0:] = scores(sub, keys_of(sub, k_tile), q0)

    def update_from(buf, tile, causal, q0):
        vt_tile = vt_ref[tile]
        mask = None
        if causal:
            key = lax.broadcasted_iota(jnp.int32, (TM, TQ - q0), 0)
            qry = lax.broadcasted_iota(jnp.int32, (TM, TQ - q0), 1)
            mask = key <= qry
        for sub in range(2):
            update(sub, buf[sub, :, q0:], values_of(sub, vt_tile), mask, q0)

    key = lax.broadcasted_iota(jnp.int32, (N_META, TQ), 0)
    qry = lax.broadcasted_iota(jnp.int32, (N_META, TQ), 1)
    meta_mask = key <= jnp.where(is_real, N_META - 1, qry)
    km_tile = km_ref[...]
    vtm_tile = vtm_ref[0][:, :LANES]
    meta_scores = [scores(sub, keys_of(sub, km_tile), 0) for sub in range(2)]

    scores_into(sa_ref, 0, 0)
    for sub in range(2):
        update(sub, meta_scores[sub], values_of(sub, vtm_tile), meta_mask, 0)

    def pair(i, carry):
        scores_into(sb_ref, 2 * i + 1, 0)
        update_from(sa_ref, 2 * i, False, 0)
        scores_into(sa_ref, 2 * i + 2, 0)
        update_from(sb_ref, 2 * i + 1, False, 0)
        return carry

    lax.fori_loop(0, jnp.where(is_real, jq, 0), pair, 0)

    @pl.when(is_real)
    def _():
        scores_into(sb_ref, 2 * jq + 1, TM)
        update_from(sa_ref, 2 * jq, True, 0)
        update_from(sb_ref, 2 * jq + 1, True, TM)

    a0 = acc_ref[0]
    a1 = acc_ref[1]
    o0 = a0[:dv] / a0[dv:dv + 1]
    o1 = a1[:dv] / a1[dv:dv + 1]
    if diff:
        d = (o0 - par_ref[3:4, 0:1] * o1).T
        o_ref[...] = (_rms(d, par_ref[1:2, :]) * par_ref[2:3, :]).astype(BF16)
    else:
        o_ref[...] = jnp.concatenate([o0, o1], axis=0).T.astype(BF16)


def _attention(q, k, vt, par, *, diff, nq, nb):
    R = q.shape[0]
    S = nq * TM
    ntq = R // TQ
    qw = LANES if diff else 2 * LANES
    voff = 0 if diff else HALF // LANES
    meta_blk = (nb * S) // N_META
    batch_of = lambda t: jnp.minimum(t // (nq // KEY_TILES_PER_Q), nb - 1)
    in_specs = [
        pl.BlockSpec((TQ, qw), lambda p, t: (t, p)),
        pl.BlockSpec((S, qw), lambda p, t: (batch_of(t), p)),
        pl.BlockSpec((nq, LANES, TM), lambda p, t: (batch_of(t), p + voff, 0)),
        pl.BlockSpec((N_META, qw), lambda p, t: (meta_blk, p)),
        pl.BlockSpec((1, LANES, TM), lambda p, t: (nb * nq, p + voff, 0)),
    ]
    args = [q, k, vt, k, vt]
    if diff:
        in_specs.append(pl.BlockSpec((8, LANES), lambda p, t: (0, 0)))
        args.append(par)
    acc_rows = (LANES if diff else LANES // 2) + ONES_ROWS
    return pl.pallas_call(
        functools.partial(_attn_kernel, diff=diff, nq=nq, nb=nb),
        grid=(4, ntq),
        in_specs=in_specs,
        out_specs=pl.BlockSpec((TQ, LANES), lambda p, t: (t, p)),
        out_shape=jax.ShapeDtypeStruct((R, HALF), BF16),
        scratch_shapes=[pltpu.VMEM((2, acc_rows, TQ), F32), pltpu.VMEM((2, 1, TQ), F32),
                        pltpu.VMEM((2, TQ, LANES), BF16),
                        pltpu.VMEM((2, TM, TQ), F32), pltpu.VMEM((2, TM, TQ), F32)],
        compiler_params=_cparams(("arbitrary", "arbitrary")),
        name="diff_attn" if diff else "fox_attn",
    )(*args)


def _mixout_kernel(h_ref, oa_ref, ob_ref, g_ref, wg_ref, wbd_ref, wbf_ref, wo_ref, out_ref):
    x = h_ref[...]
    hb = _rms(x, g_ref[...]).astype(BF16)
    gates = jax.nn.sigmoid(jnp.dot(hb, wg_ref[...], preferred_element_type=F32))
    a = jnp.dot(oa_ref[...], wbd_ref[...], preferred_element_type=F32)
    b = jnp.dot(ob_ref[...], wbf_ref[...], preferred_element_type=F32)
    merged = gates[:, :D_MODEL] * a + gates[:, D_MODEL:] * b
    out_ref[...] = x + jnp.dot(merged.astype(BF16), wo_ref[...], preferred_element_type=F32)


def _mixout(h, oa, ob, g, wg, wbd, wbf, wo):
    R = h.shape[0]
    rows = lambda i: (i, 0)
    const = lambda i: (0, 0)
    return pl.pallas_call(
        _mixout_kernel,
        grid=(R // TM,),
        in_specs=[
            pl.BlockSpec((TM, D_MODEL), rows), pl.BlockSpec((TM, HALF), rows), pl.BlockSpec((TM, HALF), rows),
            pl.BlockSpec((1, D_MODEL), const), pl.BlockSpec((D_MODEL, 2 * D_MODEL), const),
            pl.BlockSpec((HALF, D_MODEL), const), pl.BlockSpec((HALF, D_MODEL), const),
            pl.BlockSpec((D_MODEL, D_MODEL), const),
        ],
        out_specs=pl.BlockSpec((TM, D_MODEL), rows),
        out_shape=jax.ShapeDtypeStruct((R, D_MODEL), F32),
        compiler_params=_cparams(("arbitrary",)),
        name="mixout",
    )(h, oa, ob, g, wg, wbd, wbf, wo)


def _swiglu_acc(xb, wg_ref, wu_ref, wd_ref, acc):
    def gate_up(c):
        sl = slice(FF_CHUNK * c, FF_CHUNK * (c + 1))
        return (jnp.dot(xb, wg_ref[:, sl], preferred_element_type=F32),
                jnp.dot(xb, wu_ref[:, sl], preferred_element_type=F32))

    nf = wg_ref.shape[1] // FF_CHUNK
    nxt = gate_up(0)
    for c in range(nf):
        gate, up = nxt
        if c + 1 < nf:
            nxt = gate_up(c + 1)
        mid = (gate * jax.nn.sigmoid(gate) * up).astype(BF16)
        acc = acc + jnp.dot(mid, wd_ref[FF_CHUNK * c:FF_CHUNK * (c + 1), :], preferred_element_type=F32)
    return acc


def _dense_ffn_kernel(h_ref, g_ref, wg_ref, wu_ref, wd_ref, out_ref):
    x = h_ref[...]
    hb = _rms(x, g_ref[...]).astype(BF16)
    out_ref[...] = _swiglu_acc(hb, wg_ref, wu_ref, wd_ref, x)


def _dense_ffn(h, g, wg, wu, wd):
    R = h.shape[0]
    dff = wg.shape[1]
    rows = lambda i: (i, 0)
    const = lambda i: (0, 0)
    return pl.pallas_call(
        _dense_ffn_kernel,
        grid=(R // TM,),
        in_specs=[
            pl.BlockSpec((TM, D_MODEL), rows), pl.BlockSpec((1, D_MODEL), const),
            pl.BlockSpec((D_MODEL, dff), const), pl.BlockSpec((D_MODEL, dff), const),
            pl.BlockSpec((dff, D_MODEL), const),
        ],
        out_specs=pl.BlockSpec((TM, D_MODEL), rows),
        out_shape=jax.ShapeDtypeStruct((R, D_MODEL), F32),
        compiler_params=_cparams(("arbitrary",)),
        name="dense_ffn",
    )(h, g, wg, wu, wd)


def _route_kernel(h_ref, g_ref, wr_ref, hn_ref, info_ref, cnt_ref, carry_ref):
    i = pl.program_id(0)

    @pl.when(i == 0)
    def _():
        carry_ref[...] = jnp.zeros_like(carry_ref)

    hn = _rms(h_ref[...], g_ref[...])
    _store_row_tiles(hn_ref, hn)
    h_hi, h_mid, _ = _split3(hn)
    logits = (jnp.dot(h_hi.astype(BF16), wr_ref[0], preferred_element_type=F32)
              + jnp.dot(h_mid.astype(BF16), wr_ref[0], preferred_element_type=F32)
              + jnp.dot(h_hi.astype(BF16), wr_ref[1], preferred_element_type=F32))
    lane = lax.broadcasted_iota(jnp.int32, logits.shape, 1)
    logits = jnp.where(lane < N_EXPERTS, logits, -jnp.inf)
    v1 = jnp.max(logits, axis=-1, keepdims=True)
    e1 = jnp.min(jnp.where(logits == v1, lane, LANES), axis=-1, keepdims=True)
    rest = jnp.where(lane == e1, -jnp.inf, logits)
    v2 = jnp.max(rest, axis=-1, keepdims=True)
    e2 = jnp.min(jnp.where(rest == v2, lane, LANES), axis=-1, keepdims=True)
    ex = jnp.exp(v2 - v1)
    w1 = 1.0 / (1.0 + ex)
    w2 = ex / (1.0 + ex)
    hot1 = jnp.where(lane == e1, 1.0, 0.0)
    hot2 = jnp.where(lane == e2, 1.0, 0.0)
    hot = hot1 + hot2
    before = _tri_cumsum(hot, inclusive=False) + carry_ref[...]
    r1 = jnp.sum(before * hot1, axis=-1, keepdims=True)
    r2 = jnp.sum(before * hot2, axis=-1, keepdims=True)
    total = before[TM - 1:TM, :] + hot[TM - 1:TM, :]
    carry_ref[...] = total
    cnt_ref[...] = jnp.broadcast_to(total, cnt_ref.shape)
    info_ref[...] = jnp.where(lane == 0, e1.astype(F32), jnp.where(lane == 1, e2.astype(F32),
                              jnp.where(lane == 2, r1, jnp.where(lane == 3, r2,
                                        jnp.where(lane == 4, w1, jnp.where(lane == 5, w2, 0.0))))))


def _route(h, g, wr):
    R = h.shape[0]
    rows = lambda i: (i, 0)
    return pl.pallas_call(
        _route_kernel,
        grid=(R // TM,),
        in_specs=[pl.BlockSpec((TM, D_MODEL), rows), pl.BlockSpec((1, D_MODEL), lambda i: (0, 0)),
                  pl.BlockSpec((2, D_MODEL, LANES), lambda i: (0, 0, 0))],
        out_specs=[pl.BlockSpec((TM * ROW_TILE, LANES), rows), pl.BlockSpec((TM, LANES), rows),
                   pl.BlockSpec((8, LANES), lambda i: (0, 0))],
        out_shape=[jax.ShapeDtypeStruct((R * ROW_TILE, LANES), F32), jax.ShapeDtypeStruct((R, LANES), F32),
                   jax.ShapeDtypeStruct((8, LANES), F32)],
        scratch_shapes=[pltpu.VMEM((1, LANES), F32)],
        compiler_params=_cparams(("arbitrary",)),
        name="route",
    )(h, g, wr)


def _scatter_kernel(dest_ref, last_ref, src_ref, out_ref, zero_ref, sem):
    @pl.when(pl.program_id(0) == 0)
    def _():
        zero_ref[...] = jnp.zeros_like(zero_ref)

        def zero_copy(e):
            row = pl.multiple_of(last_ref[0, e] * (TME * ROW_TILE), TME * ROW_TILE)
            return pltpu.make_async_copy(zero_ref, out_ref.at[pl.ds(row, TME * ROW_TILE)], sem)

        for e in range(N_EXPERTS):
            @pl.when(last_ref[1, e] > 0)
            def _(e=e):
                zero_copy(e).start()
        for e in range(N_EXPERTS):
            @pl.when(last_ref[1, e] > 0)
            def _(e=e):
                zero_copy(e).wait()

        def spare_copy(tile):
            row = pl.multiple_of(tile * (TME * ROW_TILE), TME * ROW_TILE)
            return pltpu.make_async_copy(zero_ref, out_ref.at[pl.ds(row, TME * ROW_TILE)], sem)

        def start_spare(tile, c):
            spare_copy(tile).start()
            return c

        def wait_spare(tile, c):
            spare_copy(tile).wait()
            return c

        n_tiles = out_ref.shape[0] // (TME * ROW_TILE)
        lax.fori_loop(last_ref[2, 0], n_tiles, start_spare, 0)
        lax.fori_loop(last_ref[2, 0], n_tiles, wait_spare, 0)

    def copy(r, k):
        return pltpu.make_async_copy(
            src_ref.at[pl.ds(pl.multiple_of(r * ROW_TILE, ROW_TILE), ROW_TILE)],
            out_ref.at[pl.ds(pl.multiple_of(dest_ref[0, 0, 2 * r + k], ROW_TILE), ROW_TILE)], sem)

    def issue(r, c):
        copy(r, 0).start(priority=0)
        copy(r, 1).start(priority=1)
        return c

    lax.fori_loop(0, TM, issue, 0, unroll=DMA_UNROLL)
    for _ in range(2):
        pltpu.make_async_copy(src_ref, out_ref.at[pl.ds(0, TM * ROW_TILE)], sem).wait()


def _scatter_rows(dest, last_tiles, src, n_rows):
    R = src.shape[0] // ROW_TILE
    return pl.pallas_call(
        _scatter_kernel,
        grid=(R // TM,),
        in_specs=[pl.BlockSpec((1, 1, 2 * TM), lambda i: (i, 0, 0), memory_space=pltpu.SMEM),
                  pl.BlockSpec(memory_space=pltpu.SMEM),
                  pl.BlockSpec((TM * ROW_TILE, LANES), lambda i: (i, 0))],
        out_specs=pl.BlockSpec(memory_space=pl.ANY),
        out_shape=jax.ShapeDtypeStruct((n_rows * ROW_TILE, LANES), src.dtype),
        scratch_shapes=[pltpu.VMEM((TME * ROW_TILE, LANES), F32), pltpu.SemaphoreType.DMA(())],
        compiler_params=pltpu.CompilerParams(dimension_semantics=("arbitrary",), vmem_limit_bytes=VMEM_LIMIT,
                                             has_side_effects=True),
        name="scatter_rows",
    )(dest, last_tiles, src)


def _expert_kernel(te_ref, act_ref, x_ref, wg_ref, wu_ref, wd_ref, y_ref, acc_ref):
    i = pl.program_id(0)
    f = pl.program_id(1)
    active = act_ref[i] > 0
    del te_ref

    def step(first):
        xb = _load_row_tiles(x_ref, TME).astype(BF16)
        prev = jnp.zeros(acc_ref.shape, F32) if first else acc_ref[...]
        acc = _swiglu_acc(xb, wg_ref, wu_ref, wd_ref, prev)
        acc_ref[...] = acc
        _store_row_tiles(y_ref, acc)

    @pl.when(active & (f == 0))
    def _():
        step(True)

    @pl.when(active & (f > 0))
    def _():
        step(False)

    @pl.when(jnp.logical_not(active))
    def _():
        y_ref[...] = jnp.zeros_like(y_ref)


def _experts(tile_expert, tile_active, xs, wg, wu, wd):
    mt = tile_expert.shape[0]
    dffe = wg.shape[2]
    grid_spec = pltpu.PrefetchScalarGridSpec(
        num_scalar_prefetch=2,
        grid=(mt, dffe // TF_MOE),
        in_specs=[
            pl.BlockSpec((TME * ROW_TILE, LANES), lambda i, f, te, act: (i, 0)),
            pl.BlockSpec((None, D_MODEL, TF_MOE), lambda i, f, te, act: (te[i], 0, f)),
            pl.BlockSpec((None, D_MODEL, TF_MOE), lambda i, f, te, act: (te[i], 0, f)),
            pl.BlockSpec((None, TF_MOE, D_MODEL), lambda i, f, te, act: (te[i], f, 0)),
        ],
        out_specs=pl.BlockSpec((TME * ROW_TILE, LANES), lambda i, f, te, act: (i, 0)),
        scratch_shapes=[pltpu.VMEM((TME, D_MODEL), F32)],
    )
    return pl.pallas_call(
        _expert_kernel,
        grid_spec=grid_spec,
        out_shape=jax.ShapeDtypeStruct((mt * TME * ROW_TILE, LANES), F32),
        compiler_params=_cparams(("arbitrary", "arbitrary")),
        name="experts",
    )(tile_expert, tile_active, xs, wg, wu, wd)


def _combine_kernel(dest_ref, next_ref, h_ref, info_ref, g_ref, y_ref, out_ref, buf_ref, sem, *, final):
    i = pl.program_id(0)

    def issue(idx_ref, slot):
        def copy(r, k):
            return pltpu.make_async_copy(
                y_ref.at[pl.ds(pl.multiple_of(idx_ref[0, 0, 2 * r + k], ROW_TILE), ROW_TILE)],
                buf_ref.at[slot, k, pl.ds(pl.multiple_of(r * ROW_TILE, ROW_TILE), ROW_TILE)], sem.at[slot])

        def body(r, c):
            copy(r, 0).start(priority=0)
            copy(r, 1).start(priority=1)
            return c

        lax.fori_loop(0, TM, body, 0, unroll=DMA_UNROLL)

    def finish(slot):
        for k in range(2):
            pltpu.make_async_copy(y_ref.at[pl.ds(0, TM * ROW_TILE)], buf_ref.at[slot, k], sem.at[slot]).wait()
        info = info_ref[...]
        out = (h_ref[...] + info[:, 4:5] * _load_row_tiles(buf_ref.at[slot, 0], TM)
               + info[:, 5:6] * _load_row_tiles(buf_ref.at[slot, 1], TM))
        out_ref[...] = _rms(out, g_ref[...]) if final else out

    @pl.when(i == 0)
    def _():
        issue(dest_ref, 0)

    for slot in range(2):
        @pl.when(lax.rem(i, 2) == slot)
        def _(slot=slot):
            @pl.when(i + 1 < pl.num_programs(0))
            def _():
                issue(next_ref, 1 - slot)

            finish(slot)


def _combine(dest, h, info, y, final_g=None, n_rows=None):
    final = final_g is not None
    n_rows = n_rows if final else h.shape[0]
    g = final_g if final else jnp.ones((1, D_MODEL), F32)
    rows = lambda i: (i, 0)
    nt = n_rows // TM
    return pl.pallas_call(
        functools.partial(_combine_kernel, final=final),
        grid=(nt,),
        in_specs=[pl.BlockSpec((1, 1, 2 * TM), lambda i: (i, 0, 0), memory_space=pltpu.SMEM),
                  pl.BlockSpec((1, 1, 2 * TM), lambda i: (jnp.minimum(i + 1, nt - 1), 0, 0),
                               memory_space=pltpu.SMEM),
                  pl.BlockSpec((TM, D_MODEL), rows), pl.BlockSpec((TM, LANES), rows),
                  pl.BlockSpec((1, D_MODEL), lambda i: (0, 0)), pl.BlockSpec(memory_space=pl.ANY)],
        out_specs=pl.BlockSpec((TM, D_MODEL), rows),
        out_shape=jax.ShapeDtypeStruct((n_rows, D_MODEL), F32),
        scratch_shapes=[pltpu.VMEM((2, 2, TM * ROW_TILE, LANES), F32), pltpu.SemaphoreType.DMA((2,))],
        compiler_params=_cparams(("arbitrary",)),
        name="combine",
    )(dest, dest, h, info, g, y)


def _moe_ffn(h, g, wr, wg, wu, wd, final_g=None, n_rows=None):
    R = h.shape[0]
    hn, info, counts = _route(h, g, wr)
    cnt = counts[0, :N_EXPERTS].astype(jnp.int32)
    padded = ((cnt + TME - 1) // TME) * TME
    ends = jnp.cumsum(padded)
    starts = ends - padded
    mt = (2 * R) // TME + N_EXPERTS
    tile_row = jnp.arange(mt, dtype=jnp.int32) * TME
    tile_expert = jnp.minimum(jnp.searchsorted(ends, tile_row, side="right"), N_EXPERTS - 1).astype(jnp.int32)
    tile_active = (tile_row < ends[-1]).astype(jnp.int32)
    e12 = info[:, 0:2].astype(jnp.int32)
    dest = ((starts[e12] + info[:, 2:4].astype(jnp.int32)) * ROW_TILE).reshape(R // TM, 1, 2 * TM)
    last_tiles = jnp.stack([jnp.maximum(ends // TME - 1, 0), (padded > 0).astype(jnp.int32),
                            jnp.broadcast_to(ends[-1] // TME, (N_EXPERTS,))]).astype(jnp.int32)
    xs = _scatter_rows(dest, last_tiles, hn, mt * TME)
    y = _experts(tile_expert, tile_active, xs, wg, wu, wd)
    return _combine(dest, h, info, y, final_g, n_rows)


def _final_kernel(h_ref, g_ref, out_ref):
    out_ref[...] = _rms(h_ref[...], g_ref[...])


def _final_norm(h, g, n_rows):
    rows = lambda i: (i, 0)
    return pl.pallas_call(
        _final_kernel,
        grid=(n_rows // TM,),
        in_specs=[pl.BlockSpec((TM, D_MODEL), rows), pl.BlockSpec((1, D_MODEL), lambda i: (0, 0))],
        out_specs=pl.BlockSpec((TM, D_MODEL), rows),
        out_shape=jax.ShapeDtypeStruct((n_rows, D_MODEL), F32),
        compiler_params=_cparams(("arbitrary",)),
        name="final_norm",
    )(h, g)


def _forget_column_placement():
    row = jnp.arange(3 * LANES)[:, None]
    col = jnp.arange(LANES)[None, :]
    term, head = row // LANES, row % LANES
    valid = head < FOX_HEADS
    plus = valid & (col == 6 * head + term)
    minus = valid & (col == 6 * head + 3 + term)
    return (plus.astype(F32) - minus.astype(F32)).astype(BF16)


def _rope_table(S):
    rd = DIFF_DH // 4
    inv = ROPE_THETA ** (-jnp.arange(0, rd, 2, dtype=F32) / rd)
    pos = jnp.concatenate([jnp.arange(N_META, N_META + S, dtype=F32), jnp.arange(TM, dtype=F32)])
    ang = pos[:, None] * inv[None, :]
    cos, sin = jnp.cos(ang), jnp.sin(ang)
    n = pos.shape[0]
    cos_t = jnp.tile(jnp.concatenate([cos, cos, jnp.ones((n, 48), F32)], axis=1), (1, 2))
    sin_lo = jnp.tile(jnp.concatenate([-sin, jnp.zeros((n, 56), F32)], axis=1), (1, 2))
    sin_hi = jnp.tile(jnp.concatenate([jnp.zeros((n, 8), F32), sin, jnp.zeros((n, 48), F32)], axis=1), (1, 2))
    return jnp.concatenate([cos_t, sin_lo, sin_hi], axis=1)


def kernel(x, meta_tokens, norm_mix_g, w_in, b_forget, diff_lambda, diff_subln_g, w_branch_diff, w_branch_fox,
           w_out, norm_ffn_g, ffn_w_gate, ffn_w_up, ffn_w_down, moe_router, moe_w_gate, moe_w_up, moe_w_down,
           final_norm_g):
    B, S, D = x.shape
    depth = w_in.shape[0]
    assert D == D_MODEL and S % TQ == 0 and meta_tokens.shape[0] == N_META
    nq = S // TM
    h = jnp.concatenate([x.reshape(B * S, D), meta_tokens.astype(x.dtype),
                         jnp.zeros((TQ - N_META, D), x.dtype)], axis=0)
    rope = _rope_table(S)
    g_final = final_norm_g.astype(F32).reshape(1, D)
    e6 = _forget_column_placement()
    scale = DIFF_DH ** -0.5 * LOG2E
    for layer in range(depth):
        lam_init = 0.8 - 0.6 * math.exp(-0.3 * layer)
        w = w_in[layer]
        dq, dk, dv, fq, fk, fv, ff, ga, gb = jnp.split(
            w, [HALF, 2 * HALF, 3 * HALF, 4 * HALF, 5 * HALF, 6 * HALF, 6 * HALF + FOX_HEADS,
                6 * HALF + FOX_HEADS + D_MODEL], axis=1)
        w1 = jnp.concatenate([jnp.pad(ff, ((0, 0), (0, LANES - FOX_HEADS))), dq * scale, dk], axis=1).astype(BF16)
        w2 = jnp.concatenate([dv, fv], axis=1).T.astype(BF16)
        w3 = jnp.concatenate([fq * scale, fk], axis=1).astype(BF16)
        bfp = jnp.pad(b_forget[layer].astype(F32), (0, LANES - FOX_HEADS)).reshape(1, LANES)
        g_mix = norm_mix_g[layer].astype(F32).reshape(1, D)
        dq_a, dk_a, vt_a, fq_a, fk_a = _inproj(h, g_mix, w1, w2, w3, e6, bfp, rope, nq=nq)

        lp = diff_lambda[layer].astype(F32)
        lam = jnp.exp(jnp.sum(lp[0] * lp[1])) - jnp.exp(jnp.sum(lp[2] * lp[3])) + lam_init
        par = jnp.zeros((8, LANES), F32)
        par = par.at[1].set(diff_subln_g[layer].astype(F32)).at[2].set(1.0 - lam_init).at[3].set(lam)
        o_a = _attention(dq_a, dk_a, vt_a, par, diff=True, nq=nq, nb=B)
        o_b = _attention(fq_a, fk_a, vt_a, None, diff=False, nq=nq, nb=B)

        wgate = jnp.concatenate([ga, gb], axis=1).astype(BF16)
        h = _mixout(h, o_a, o_b, g_mix, wgate, w_branch_diff[layer].astype(BF16),
                    w_branch_fox[layer].astype(BF16), w_out[layer].astype(BF16))

        g_ffn = norm_ffn_g[layer].astype(F32).reshape(1, D)
        jj = layer // 2
        if layer % 2 == 0:
            h = _dense_ffn(h, g_ffn, ffn_w_gate[jj].astype(BF16), ffn_w_up[jj].astype(BF16),
                           ffn_w_down[jj].astype(BF16))
        else:
            r_hi, r_mid, _ = _split3(jnp.pad(moe_router[jj].astype(F32), ((0, 0), (0, LANES - N_EXPERTS))))
            wr = jnp.stack([r_hi, r_mid]).astype(BF16)
            last = layer == depth - 1
            h = _moe_ffn(h, g_ffn, wr, moe_w_gate[jj].astype(BF16), moe_w_up[jj].astype(BF16),
                         moe_w_down[jj].astype(BF16), g_final if last else None, B * S)
    out = h if depth % 2 == 0 else _final_norm(h, g_final, B * S)
    return out.reshape(B, S, D)
```

```python
import functools
import math

import jax
import jax.numpy as jnp
from jax import lax
from jax.experimental import pallas as pl
from jax.experimental.pallas import tpu as pltpu

D_MODEL = 1024
N_META = 16
ROPE_THETA = 500000.0
RMS_EPS = 1e-6
NEG_INF = -1e30

DIFF_HEADS = 4
DIFF_DH = 64
FOX_HEADS = 8
FOX_DH = 64
HALF = 512
N_EXPERTS = 8
LANES = 128

TM = 512
KEY_TILES_PER_Q = 2
TQ = KEY_TILES_PER_Q * TM
TME = 512
TF_MOE = 1792
FF_CHUNK = 256
ROW_TILE = 8
DMA_UNROLL = 8
LOG2E = 1.4426950408889634
ONES_ROWS = 16
VMEM_LIMIT = 56 * 1024 * 1024

F32 = jnp.float32
BF16 = jnp.bfloat16


def _cparams(sem):
    return pltpu.CompilerParams(dimension_semantics=sem, vmem_limit_bytes=VMEM_LIMIT)


def _rms(x, g):
    ms = jnp.mean(x * x, axis=-1, keepdims=True)
    return x * lax.rsqrt(ms + RMS_EPS) * g


def _split3(x):
    hi = x.astype(BF16).astype(F32)
    r = x - hi
    mid = r.astype(BF16).astype(F32)
    lo = (r - mid).astype(BF16).astype(F32)
    return hi, mid, lo


def _tri_cumsum(x, inclusive):
    n = x.shape[0]
    row = lax.broadcasted_iota(jnp.int32, (n, n), 0)
    col = lax.broadcasted_iota(jnp.int32, (n, n), 1)
    tri = jnp.where((col <= row) if inclusive else (col < row), 1.0, 0.0).astype(BF16)
    parts = jnp.concatenate([p.astype(BF16) for p in _split3(x)], axis=1)
    out = jnp.dot(tri, parts, preferred_element_type=F32)
    return out[:, :LANES] + out[:, LANES:2 * LANES] + out[:, 2 * LANES:]


def _store_row_tiles(ref, x):
    rows = x.shape[0]
    for s in range(ROW_TILE):
        ref[pl.ds(s, rows, stride=ROW_TILE), :] = x[:, LANES * s:LANES * (s + 1)]


def _load_row_tiles(ref, rows):
    return jnp.concatenate([ref[pl.ds(s, rows, stride=ROW_TILE), :] for s in range(ROW_TILE)], axis=1)


def _inproj_kernel(h_ref, g_ref, w1_ref, w2_ref, w3_ref, e6_ref, bf_ref, rope_ref,
                   dq_ref, dk_ref, vt_ref, fq_ref, fk_ref, carry_ref, mcarry_ref, *, nq):
    i = pl.program_id(0)

    @pl.when(i == 0)
    def _():
        carry_ref[...] = jnp.zeros_like(carry_ref)
        mcarry_ref[...] = jnp.zeros_like(mcarry_ref)

    hb = _rms(h_ref[...], g_ref[...]).astype(BF16)

    z1 = jnp.dot(hb, w1_ref[...], preferred_element_type=F32)

    z4 = z1[:, :LANES] + bf_ref[...]
    lane = lax.broadcasted_iota(jnp.int32, z4.shape, 1)
    logf = jnp.minimum(z4, 0.0) - jnp.log1p(jnp.exp(-jnp.abs(z4)))
    logf = jnp.where(lane < FOX_HEADS, logf, 0.0)
    j_in_batch = lax.rem(jnp.maximum(i - 1, 0), nq)
    base = jnp.where(i == 0, 0.0, jnp.where(j_in_batch == 0, mcarry_ref[...], carry_ref[...]))
    c = _tri_cumsum(logf, inclusive=True) + base

    vt_ref[0] = lax.dot_general(w2_ref[...], hb, (((1,), (1,)), ((), ())),
                                preferred_element_type=F32).astype(BF16)

    parts = jnp.concatenate([p.astype(BF16) for p in _split3(c * LOG2E)], axis=1)
    c6 = jnp.dot(parts, e6_ref[...], preferred_element_type=F32)
    z3 = jnp.dot(hb, w3_ref[...], preferred_element_type=F32)

    cos_t = rope_ref[:, 0:LANES]
    sin_lo = rope_ref[:, LANES:2 * LANES]
    sin_hi = rope_ref[:, 2 * LANES:3 * LANES]
    for j in range(8):
        zj = z1[:, LANES * (j + 1):LANES * (j + 2)]
        rot = zj * cos_t + pltpu.roll(zj, LANES - 8, 1) * sin_lo + pltpu.roll(zj, 8, 1) * sin_hi
        dst = dq_ref if j < 4 else dk_ref
        dst[:, LANES * (j % 4):LANES * (j % 4 + 1)] = rot.astype(BF16)

    @pl.when(i == 0)
    def _():
        mcarry_ref[...] = c[N_META - 1:N_META, :]

    carry_ref[...] = c[TM - 1:TM, :]

    for hd in range(FOX_HEADS):
        off = 64 if hd % 2 == 0 else 0
        moved = pltpu.roll(c6, (off - 6 * hd) % LANES, 1)
        first3 = (lane >= off) & (lane < off + 3)
        last3 = (lane >= off + 3) & (lane < off + 6)
        aug_q = jnp.where(first3, moved, jnp.where(last3, 1.0, 0.0))
        aug_k = jnp.where(last3, moved, jnp.where(first3, 1.0, 0.0))
        slab = hd // 2
        keep = (lane < 64) if hd % 2 == 0 else (lane >= 64)
        zq = z3[:, LANES * slab:LANES * (slab + 1)]
        zk = z3[:, HALF + LANES * slab:HALF + LANES * (slab + 1)]
        fq_ref[:, LANES * hd:LANES * (hd + 1)] = jnp.where(keep, zq, aug_q).astype(BF16)
        fk_ref[:, LANES * hd:LANES * (hd + 1)] = jnp.where(keep, zk, aug_k).astype(BF16)


def _inproj(h, g, w1, w2, w3, e6, bfp, rope, *, nq):
    R = h.shape[0]
    nt = R // TM
    n_real = nt - TQ // TM
    rows = lambda i: (jnp.where(i == 0, n_real, jnp.where(i <= n_real, i - 1, i)), 0)
    rope_rows = lambda i: (jnp.where(i == 0, nq, lax.rem(jnp.maximum(i - 1, 0), nq)), 0)
    const = lambda i: (0, 0)
    out_sd = lambda w: jax.ShapeDtypeStruct((R, w), BF16)
    return pl.pallas_call(
        functools.partial(_inproj_kernel, nq=nq),
        grid=(nt,),
        in_specs=[
            pl.BlockSpec((TM, D_MODEL), rows),
            pl.BlockSpec((1, D_MODEL), const),
            pl.BlockSpec((D_MODEL, LANES + 2 * HALF), const),
            pl.BlockSpec((2 * HALF, D_MODEL), const),
            pl.BlockSpec((D_MODEL, 2 * HALF), const),
            pl.BlockSpec((3 * LANES, LANES), const),
            pl.BlockSpec((1, LANES), const),
            pl.BlockSpec((TM, 3 * LANES), rope_rows),
        ],
        out_specs=[
            pl.BlockSpec((TM, HALF), rows), pl.BlockSpec((TM, HALF), rows),
            pl.BlockSpec((1, 2 * HALF, TM), lambda i: (rows(i)[0], 0, 0)),
            pl.BlockSpec((TM, 2 * HALF), rows), pl.BlockSpec((TM, 2 * HALF), rows),
        ],
        out_shape=[out_sd(HALF), out_sd(HALF), jax.ShapeDtypeStruct((nt, 2 * HALF, TM), BF16),
                   out_sd(2 * HALF), out_sd(2 * HALF)],
        scratch_shapes=[pltpu.VMEM((1, LANES), F32), pltpu.VMEM((1, LANES), F32)],
        compiler_params=_cparams(("arbitrary",)),
        name="inproj",
    )(h, g, w1, w2, w3, e6, bfp, rope)


def _attn_kernel(*refs, diff, nq, nb):
    if diff:
        q_ref, k_ref, vt_ref, km_ref, vtm_ref, par_ref, o_ref, acc_ref, m_ref, q_scr, sa_ref, sb_ref = refs
    else:
        q_ref, k_ref, vt_ref, km_ref, vtm_ref, o_ref, acc_ref, m_ref, q_scr, sa_ref, sb_ref = refs
    t = pl.program_id(1)
    nqt = nq // KEY_TILES_PER_Q
    is_real = t < nb * nqt
    jq = lax.rem(t, nqt)
    dv = acc_ref.shape[1] - ONES_ROWS

    for sub in range(2):
        if diff:
            q = q_ref[...]
            lane = lax.broadcasted_iota(jnp.int32, q.shape, 1)
            q_scr[sub] = jnp.where((lane < 64) if sub == 0 else (lane >= 64), q, jnp.zeros_like(q))
        else:
            q_scr[sub] = q_ref[:, LANES * sub:LANES * (sub + 1)]
        m_ref[sub] = jnp.full(m_ref.shape[1:], NEG_INF, F32)
        acc_ref[sub] = jnp.zeros(acc_ref.shape[1:], F32)

    def keys_of(sub, k_tile):
        return k_tile if diff else k_tile[:, LANES * sub:LANES * (sub + 1)]

    def values_of(sub, vt_tile):
        ones = jnp.ones((ONES_ROWS, vt_tile.shape[1]), BF16)
        vt = vt_tile if diff else vt_tile[dv * sub:dv * (sub + 1), :]
        return jnp.concatenate([vt, ones], axis=0)

    def scores(sub, k, q0):
        return lax.dot_general(k, q_scr[sub, q0:, :], (((1,), (1,)), ((), ())), preferred_element_type=F32)

    def update(sub, st, vt, mask, q0):
        if mask is not None:
            st = jnp.where(mask, st, NEG_INF)
        m_prev = m_ref[sub, :, q0:]
        m_new = jnp.maximum(m_prev, jnp.max(st, axis=0, keepdims=True))
        p = jnp.exp2(st - m_new).astype(BF16)
        if p.shape[0] < vt.shape[1]:
            p = jnp.concatenate([p, jnp.zeros((vt.shape[1] - p.shape[0], p.shape[1]), BF16)], axis=0)
        acc_ref[sub, :, q0:] = (jnp.exp2(m_prev - m_new) * acc_ref[sub, :, q0:]
                                + jnp.dot(vt, p, preferred_element_type=F32))
        m_ref[sub, :, q0:] = m_new

    def scores_into(buf, tile, q0, subs=(0, 1)):
        k_tile = k_ref[pl.ds(pl.multiple_of(tile * TM, TM), TM), :]
        for sub in subs:
            buf[sub, :, q0:] = scores(sub, keys_of(sub, k_tile), q0)

    def update_from(buf, tile, causal, q0, subs=(0, 1)):
        vt_tile = vt_ref[tile]
        mask = None
        if causal:
            key = lax.broadcasted_iota(jnp.int32, (TM, TQ - q0), 0)
            qry = lax.broadcasted_iota(jnp.int32, (TM, TQ - q0), 1)
            mask = key <= qry
        for sub in subs:
            update(sub, buf[sub, :, q0:], values_of(sub, vt_tile), mask, q0)

    key = lax.broadcasted_iota(jnp.int32, (N_META, TQ), 0)
    qry = lax.broadcasted_iota(jnp.int32, (N_META, TQ), 1)
    meta_mask = key <= jnp.where(is_real, N_META - 1, qry)
    km_tile = km_ref[...]
    vtm_tile = vtm_ref[0][:, :LANES]

    for sub in range(2):
        meta_scores = scores(sub, keys_of(sub, km_tile), 0)
        scores_into(sa_ref, 0, 0, (sub,))
        update(sub, meta_scores, values_of(sub, vtm_tile), meta_mask, 0)

    def pair(i, carry):
        for sub in range(2):
            scores_into(sb_ref, 2 * i + 1, 0, (sub,))
            update_from(sa_ref, 2 * i, False, 0, (sub,))
        for sub in range(2):
            scores_into(sa_ref, 2 * i + 2, 0, (sub,))
            update_from(sb_ref, 2 * i + 1, False, 0, (sub,))
        return carry

    lax.fori_loop(0, jnp.where(is_real, jq, 0), pair, 0)

    @pl.when(is_real)
    def _():
        for sub in range(2):
            scores_into(sb_ref, 2 * jq + 1, TM, (sub,))
            update_from(sa_ref, 2 * jq, True, 0, (sub,))
        update_from(sb_ref, 2 * jq + 1, True, TM)

    a0 = acc_ref[0]
    a1 = acc_ref[1]
    o0 = a0[:dv] / a0[dv:dv + 1]
    o1 = a1[:dv] / a1[dv:dv + 1]
    if diff:
        d = (o0 - par_ref[3:4, 0:1] * o1).T
        o_ref[...] = (_rms(d, par_ref[1:2, :]) * par_ref[2:3, :]).astype(BF16)
    else:
        o_ref[...] = jnp.concatenate([o0, o1], axis=0).T.astype(BF16)


def _attention(q, k, vt, par, *, diff, nq, nb):
    R = q.shape[0]
    S = nq * TM
    ntq = R // TQ
    qw = LANES if diff else 2 * LANES
    voff = 0 if diff else HALF // LANES
    meta_blk = (nb * S) // N_META
    batch_of = lambda t: jnp.minimum(t // (nq // KEY_TILES_PER_Q), nb - 1)
    in_specs = [
        pl.BlockSpec((TQ, qw), lambda p, t: (t, p)),
        pl.BlockSpec((S, qw), lambda p, t: (batch_of(t), p)),
        pl.BlockSpec((nq, LANES, TM), lambda p, t: (batch_of(t), p + voff, 0)),
        pl.BlockSpec((N_META, qw), lambda p, t: (meta_blk, p)),
        pl.BlockSpec((1, LANES, TM), lambda p, t: (nb * nq, p + voff, 0)),
    ]
    args = [q, k, vt, k, vt]
    if diff:
        in_specs.append(pl.BlockSpec((8, LANES), lambda p, t: (0, 0)))
        args.append(par)
    acc_rows = (LANES if diff else LANES // 2) + ONES_ROWS
    return pl.pallas_call(
        functools.partial(_attn_kernel, diff=diff, nq=nq, nb=nb),
        grid=(4, ntq),
        in_specs=in_specs,
        out_specs=pl.BlockSpec((TQ, LANES), lambda p, t: (t, p)),
        out_shape=jax.ShapeDtypeStruct((R, HALF), BF16),
        scratch_shapes=[pltpu.VMEM((2, acc_rows, TQ), F32), pltpu.VMEM((2, 1, TQ), F32),
                        pltpu.VMEM((2, TQ, LANES), BF16),
                        pltpu.VMEM((2, TM, TQ), F32), pltpu.VMEM((2, TM, TQ), F32)],
        compiler_params=_cparams(("arbitrary", "arbitrary")),
        name="diff_attn" if diff else "fox_attn",
    )(*args)


def _mixout_kernel(h_ref, oa_ref, ob_ref, g_ref, wg_ref, wbd_ref, wbf_ref, wo_ref, out_ref):
    x = h_ref[...]
    hb = _rms(x, g_ref[...]).astype(BF16)
    gates = jax.nn.sigmoid(jnp.dot(hb, wg_ref[...], preferred_element_type=F32))
    a = jnp.dot(oa_ref[...], wbd_ref[...], preferred_element_type=F32)
    b = jnp.dot(ob_ref[...], wbf_ref[...], preferred_element_type=F32)
    merged = gates[:, :D_MODEL] * a + gates[:, D_MODEL:] * b
    out_ref[...] = x + jnp.dot(merged.astype(BF16), wo_ref[...], preferred_element_type=F32)


def _mixout(h, oa, ob, g, wg, wbd, wbf, wo):
    R = h.shape[0]
    rows = lambda i: (i, 0)
    const = lambda i: (0, 0)
    return pl.pallas_call(
        _mixout_kernel,
        grid=(R // TM,),
        in_specs=[
            pl.BlockSpec((TM, D_MODEL), rows), pl.BlockSpec((TM, HALF), rows), pl.BlockSpec((TM, HALF), rows),
            pl.BlockSpec((1, D_MODEL), const), pl.BlockSpec((D_MODEL, 2 * D_MODEL), const),
            pl.BlockSpec((HALF, D_MODEL), const), pl.BlockSpec((HALF, D_MODEL), const),
            pl.BlockSpec((D_MODEL, D_MODEL), const),
        ],
        out_specs=pl.BlockSpec((TM, D_MODEL), rows),
        out_shape=jax.ShapeDtypeStruct((R, D_MODEL), F32),
        compiler_params=_cparams(("arbitrary",)),
        name="mixout",
    )(h, oa, ob, g, wg, wbd, wbf, wo)


def _swiglu_acc(xb, wg_ref, wu_ref, wd_ref, acc):
    def gate_up(c):
        sl = slice(FF_CHUNK * c, FF_CHUNK * (c + 1))
        return (jnp.dot(xb, wg_ref[:, sl], preferred_element_type=F32),
                jnp.dot(xb, wu_ref[:, sl], preferred_element_type=F32))

    nf = wg_ref.shape[1] // FF_CHUNK
    nxt = gate_up(0)
    for c in range(nf):
        gate, up = nxt
        if c + 1 < nf:
            nxt = gate_up(c + 1)
        mid = (gate * jax.nn.sigmoid(gate) * up).astype(BF16)
        acc = acc + jnp.dot(mid, wd_ref[FF_CHUNK * c:FF_CHUNK * (c + 1), :], preferred_element_type=F32)
    return acc


def _dense_ffn_kernel(h_ref, g_ref, wg_ref, wu_ref, wd_ref, out_ref):
    x = h_ref[...]
    hb = _rms(x, g_ref[...]).astype(BF16)
    out_ref[...] = _swiglu_acc(hb, wg_ref, wu_ref, wd_ref, x)


def _dense_ffn(h, g, wg, wu, wd):
    R = h.shape[0]
    dff = wg.shape[1]
    rows = lambda i: (i, 0)
    const = lambda i: (0, 0)
    return pl.pallas_call(
        _dense_ffn_kernel,
        grid=(R // TM,),
        in_specs=[
            pl.BlockSpec((TM, D_MODEL), rows), pl.BlockSpec((1, D_MODEL), const),
            pl.BlockSpec((D_MODEL, dff), const), pl.BlockSpec((D_MODEL, dff), const),
            pl.BlockSpec((dff, D_MODEL), const),
        ],
        out_specs=pl.BlockSpec((TM, D_MODEL), rows),
        out_shape=jax.ShapeDtypeStruct((R, D_MODEL), F32),
        compiler_params=_cparams(("arbitrary",)),
        name="dense_ffn",
    )(h, g, wg, wu, wd)


def _route_kernel(h_ref, g_ref, wr_ref, hn_ref, info_ref, cnt_ref, carry_ref):
    i = pl.program_id(0)

    @pl.when(i == 0)
    def _():
        carry_ref[...] = jnp.zeros_like(carry_ref)

    hn = _rms(h_ref[...], g_ref[...])
    _store_row_tiles(hn_ref, hn)
    h_hi, h_mid, _ = _split3(hn)
    logits = (jnp.dot(h_hi.astype(BF16), wr_ref[0], preferred_element_type=F32)
              + jnp.dot(h_mid.astype(BF16), wr_ref[0], preferred_element_type=F32)
              + jnp.dot(h_hi.astype(BF16), wr_ref[1], preferred_element_type=F32))
    lane = lax.broadcasted_iota(jnp.int32, logits.shape, 1)
    logits = jnp.where(lane < N_EXPERTS, logits, -jnp.inf)
    v1 = jnp.max(logits, axis=-1, keepdims=True)
    e1 = jnp.min(jnp.where(logits == v1, lane, LANES), axis=-1, keepdims=True)
    rest = jnp.where(lane == e1, -jnp.inf, logits)
    v2 = jnp.max(rest, axis=-1, keepdims=True)
    e2 = jnp.min(jnp.where(rest == v2, lane, LANES), axis=-1, keepdims=True)
    ex = jnp.exp(v2 - v1)
    w1 = 1.0 / (1.0 + ex)
    w2 = ex / (1.0 + ex)
    hot1 = jnp.where(lane == e1, 1.0, 0.0)
    hot2 = jnp.where(lane == e2, 1.0, 0.0)
    hot = hot1 + hot2
    before = _tri_cumsum(hot, inclusive=False) + carry_ref[...]
    r1 = jnp.sum(before * hot1, axis=-1, keepdims=True)
    r2 = jnp.sum(before * hot2, axis=-1, keepdims=True)
    total = before[TM - 1:TM, :] + hot[TM - 1:TM, :]
    carry_ref[...] = total
    cnt_ref[...] = jnp.broadcast_to(total, cnt_ref.shape)
    info_ref[...] = jnp.where(lane == 0, e1.astype(F32), jnp.where(lane == 1, e2.astype(F32),
                              jnp.where(lane == 2, r1, jnp.where(lane == 3, r2,
                                        jnp.where(lane == 4, w1, jnp.where(lane == 5, w2, 0.0))))))


def _route(h, g, wr):
    R = h.shape[0]
    rows = lambda i: (i, 0)
    return pl.pallas_call(
        _route_kernel,
        grid=(R // TM,),
        in_specs=[pl.BlockSpec((TM, D_MODEL), rows), pl.BlockSpec((1, D_MODEL), lambda i: (0, 0)),
                  pl.BlockSpec((2, D_MODEL, LANES), lambda i: (0, 0, 0))],
        out_specs=[pl.BlockSpec((TM * ROW_TILE, LANES), rows), pl.BlockSpec((TM, LANES), rows),
                   pl.BlockSpec((8, LANES), lambda i: (0, 0))],
        out_shape=[jax.ShapeDtypeStruct((R * ROW_TILE, LANES), F32), jax.ShapeDtypeStruct((R, LANES), F32),
                   jax.ShapeDtypeStruct((8, LANES), F32)],
        scratch_shapes=[pltpu.VMEM((1, LANES), F32)],
        compiler_params=_cparams(("arbitrary",)),
        name="route",
    )(h, g, wr)


def _scatter_kernel(dest_ref, last_ref, src_ref, out_ref, zero_ref, sem):
    @pl.when(pl.program_id(0) == 0)
    def _():
        zero_ref[...] = jnp.zeros_like(zero_ref)

        def zero_copy(e):
            row = pl.multiple_of(last_ref[0, e] * (TME * ROW_TILE), TME * ROW_TILE)
            return pltpu.make_async_copy(zero_ref, out_ref.at[pl.ds(row, TME * ROW_TILE)], sem)

        for e in range(N_EXPERTS):
            @pl.when(last_ref[1, e] > 0)
            def _(e=e):
                zero_copy(e).start()
        for e in range(N_EXPERTS):
            @pl.when(last_ref[1, e] > 0)
            def _(e=e):
                zero_copy(e).wait()

        def spare_copy(tile):
            row = pl.multiple_of(tile * (TME * ROW_TILE), TME * ROW_TILE)
            return pltpu.make_async_copy(zero_ref, out_ref.at[pl.ds(row, TME * ROW_TILE)], sem)

        def start_spare(tile, c):
            spare_copy(tile).start()
            return c

        def wait_spare(tile, c):
            spare_copy(tile).wait()
            return c

        n_tiles = out_ref.shape[0] // (TME * ROW_TILE)
        lax.fori_loop(last_ref[2, 0], n_tiles, start_spare, 0)
        lax.fori_loop(last_ref[2, 0], n_tiles, wait_spare, 0)

    def copy(r, k):
        return pltpu.make_async_copy(
            src_ref.at[pl.ds(pl.multiple_of(r * ROW_TILE, ROW_TILE), ROW_TILE)],
            out_ref.at[pl.ds(pl.multiple_of(dest_ref[0, 0, 2 * r + k], ROW_TILE), ROW_TILE)], sem)

    def issue(r, c):
        copy(r, 0).start(priority=0)
        copy(r, 1).start(priority=1)
        return c

    lax.fori_loop(0, TM, issue, 0, unroll=DMA_UNROLL)
    for _ in range(2):
        pltpu.make_async_copy(src_ref, out_ref.at[pl.ds(0, TM * ROW_TILE)], sem).wait()


def _scatter_rows(dest, last_tiles, src, n_rows):
    R = src.shape[0] // ROW_TILE
    return pl.pallas_call(
        _scatter_kernel,
        grid=(R // TM,),
        in_specs=[pl.BlockSpec((1, 1, 2 * TM), lambda i: (i, 0, 0), memory_space=pltpu.SMEM),
                  pl.BlockSpec(memory_space=pltpu.SMEM),
                  pl.BlockSpec((TM * ROW_TILE, LANES), lambda i: (i, 0))],
        out_specs=pl.BlockSpec(memory_space=pl.ANY),
        out_shape=jax.ShapeDtypeStruct((n_rows * ROW_TILE, LANES), src.dtype),
        scratch_shapes=[pltpu.VMEM((TME * ROW_TILE, LANES), F32), pltpu.SemaphoreType.DMA(())],
        compiler_params=pltpu.CompilerParams(dimension_semantics=("arbitrary",), vmem_limit_bytes=VMEM_LIMIT,
                                             has_side_effects=True),
        name="scatter_rows",
    )(dest, last_tiles, src)


def _expert_kernel(te_ref, act_ref, x_ref, wg_ref, wu_ref, wd_ref, y_ref, acc_ref):
    i = pl.program_id(0)
    f = pl.program_id(1)
    active = act_ref[i] > 0
    del te_ref

    def step(first):
        xb = _load_row_tiles(x_ref, TME).astype(BF16)
        prev = jnp.zeros(acc_ref.shape, F32) if first else acc_ref[...]
        acc = _swiglu_acc(xb, wg_ref, wu_ref, wd_ref, prev)
        acc_ref[...] = acc
        _store_row_tiles(y_ref, acc)

    @pl.when(active & (f == 0))
    def _():
        step(True)

    @pl.when(active & (f > 0))
    def _():
        step(False)

    @pl.when(jnp.logical_not(active))
    def _():
        y_ref[...] = jnp.zeros_like(y_ref)


def _experts(tile_expert, tile_active, xs, wg, wu, wd):
    mt = tile_expert.shape[0]
    dffe = wg.shape[2]
    grid_spec = pltpu.PrefetchScalarGridSpec(
        num_scalar_prefetch=2,
        grid=(mt, dffe // TF_MOE),
        in_specs=[
            pl.BlockSpec((TME * ROW_TILE, LANES), lambda i, f, te, act: (i, 0)),
            pl.BlockSpec((None, D_MODEL, TF_MOE), lambda i, f, te, act: (te[i], 0, f)),
            pl.BlockSpec((None, D_MODEL, TF_MOE), lambda i, f, te, act: (te[i], 0, f)),
            pl.BlockSpec((None, TF_MOE, D_MODEL), lambda i, f, te, act: (te[i], f, 0)),
        ],
        out_specs=pl.BlockSpec((TME * ROW_TILE, LANES), lambda i, f, te, act: (i, 0)),
        scratch_shapes=[pltpu.VMEM((TME, D_MODEL), F32)],
    )
    return pl.pallas_call(
        _expert_kernel,
        grid_spec=grid_spec,
        out_shape=jax.ShapeDtypeStruct((mt * TME * ROW_TILE, LANES), F32),
        compiler_params=_cparams(("arbitrary", "arbitrary")),
        name="experts",
    )(tile_expert, tile_active, xs, wg, wu, wd)


def _combine_kernel(dest_ref, next_ref, h_ref, info_ref, g_ref, y_ref, out_ref, buf_ref, sem, *, final):
    i = pl.program_id(0)

    def issue(idx_ref, slot):
        def copy(r, k):
            return pltpu.make_async_copy(
                y_ref.at[pl.ds(pl.multiple_of(idx_ref[0, 0, 2 * r + k], ROW_TILE), ROW_TILE)],
                buf_ref.at[slot, k, pl.ds(pl.multiple_of(r * ROW_TILE, ROW_TILE), ROW_TILE)], sem.at[slot])

        def body(r, c):
            copy(r, 0).start(priority=0)
            copy(r, 1).start(priority=1)
            return c

        lax.fori_loop(0, TM, body, 0, unroll=DMA_UNROLL)

    def finish(slot):
        for k in range(2):
            pltpu.make_async_copy(y_ref.at[pl.ds(0, TM * ROW_TILE)], buf_ref.at[slot, k], sem.at[slot]).wait()
        info = info_ref[...]
        out = (h_ref[...] + info[:, 4:5] * _load_row_tiles(buf_ref.at[slot, 0], TM)
               + info[:, 5:6] * _load_row_tiles(buf_ref.at[slot, 1], TM))
        out_ref[...] = _rms(out, g_ref[...]) if final else out

    @pl.when(i == 0)
    def _():
        issue(dest_ref, 0)

    for slot in range(2):
        @pl.when(lax.rem(i, 2) == slot)
        def _(slot=slot):
            @pl.when(i + 1 < pl.num_programs(0))
            def _():
                issue(next_ref, 1 - slot)

            finish(slot)


def _combine(dest, h, info, y, final_g=None, n_rows=None):
    final = final_g is not None
    n_rows = n_rows if final else h.shape[0]
    g = final_g if final else jnp.ones((1, D_MODEL), F32)
    rows = lambda i: (i, 0)
    nt = n_rows // TM
    return pl.pallas_call(
        functools.partial(_combine_kernel, final=final),
        grid=(nt,),
        in_specs=[pl.BlockSpec((1, 1, 2 * TM), lambda i: (i, 0, 0), memory_space=pltpu.SMEM),
                  pl.BlockSpec((1, 1, 2 * TM), lambda i: (jnp.minimum(i + 1, nt - 1), 0, 0),
                               memory_space=pltpu.SMEM),
                  pl.BlockSpec((TM, D_MODEL), rows), pl.BlockSpec((TM, LANES), rows),
                  pl.BlockSpec((1, D_MODEL), lambda i: (0, 0)), pl.BlockSpec(memory_space=pl.ANY)],
        out_specs=pl.BlockSpec((TM, D_MODEL), rows),
        out_shape=jax.ShapeDtypeStruct((n_rows, D_MODEL), F32),
        scratch_shapes=[pltpu.VMEM((2, 2, TM * ROW_TILE, LANES), F32), pltpu.SemaphoreType.DMA((2,))],
        compiler_params=_cparams(("arbitrary",)),
        name="combine",
    )(dest, dest, h, info, g, y)


def _moe_ffn(h, g, wr, wg, wu, wd, final_g=None, n_rows=None):
    R = h.shape[0]
    hn, info, counts = _route(h, g, wr)
    cnt = counts[0, :N_EXPERTS].astype(jnp.int32)
    padded = ((cnt + TME - 1) // TME) * TME
    ends = jnp.cumsum(padded)
    starts = ends - padded
    mt = (2 * R) // TME + N_EXPERTS
    tile_row = jnp.arange(mt, dtype=jnp.int32) * TME
    tile_expert = jnp.minimum(jnp.searchsorted(ends, tile_row, side="right"), N_EXPERTS - 1).astype(jnp.int32)
    tile_active = (tile_row < ends[-1]).astype(jnp.int32)
    e12 = info[:, 0:2].astype(jnp.int32)
    dest = ((starts[e12] + info[:, 2:4].astype(jnp.int32)) * ROW_TILE).reshape(R // TM, 1, 2 * TM)
    last_tiles = jnp.stack([jnp.maximum(ends // TME - 1, 0), (padded > 0).astype(jnp.int32),
                            jnp.broadcast_to(ends[-1] // TME, (N_EXPERTS,))]).astype(jnp.int32)
    xs = _scatter_rows(dest, last_tiles, hn, mt * TME)
    y = _experts(tile_expert, tile_active, xs, wg, wu, wd)
    return _combine(dest, h, info, y, final_g, n_rows)


def _final_kernel(h_ref, g_ref, out_ref):
    out_ref[...] = _rms(h_ref[...], g_ref[...])


def _final_norm(h, g, n_rows):
    rows = lambda i: (i, 0)
    return pl.pallas_call(
        _final_kernel,
        grid=(n_rows // TM,),
        in_specs=[pl.BlockSpec((TM, D_MODEL), rows), pl.BlockSpec((1, D_MODEL), lambda i: (0, 0))],
        out_specs=pl.BlockSpec((TM, D_MODEL), rows),
        out_shape=jax.ShapeDtypeStruct((n_rows, D_MODEL), F32),
        compiler_params=_cparams(("arbitrary",)),
        name="final_norm",
    )(h, g)


def _forget_column_placement():
    row = jnp.arange(3 * LANES)[:, None]
    col = jnp.arange(LANES)[None, :]
    term, head = row // LANES, row % LANES
    valid = head < FOX_HEADS
    plus = valid & (col == 6 * head + term)
    minus = valid & (col == 6 * head + 3 + term)
    return (plus.astype(F32) - minus.astype(F32)).astype(BF16)


def _rope_table(S):
    rd = DIFF_DH // 4
    inv = ROPE_THETA ** (-jnp.arange(0, rd, 2, dtype=F32) / rd)
    pos = jnp.concatenate([jnp.arange(N_META, N_META + S, dtype=F32), jnp.arange(TM, dtype=F32)])
    ang = pos[:, None] * inv[None, :]
    cos, sin = jnp.cos(ang), jnp.sin(ang)
    n = pos.shape[0]
    cos_t = jnp.tile(jnp.concatenate([cos, cos, jnp.ones((n, 48), F32)], axis=1), (1, 2))
    sin_lo = jnp.tile(jnp.concatenate([-sin, jnp.zeros((n, 56), F32)], axis=1), (1, 2))
    sin_hi = jnp.tile(jnp.concatenate([jnp.zeros((n, 8), F32), sin, jnp.zeros((n, 48), F32)], axis=1), (1, 2))
    return jnp.concatenate([cos_t, sin_lo, sin_hi], axis=1)


def kernel(x, meta_tokens, norm_mix_g, w_in, b_forget, diff_lambda, diff_subln_g, w_branch_diff, w_branch_fox,
           w_out, norm_ffn_g, ffn_w_gate, ffn_w_up, ffn_w_down, moe_router, moe_w_gate, moe_w_up, moe_w_down,
           final_norm_g):
    B, S, D = x.shape
    depth = w_in.shape[0]
    assert D == D_MODEL and S % TQ == 0 and meta_tokens.shape[0] == N_META
    nq = S // TM
    h = jnp.concatenate([x.reshape(B * S, D), meta_tokens.astype(x.dtype),
                         jnp.zeros((TQ - N_META, D), x.dtype)], axis=0)
    rope = _rope_table(S)
    g_final = final_norm_g.astype(F32).reshape(1, D)
    e6 = _forget_column_placement()
    scale = DIFF_DH ** -0.5 * LOG2E
    for layer in range(depth):
        lam_init = 0.8 - 0.6 * math.exp(-0.3 * layer)
        w = w_in[layer]
        dq, dk, dv, fq, fk, fv, ff, ga, gb = jnp.split(
            w, [HALF, 2 * HALF, 3 * HALF, 4 * HALF, 5 * HALF, 6 * HALF, 6 * HALF + FOX_HEADS,
                6 * HALF + FOX_HEADS + D_MODEL], axis=1)
        w1 = jnp.concatenate([jnp.pad(ff, ((0, 0), (0, LANES - FOX_HEADS))), dq * scale, dk], axis=1).astype(BF16)
        w2 = jnp.concatenate([dv, fv], axis=1).T.astype(BF16)
        w3 = jnp.concatenate([fq * scale, fk], axis=1).astype(BF16)
        bfp = jnp.pad(b_forget[layer].astype(F32), (0, LANES - FOX_HEADS)).reshape(1, LANES)
        g_mix = norm_mix_g[layer].astype(F32).reshape(1, D)
        dq_a, dk_a, vt_a, fq_a, fk_a = _inproj(h, g_mix, w1, w2, w3, e6, bfp, rope, nq=nq)

        lp = diff_lambda[layer].astype(F32)
        lam = jnp.exp(jnp.sum(lp[0] * lp[1])) - jnp.exp(jnp.sum(lp[2] * lp[3])) + lam_init
        par = jnp.zeros((8, LANES), F32)
        par = par.at[1].set(diff_subln_g[layer].astype(F32)).at[2].set(1.0 - lam_init).at[3].set(lam)
        o_a = _attention(dq_a, dk_a, vt_a, par, diff=True, nq=nq, nb=B)
        o_b = _attention(fq_a, fk_a, vt_a, None, diff=False, nq=nq, nb=B)

        wgate = jnp.concatenate([ga, gb], axis=1).astype(BF16)
        h = _mixout(h, o_a, o_b, g_mix, wgate, w_branch_diff[layer].astype(BF16),
                    w_branch_fox[layer].astype(BF16), w_out[layer].astype(BF16))

        g_ffn = norm_ffn_g[layer].astype(F32).reshape(1, D)
        jj = layer // 2
        if layer % 2 == 0:
            h = _dense_ffn(h, g_ffn, ffn_w_gate[jj].astype(BF16), ffn_w_up[jj].astype(BF16),
                           ffn_w_down[jj].astype(BF16))
        else:
            r_hi, r_mid, _ = _split3(jnp.pad(moe_router[jj].astype(F32), ((0, 0), (0, LANES - N_EXPERTS))))
            wr = jnp.stack([r_hi, r_mid]).astype(BF16)
            last = layer == depth - 1
            h = _moe_ffn(h, g_ffn, wr, moe_w_gate[jj].astype(BF16), moe_w_up[jj].astype(BF16),
                         moe_w_down[jj].astype(BF16), g_final if last else None, B * S)
    out = h if depth % 2 == 0 else _final_norm(h, g_final, B * S)
    return out.reshape(B, S, D)
```

```python
import functools
import math

import jax
import jax.numpy as jnp
from jax import lax
from jax.experimental import pallas as pl
from jax.experimental.pallas import tpu as pltpu

D_MODEL = 1024
N_META = 16
ROPE_THETA = 500000.0
RMS_EPS = 1e-6
NEG_INF = -1e30

DIFF_HEADS = 4
DIFF_DH = 64
FOX_HEADS = 8
FOX_DH = 64
HALF = 512
N_EXPERTS = 8
LANES = 128

TM = 512
KEY_TILES_PER_Q = 2
TQ = KEY_TILES_PER_Q * TM
TME = 512
TF_MOE = 1792
FF_CHUNK = 256
ROW_TILE = 8
DMA_UNROLL = 8
LOG2E = 1.4426950408889634
ONES_ROWS = 16
VMEM_LIMIT = 56 * 1024 * 1024

F32 = jnp.float32
BF16 = jnp.bfloat16


def _cparams(sem):
    return pltpu.CompilerParams(dimension_semantics=sem, vmem_limit_bytes=VMEM_LIMIT)


def _rms(x, g):
    ms = jnp.mean(x * x, axis=-1, keepdims=True)
    return x * lax.rsqrt(ms + RMS_EPS) * g


def _split3(x):
    hi = x.astype(BF16).astype(F32)
    r = x - hi
    mid = r.astype(BF16).astype(F32)
    lo = (r - mid).astype(BF16).astype(F32)
    return hi, mid, lo


def _tri_cumsum(x, inclusive):
    n = x.shape[0]
    row = lax.broadcasted_iota(jnp.int32, (n, n), 0)
    col = lax.broadcasted_iota(jnp.int32, (n, n), 1)
    tri = jnp.where((col <= row) if inclusive else (col < row), 1.0, 0.0).astype(BF16)
    parts = jnp.concatenate([p.astype(BF16) for p in _split3(x)], axis=1)
    out = jnp.dot(tri, parts, preferred_element_type=F32)
    return out[:, :LANES] + out[:, LANES:2 * LANES] + out[:, 2 * LANES:]


def _store_row_tiles(ref, x):
    rows = x.shape[0]
    for s in range(ROW_TILE):
        ref[pl.ds(s, rows, stride=ROW_TILE), :] = x[:, LANES * s:LANES * (s + 1)]


def _load_row_tiles(ref, rows):
    return jnp.concatenate([ref[pl.ds(s, rows, stride=ROW_TILE), :] for s in range(ROW_TILE)], axis=1)


def _inproj_kernel(h_ref, g_ref, w1_ref, w2_ref, w3_ref, e6_ref, bf_ref, rope_ref,
                   dq_ref, dk_ref, vt_ref, fq_ref, fk_ref, carry_ref, mcarry_ref, *, nq):
    i = pl.program_id(0)

    @pl.when(i == 0)
    def _():
        carry_ref[...] = jnp.zeros_like(carry_ref)
        mcarry_ref[...] = jnp.zeros_like(mcarry_ref)

    hb = _rms(h_ref[...], g_ref[...]).astype(BF16)

    z1 = jnp.dot(hb, w1_ref[...], preferred_element_type=F32)

    z4 = z1[:, :LANES] + bf_ref[...]
    lane = lax.broadcasted_iota(jnp.int32, z4.shape, 1)
    logf = jnp.minimum(z4, 0.0) - jnp.log1p(jnp.exp(-jnp.abs(z4)))
    logf = jnp.where(lane < FOX_HEADS, logf, 0.0)
    j_in_batch = lax.rem(jnp.maximum(i - 1, 0), nq)
    base = jnp.where(i == 0, 0.0, jnp.where(j_in_batch == 0, mcarry_ref[...], carry_ref[...]))
    c = _tri_cumsum(logf, inclusive=True) + base

    vt_ref[0] = lax.dot_general(w2_ref[...], hb, (((1,), (1,)), ((), ())),
                                preferred_element_type=F32).astype(BF16)

    parts = jnp.concatenate([p.astype(BF16) for p in _split3(c * LOG2E)], axis=1)
    c6 = jnp.dot(parts, e6_ref[...], preferred_element_type=F32)
    z3 = jnp.dot(hb, w3_ref[...], preferred_element_type=F32)

    cos_t = rope_ref[:, 0:LANES]
    sin_lo = rope_ref[:, LANES:2 * LANES]
    sin_hi = rope_ref[:, 2 * LANES:3 * LANES]
    for j in range(8):
        zj = z1[:, LANES * (j + 1):LANES * (j + 2)]
        rot = zj * cos_t + pltpu.roll(zj, LANES - 8, 1) * sin_lo + pltpu.roll(zj, 8, 1) * sin_hi
        dst = dq_ref if j < 4 else dk_ref
        dst[:, LANES * (j % 4):LANES * (j % 4 + 1)] = rot.astype(BF16)

    @pl.when(i == 0)
    def _():
        mcarry_ref[...] = c[N_META - 1:N_META, :]

    carry_ref[...] = c[TM - 1:TM, :]

    for hd in range(FOX_HEADS):
        off = 64 if hd % 2 == 0 else 0
        moved = pltpu.roll(c6, (off - 6 * hd) % LANES, 1)
        first3 = (lane >= off) & (lane < off + 3)
        last3 = (lane >= off + 3) & (lane < off + 6)
        aug_q = jnp.where(first3, moved, jnp.where(last3, 1.0, 0.0))
        aug_k = jnp.where(last3, moved, jnp.where(first3, 1.0, 0.0))
        slab = hd // 2
        keep = (lane < 64) if hd % 2 == 0 else (lane >= 64)
        zq = z3[:, LANES * slab:LANES * (slab + 1)]
        zk = z3[:, HALF + LANES * slab:HALF + LANES * (slab + 1)]
        fq_ref[:, LANES * hd:LANES * (hd + 1)] = jnp.where(keep, zq, aug_q).astype(BF16)
        fk_ref[:, LANES * hd:LANES * (hd + 1)] = jnp.where(keep, zk, aug_k).astype(BF16)


def _inproj(h, g, w1, w2, w3, e6, bfp, rope, *, nq):
    R = h.shape[0]
    nt = R // TM
    n_real = nt - TQ // TM
    rows = lambda i: (jnp.where(i == 0, n_real, jnp.where(i <= n_real, i - 1, i)), 0)
    rope_rows = lambda i: (jnp.where(i == 0, nq, lax.rem(jnp.maximum(i - 1, 0), nq)), 0)
    const = lambda i: (0, 0)
    out_sd = lambda w: jax.ShapeDtypeStruct((R, w), BF16)
    return pl.pallas_call(
        functools.partial(_inproj_kernel, nq=nq),
        grid=(nt,),
        in_specs=[
            pl.BlockSpec((TM, D_MODEL), rows),
            pl.BlockSpec((1, D_MODEL), const),
            pl.BlockSpec((D_MODEL, LANES + 2 * HALF), const),
            pl.BlockSpec((2 * HALF, D_MODEL), const),
            pl.BlockSpec((D_MODEL, 2 * HALF), const),
            pl.BlockSpec((3 * LANES, LANES), const),
            pl.BlockSpec((1, LANES), const),
            pl.BlockSpec((TM, 3 * LANES), rope_rows),
        ],
        out_specs=[
            pl.BlockSpec((TM, HALF), rows), pl.BlockSpec((TM, HALF), rows),
            pl.BlockSpec((1, 2 * HALF, TM), lambda i: (rows(i)[0], 0, 0)),
            pl.BlockSpec((TM, 2 * HALF), rows), pl.BlockSpec((TM, 2 * HALF), rows),
        ],
        out_shape=[out_sd(HALF), out_sd(HALF), jax.ShapeDtypeStruct((nt, 2 * HALF, TM), BF16),
                   out_sd(2 * HALF), out_sd(2 * HALF)],
        scratch_shapes=[pltpu.VMEM((1, LANES), F32), pltpu.VMEM((1, LANES), F32)],
        compiler_params=_cparams(("arbitrary",)),
        name="inproj",
    )(h, g, w1, w2, w3, e6, bfp, rope)


def _attn_kernel(*refs, diff, nq, nb):
    if diff:
        q_ref, k_ref, vt_ref, km_ref, vtm_ref, par_ref, o_ref, acc_ref, m_ref, q_scr, sa_ref, sb_ref = refs
    else:
        q_ref, k_ref, vt_ref, km_ref, vtm_ref, o_ref, acc_ref, m_ref, q_scr, sa_ref, sb_ref = refs
    t = pl.program_id(1)
    nqt = nq // KEY_TILES_PER_Q
    is_real = t < nb * nqt
    jq = lax.rem(t, nqt)
    dv = acc_ref.shape[1] - ONES_ROWS

    for sub in range(2):
        if diff:
            q = q_ref[...]
            lane = lax.broadcasted_iota(jnp.int32, q.shape, 1)
            q_scr[sub] = jnp.where((lane < 64) if sub == 0 else (lane >= 64), q, jnp.zeros_like(q))
        else:
            q_scr[sub] = q_ref[:, LANES * sub:LANES * (sub + 1)]
        m_ref[sub] = jnp.full(m_ref.shape[1:], NEG_INF, F32)
        acc_ref[sub] = jnp.zeros(acc_ref.shape[1:], F32)

    def keys_of(sub, k_tile):
        return k_tile if diff else k_tile[:, LANES * sub:LANES * (sub + 1)]

    def values_of(sub, vt_tile):
        ones = jnp.ones((ONES_ROWS, vt_tile.shape[1]), BF16)
        vt = vt_tile if diff else vt_tile[dv * sub:dv * (sub + 1), :]
        return jnp.concatenate([vt, ones], axis=0)

    def scores(sub, k, q0, q1=TQ):
        return lax.dot_general(k, q_scr[sub, q0:q1, :], (((1,), (1,)), ((), ())), preferred_element_type=F32)

    def update(sub, st, vt, mask, q0, q1=TQ):
        if mask is not None:
            st = jnp.where(mask, st, NEG_INF)
        m_prev = m_ref[sub, :, q0:q1]
        m_new = jnp.maximum(m_prev, jnp.max(st, axis=0, keepdims=True))
        p = jnp.exp2(st - m_new).astype(BF16)
        if p.shape[0] < vt.shape[1]:
            p = jnp.concatenate([p, jnp.zeros((vt.shape[1] - p.shape[0], p.shape[1]), BF16)], axis=0)
        acc_ref[sub, :, q0:q1] = (jnp.exp2(m_prev - m_new) * acc_ref[sub, :, q0:q1]
                                  + jnp.dot(vt, p, preferred_element_type=F32))
        m_ref[sub, :, q0:q1] = m_new

    def scores_into(buf, tile, q0, subs=(0, 1), q1=TQ):
        k_tile = k_ref[pl.ds(pl.multiple_of(tile * TM, TM), TM), :]
        for sub in subs:
            buf[sub, :, q0:q1] = scores(sub, keys_of(sub, k_tile), q0, q1)

    def update_from(buf, tile, causal, q0, subs=(0, 1), q1=TQ):
        vt_tile = vt_ref[tile]
        mask = None
        if causal:
            key = lax.broadcasted_iota(jnp.int32, (TM, q1 - q0), 0)
            qry = lax.broadcasted_iota(jnp.int32, (TM, q1 - q0), 1)
            mask = key <= qry
        for sub in subs:
            update(sub, buf[sub, :, q0:q1], values_of(sub, vt_tile), mask, q0, q1)

    key = lax.broadcasted_iota(jnp.int32, (N_META, TQ), 0)
    qry = lax.broadcasted_iota(jnp.int32, (N_META, TQ), 1)
    meta_mask = key <= jnp.where(is_real, N_META - 1, qry)
    km_tile = km_ref[...]
    vtm_tile = vtm_ref[0][:, :LANES]

    for sub in range(2):
        meta_scores = scores(sub, keys_of(sub, km_tile), 0)
        scores_into(sa_ref, 0, 0, (sub,))
        update(sub, meta_scores, values_of(sub, vtm_tile), meta_mask, 0)

    def pair(i, carry):
        for cur, nxt, tile in ((sa_ref, sb_ref, 2 * i), (sb_ref, sa_ref, 2 * i + 1)):
            for sub in range(2):
                for q0 in range(0, TQ, TM):
                    scores_into(nxt, tile + 1, q0, (sub,), q0 + TM)
                    update_from(cur, tile, False, q0, (sub,), q0 + TM)
        return carry

    lax.fori_loop(0, jnp.where(is_real, jq, 0), pair, 0)

    @pl.when(is_real)
    def _():
        for sub in range(2):
            scores_into(sb_ref, 2 * jq + 1, TM, (sub,))
            update_from(sa_ref, 2 * jq, True, 0, (sub,), TM)
            update_from(sa_ref, 2 * jq, False, TM, (sub,))
        update_from(sb_ref, 2 * jq + 1, True, TM)

    a0 = acc_ref[0]
    a1 = acc_ref[1]
    o0 = a0[:dv] / a0[dv:dv + 1]
    o1 = a1[:dv] / a1[dv:dv + 1]
    if diff:
        d = (o0 - par_ref[3:4, 0:1] * o1).T
        o_ref[...] = (_rms(d, par_ref[1:2, :]) * par_ref[2:3, :]).astype(BF16)
    else:
        o_ref[...] = jnp.concatenate([o0, o1], axis=0).T.astype(BF16)


def _attention(q, k, vt, par, *, diff, nq, nb):
    R = q.shape[0]
    S = nq * TM
    ntq = R // TQ
    qw = LANES if diff else 2 * LANES
    voff = 0 if diff else HALF // LANES
    meta_blk = (nb * S) // N_META
    batch_of = lambda t: jnp.minimum(t // (nq // KEY_TILES_PER_Q), nb - 1)
    in_specs = [
        pl.BlockSpec((TQ, qw), lambda p, t: (t, p)),
        pl.BlockSpec((S, qw), lambda p, t: (batch_of(t), p)),
        pl.BlockSpec((nq, LANES, TM), lambda p, t: (batch_of(t), p + voff, 0)),
        pl.BlockSpec((N_META, qw), lambda p, t: (meta_blk, p)),
        pl.BlockSpec((1, LANES, TM), lambda p, t: (nb * nq, p + voff, 0)),
    ]
    args = [q, k, vt, k, vt]
    if diff:
        in_specs.append(pl.BlockSpec((8, LANES), lambda p, t: (0, 0)))
        args.append(par)
    acc_rows = (LANES if diff else LANES // 2) + ONES_ROWS
    return pl.pallas_call(
        functools.partial(_attn_kernel, diff=diff, nq=nq, nb=nb),
        grid=(4, ntq),
        in_specs=in_specs,
        out_specs=pl.BlockSpec((TQ, LANES), lambda p, t: (t, p)),
        out_shape=jax.ShapeDtypeStruct((R, HALF), BF16),
        scratch_shapes=[pltpu.VMEM((2, acc_rows, TQ), F32), pltpu.VMEM((2, 1, TQ), F32),
                        pltpu.VMEM((2, TQ, LANES), BF16),
                        pltpu.VMEM((2, TM, TQ), F32), pltpu.VMEM((2, TM, TQ), F32)],
        compiler_params=_cparams(("arbitrary", "arbitrary")),
        name="diff_attn" if diff else "fox_attn",
    )(*args)


def _mixout_kernel(h_ref, oa_ref, ob_ref, g_ref, wg_ref, wbd_ref, wbf_ref, wo_ref, out_ref):
    x = h_ref[...]
    hb = _rms(x, g_ref[...]).astype(BF16)
    gates = jax.nn.sigmoid(jnp.dot(hb, wg_ref[...], preferred_element_type=F32))
    a = jnp.dot(oa_ref[...], wbd_ref[...], preferred_element_type=F32)
    b = jnp.dot(ob_ref[...], wbf_ref[...], preferred_element_type=F32)
    merged = gates[:, :D_MODEL] * a + gates[:, D_MODEL:] * b
    out_ref[...] = x + jnp.dot(merged.astype(BF16), wo_ref[...], preferred_element_type=F32)


def _mixout(h, oa, ob, g, wg, wbd, wbf, wo):
    R = h.shape[0]
    rows = lambda i: (i, 0)
    const = lambda i: (0, 0)
    return pl.pallas_call(
        _mixout_kernel,
        grid=(R // TM,),
        in_specs=[
            pl.BlockSpec((TM, D_MODEL), rows), pl.BlockSpec((TM, HALF), rows), pl.BlockSpec((TM, HALF), rows),
            pl.BlockSpec((1, D_MODEL), const), pl.BlockSpec((D_MODEL, 2 * D_MODEL), const),
            pl.BlockSpec((HALF, D_MODEL), const), pl.BlockSpec((HALF, D_MODEL), const),
            pl.BlockSpec((D_MODEL, D_MODEL), const),
        ],
        out_specs=pl.BlockSpec((TM, D_MODEL), rows),
        out_shape=jax.ShapeDtypeStruct((R, D_MODEL), F32),
        compiler_params=_cparams(("arbitrary",)),
        name="mixout",
    )(h, oa, ob, g, wg, wbd, wbf, wo)


def _swiglu_acc(xb, wg_ref, wu_ref, wd_ref, acc):
    def gate_up(c):
        sl = slice(FF_CHUNK * c, FF_CHUNK * (c + 1))
        return (jnp.dot(xb, wg_ref[:, sl], preferred_element_type=F32),
                jnp.dot(xb, wu_ref[:, sl], preferred_element_type=F32))

    nf = wg_ref.shape[1] // FF_CHUNK
    nxt = gate_up(0)
    for c in range(nf):
        gate, up = nxt
        if c + 1 < nf:
            nxt = gate_up(c + 1)
        mid = (gate * jax.nn.sigmoid(gate) * up).astype(BF16)
        acc = acc + jnp.dot(mid, wd_ref[FF_CHUNK * c:FF_CHUNK * (c + 1), :], preferred_element_type=F32)
    return acc


def _dense_ffn_kernel(h_ref, g_ref, wg_ref, wu_ref, wd_ref, out_ref):
    x = h_ref[...]
    hb = _rms(x, g_ref[...]).astype(BF16)
    out_ref[...] = _swiglu_acc(hb, wg_ref, wu_ref, wd_ref, x)


def _dense_ffn(h, g, wg, wu, wd):
    R = h.shape[0]
    dff = wg.shape[1]
    rows = lambda i: (i, 0)
    const = lambda i: (0, 0)
    return pl.pallas_call(
        _dense_ffn_kernel,
        grid=(R // TM,),
        in_specs=[
            pl.BlockSpec((TM, D_MODEL), rows), pl.BlockSpec((1, D_MODEL), const),
            pl.BlockSpec((D_MODEL, dff), const), pl.BlockSpec((D_MODEL, dff), const),
            pl.BlockSpec((dff, D_MODEL), const),
        ],
        out_specs=pl.BlockSpec((TM, D_MODEL), rows),
        out_shape=jax.ShapeDtypeStruct((R, D_MODEL), F32),
        compiler_params=_cparams(("arbitrary",)),
        name="dense_ffn",
    )(h, g, wg, wu, wd)


def _route_kernel(h_ref, g_ref, wr_ref, hn_ref, info_ref, cnt_ref, carry_ref):
    i = pl.program_id(0)

    @pl.when(i == 0)
    def _():
        carry_ref[...] = jnp.zeros_like(carry_ref)

    hn = _rms(h_ref[...], g_ref[...])
    _store_row_tiles(hn_ref, hn)
    h_hi, h_mid, _ = _split3(hn)
    logits = (jnp.dot(h_hi.astype(BF16), wr_ref[0], preferred_element_type=F32)
              + jnp.dot(h_mid.astype(BF16), wr_ref[0], preferred_element_type=F32)
              + jnp.dot(h_hi.astype(BF16), wr_ref[1], preferred_element_type=F32))
    lane = lax.broadcasted_iota(jnp.int32, logits.shape, 1)
    logits = jnp.where(lane < N_EXPERTS, logits, -jnp.inf)
    v1 = jnp.max(logits, axis=-1, keepdims=True)
    e1 = jnp.min(jnp.where(logits == v1, lane, LANES), axis=-1, keepdims=True)
    rest = jnp.where(lane == e1, -jnp.inf, logits)
    v2 = jnp.max(rest, axis=-1, keepdims=True)
    e2 = jnp.min(jnp.where(rest == v2, lane, LANES), axis=-1, keepdims=True)
    ex = jnp.exp(v2 - v1)
    w1 = 1.0 / (1.0 + ex)
    w2 = ex / (1.0 + ex)
    hot1 = jnp.where(lane == e1, 1.0, 0.0)
    hot2 = jnp.where(lane == e2, 1.0, 0.0)
    hot = hot1 + hot2
    before = _tri_cumsum(hot, inclusive=False) + carry_ref[...]
    r1 = jnp.sum(before * hot1, axis=-1, keepdims=True)
    r2 = jnp.sum(before * hot2, axis=-1, keepdims=True)
    total = before[TM - 1:TM, :] + hot[TM - 1:TM, :]
    carry_ref[...] = total
    cnt_ref[...] = jnp.broadcast_to(total, cnt_ref.shape)
    info_ref[...] = jnp.where(lane == 0, e1.astype(F32), jnp.where(lane == 1, e2.astype(F32),
                              jnp.where(lane == 2, r1, jnp.where(lane == 3, r2,
                                        jnp.where(lane == 4, w1, jnp.where(lane == 5, w2, 0.0))))))


def _route(h, g, wr):
    R = h.shape[0]
    rows = lambda i: (i, 0)
    return pl.pallas_call(
        _route_kernel,
        grid=(R // TM,),
        in_specs=[pl.BlockSpec((TM, D_MODEL), rows), pl.BlockSpec((1, D_MODEL), lambda i: (0, 0)),
                  pl.BlockSpec((2, D_MODEL, LANES), lambda i: (0, 0, 0))],
        out_specs=[pl.BlockSpec((TM * ROW_TILE, LANES), rows), pl.BlockSpec((TM, LANES), rows),
                   pl.BlockSpec((8, LANES), lambda i: (0, 0))],
        out_shape=[jax.ShapeDtypeStruct((R * ROW_TILE, LANES), F32), jax.ShapeDtypeStruct((R, LANES), F32),
                   jax.ShapeDtypeStruct((8, LANES), F32)],
        scratch_shapes=[pltpu.VMEM((1, LANES), F32)],
        compiler_params=_cparams(("arbitrary",)),
        name="route",
    )(h, g, wr)


def _scatter_kernel(dest_ref, last_ref, src_ref, out_ref, zero_ref, sem):
    @pl.when(pl.program_id(0) == 0)
    def _():
        zero_ref[...] = jnp.zeros_like(zero_ref)

        def zero_copy(e):
            row = pl.multiple_of(last_ref[0, e] * (TME * ROW_TILE), TME * ROW_TILE)
            return pltpu.make_async_copy(zero_ref, out_ref.at[pl.ds(row, TME * ROW_TILE)], sem)

        for e in range(N_EXPERTS):
            @pl.when(last_ref[1, e] > 0)
            def _(e=e):
                zero_copy(e).start()
        for e in range(N_EXPERTS):
            @pl.when(last_ref[1, e] > 0)
            def _(e=e):
                zero_copy(e).wait()

        def spare_copy(tile):
            row = pl.multiple_of(tile * (TME * ROW_TILE), TME * ROW_TILE)
            return pltpu.make_async_copy(zero_ref, out_ref.at[pl.ds(row, TME * ROW_TILE)], sem)

        def start_spare(tile, c):
            spare_copy(tile).start()
            return c

        def wait_spare(tile, c):
            spare_copy(tile).wait()
            return c

        n_tiles = out_ref.shape[0] // (TME * ROW_TILE)
        lax.fori_loop(last_ref[2, 0], n_tiles, start_spare, 0)
        lax.fori_loop(last_ref[2, 0], n_tiles, wait_spare, 0)

    def copy(r, k):
        return pltpu.make_async_copy(
            src_ref.at[pl.ds(pl.multiple_of(r * ROW_TILE, ROW_TILE), ROW_TILE)],
            out_ref.at[pl.ds(pl.multiple_of(dest_ref[0, 0, 2 * r + k], ROW_TILE), ROW_TILE)], sem)

    def issue(r, c):
        copy(r, 0).start(priority=0)
        copy(r, 1).start(priority=1)
        return c

    lax.fori_loop(0, TM, issue, 0, unroll=DMA_UNROLL)
    for _ in range(2):
        pltpu.make_async_copy(src_ref, out_ref.at[pl.ds(0, TM * ROW_TILE)], sem).wait()


def _scatter_rows(dest, last_tiles, src, n_rows):
    R = src.shape[0] // ROW_TILE
    return pl.pallas_call(
        _scatter_kernel,
        grid=(R // TM,),
        in_specs=[pl.BlockSpec((1, 1, 2 * TM), lambda i: (i, 0, 0), memory_space=pltpu.SMEM),
                  pl.BlockSpec(memory_space=pltpu.SMEM),
                  pl.BlockSpec((TM * ROW_TILE, LANES), lambda i: (i, 0))],
        out_specs=pl.BlockSpec(memory_space=pl.ANY),
        out_shape=jax.ShapeDtypeStruct((n_rows * ROW_TILE, LANES), src.dtype),
        scratch_shapes=[pltpu.VMEM((TME * ROW_TILE, LANES), F32), pltpu.SemaphoreType.DMA(())],
        compiler_params=pltpu.CompilerParams(dimension_semantics=("arbitrary",), vmem_limit_bytes=VMEM_LIMIT,
                                             has_side_effects=True),
        name="scatter_rows",
    )(dest, last_tiles, src)


def _expert_kernel(te_ref, act_ref, x_ref, wg_ref, wu_ref, wd_ref, y_ref, acc_ref):
    i = pl.program_id(0)
    f = pl.program_id(1)
    active = act_ref[i] > 0
    del te_ref

    def step(first):
        xb = _load_row_tiles(x_ref, TME).astype(BF16)
        prev = jnp.zeros(acc_ref.shape, F32) if first else acc_ref[...]
        acc = _swiglu_acc(xb, wg_ref, wu_ref, wd_ref, prev)
        acc_ref[...] = acc
        _store_row_tiles(y_ref, acc)

    @pl.when(active & (f == 0))
    def _():
        step(True)

    @pl.when(active & (f > 0))
    def _():
        step(False)

    @pl.when(jnp.logical_not(active))
    def _():
        y_ref[...] = jnp.zeros_like(y_ref)


def _experts(tile_expert, tile_active, xs, wg, wu, wd):
    mt = tile_expert.shape[0]
    dffe = wg.shape[2]
    grid_spec = pltpu.PrefetchScalarGridSpec(
        num_scalar_prefetch=2,
        grid=(mt, dffe // TF_MOE),
        in_specs=[
            pl.BlockSpec((TME * ROW_TILE, LANES), lambda i, f, te, act: (i, 0)),
            pl.BlockSpec((None, D_MODEL, TF_MOE), lambda i, f, te, act: (te[i], 0, f)),
            pl.BlockSpec((None, D_MODEL, TF_MOE), lambda i, f, te, act: (te[i], 0, f)),
            pl.BlockSpec((None, TF_MOE, D_MODEL), lambda i, f, te, act: (te[i], f, 0)),
        ],
        out_specs=pl.BlockSpec((TME * ROW_TILE, LANES), lambda i, f, te, act: (i, 0)),
        scratch_shapes=[pltpu.VMEM((TME, D_MODEL), F32)],
    )
    return pl.pallas_call(
        _expert_kernel,
        grid_spec=grid_spec,
        out_shape=jax.ShapeDtypeStruct((mt * TME * ROW_TILE, LANES), F32),
        compiler_params=_cparams(("arbitrary", "arbitrary")),
        name="experts",
    )(tile_expert, tile_active, xs, wg, wu, wd)


def _combine_kernel(dest_ref, next_ref, h_ref, info_ref, g_ref, y_ref, out_ref, buf_ref, sem, *, final):
    i = pl.program_id(0)

    def issue(idx_ref, slot):
        def copy(r, k):
            return pltpu.make_async_copy(
                y_ref.at[pl.ds(pl.multiple_of(idx_ref[0, 0, 2 * r + k], ROW_TILE), ROW_TILE)],
                buf_ref.at[slot, k, pl.ds(pl.multiple_of(r * ROW_TILE, ROW_TILE), ROW_TILE)], sem.at[slot])

        def body(r, c):
            copy(r, 0).start(priority=0)
            copy(r, 1).start(priority=1)
            return c

        lax.fori_loop(0, TM, body, 0, unroll=DMA_UNROLL)

    def finish(slot):
        for k in range(2):
            pltpu.make_async_copy(y_ref.at[pl.ds(0, TM * ROW_TILE)], buf_ref.at[slot, k], sem.at[slot]).wait()
        info = info_ref[...]
        out = (h_ref[...] + info[:, 4:5] * _load_row_tiles(buf_ref.at[slot, 0], TM)
               + info[:, 5:6] * _load_row_tiles(buf_ref.at[slot, 1], TM))
        out_ref[...] = _rms(out, g_ref[...]) if final else out

    @pl.when(i == 0)
    def _():
        issue(dest_ref, 0)

    for slot in range(2):
        @pl.when(lax.rem(i, 2) == slot)
        def _(slot=slot):
            @pl.when(i + 1 < pl.num_programs(0))
            def _():
                issue(next_ref, 1 - slot)

            finish(slot)


def _combine(dest, h, info, y, final_g=None, n_rows=None):
    final = final_g is not None
    n_rows = n_rows if final else h.shape[0]
    g = final_g if final else jnp.ones((1, D_MODEL), F32)
    rows = lambda i: (i, 0)
    nt = n_rows // TM
    return pl.pallas_call(
        functools.partial(_combine_kernel, final=final),
        grid=(nt,),
        in_specs=[pl.BlockSpec((1, 1, 2 * TM), lambda i: (i, 0, 0), memory_space=pltpu.SMEM),
                  pl.BlockSpec((1, 1, 2 * TM), lambda i: (jnp.minimum(i + 1, nt - 1), 0, 0),
                               memory_space=pltpu.SMEM),
                  pl.BlockSpec((TM, D_MODEL), rows), pl.BlockSpec((TM, LANES), rows),
                  pl.BlockSpec((1, D_MODEL), lambda i: (0, 0)), pl.BlockSpec(memory_space=pl.ANY)],
        out_specs=pl.BlockSpec((TM, D_MODEL), rows),
        out_shape=jax.ShapeDtypeStruct((n_rows, D_MODEL), F32),
        scratch_shapes=[pltpu.VMEM((2, 2, TM * ROW_TILE, LANES), F32), pltpu.SemaphoreType.DMA((2,))],
        compiler_params=_cparams(("arbitrary",)),
        name="combine",
    )(dest, dest, h, info, g, y)


def _moe_ffn(h, g, wr, wg, wu, wd, final_g=None, n_rows=None):
    R = h.shape[0]
    hn, info, counts = _route(h, g, wr)
    cnt = counts[0, :N_EXPERTS].astype(jnp.int32)
    padded = ((cnt + TME - 1) // TME) * TME
    ends = jnp.cumsum(padded)
    starts = ends - padded
    mt = (2 * R) // TME + N_EXPERTS
    tile_row = jnp.arange(mt, dtype=jnp.int32) * TME
    tile_expert = jnp.minimum(jnp.searchsorted(ends, tile_row, side="right"), N_EXPERTS - 1).astype(jnp.int32)
    tile_active = (tile_row < ends[-1]).astype(jnp.int32)
    e12 = info[:, 0:2].astype(jnp.int32)
    dest = ((starts[e12] + info[:, 2:4].astype(jnp.int32)) * ROW_TILE).reshape(R // TM, 1, 2 * TM)
    last_tiles = jnp.stack([jnp.maximum(ends // TME - 1, 0), (padded > 0).astype(jnp.int32),
                            jnp.broadcast_to(ends[-1] // TME, (N_EXPERTS,))]).astype(jnp.int32)
    xs = _scatter_rows(dest, last_tiles, hn, mt * TME)
    y = _experts(tile_expert, tile_active, xs, wg, wu, wd)
    return _combine(dest, h, info, y, final_g, n_rows)


def _final_kernel(h_ref, g_ref, out_ref):
    out_ref[...] = _rms(h_ref[...], g_ref[...])


def _final_norm(h, g, n_rows):
    rows = lambda i: (i, 0)
    return pl.pallas_call(
        _final_kernel,
        grid=(n_rows // TM,),
        in_specs=[pl.BlockSpec((TM, D_MODEL), rows), pl.BlockSpec((1, D_MODEL), lambda i: (0, 0))],
        out_specs=pl.BlockSpec((TM, D_MODEL), rows),
        out_shape=jax.ShapeDtypeStruct((n_rows, D_MODEL), F32),
        compiler_params=_cparams(("arbitrary",)),
        name="final_norm",
    )(h, g)


def _forget_column_placement():
    row = jnp.arange(3 * LANES)[:, None]
    col = jnp.arange(LANES)[None, :]
    term, head = row // LANES, row % LANES
    valid = head < FOX_HEADS
    plus = valid & (col == 6 * head + term)
    minus = valid & (col == 6 * head + 3 + term)
    return (plus.astype(F32) - minus.astype(F32)).astype(BF16)


def _rope_table(S):
    rd = DIFF_DH // 4
    inv = ROPE_THETA ** (-jnp.arange(0, rd, 2, dtype=F32) / rd)
    pos = jnp.concatenate([jnp.arange(N_META, N_META + S, dtype=F32), jnp.arange(TM, dtype=F32)])
    ang = pos[:, None] * inv[None, :]
    cos, sin = jnp.cos(ang), jnp.sin(ang)
    n = pos.shape[0]
    cos_t = jnp.tile(jnp.concatenate([cos, cos, jnp.ones((n, 48), F32)], axis=1), (1, 2))
    sin_lo = jnp.tile(jnp.concatenate([-sin, jnp.zeros((n, 56), F32)], axis=1), (1, 2))
    sin_hi = jnp.tile(jnp.concatenate([jnp.zeros((n, 8), F32), sin, jnp.zeros((n, 48), F32)], axis=1), (1, 2))
    return jnp.concatenate([cos_t, sin_lo, sin_hi], axis=1)


def kernel(x, meta_tokens, norm_mix_g, w_in, b_forget, diff_lambda, diff_subln_g, w_branch_diff, w_branch_fox,
           w_out, norm_ffn_g, ffn_w_gate, ffn_w_up, ffn_w_down, moe_router, moe_w_gate, moe_w_up, moe_w_down,
           final_norm_g):
    B, S, D = x.shape
    depth = w_in.shape[0]
    assert D == D_MODEL and S % TQ == 0 and meta_tokens.shape[0] == N_META
    nq = S // TM
    h = jnp.concatenate([x.reshape(B * S, D), meta_tokens.astype(x.dtype),
                         jnp.zeros((TQ - N_META, D), x.dtype)], axis=0)
    rope = _rope_table(S)
    g_final = final_norm_g.astype(F32).reshape(1, D)
    e6 = _forget_column_placement()
    scale = DIFF_DH ** -0.5 * LOG2E
    for layer in range(depth):
        lam_init = 0.8 - 0.6 * math.exp(-0.3 * layer)
        w = w_in[layer]
        dq, dk, dv, fq, fk, fv, ff, ga, gb = jnp.split(
            w, [HALF, 2 * HALF, 3 * HALF, 4 * HALF, 5 * HALF, 6 * HALF, 6 * HALF + FOX_HEADS,
                6 * HALF + FOX_HEADS + D_MODEL], axis=1)
        w1 = jnp.concatenate([jnp.pad(ff, ((0, 0), (0, LANES - FOX_HEADS))), dq * scale, dk], axis=1).astype(BF16)
        w2 = jnp.concatenate([dv, fv], axis=1).T.astype(BF16)
        w3 = jnp.concatenate([fq * scale, fk], axis=1).astype(BF16)
        bfp = jnp.pad(b_forget[layer].astype(F32), (0, LANES - FOX_HEADS)).reshape(1, LANES)
        g_mix = norm_mix_g[layer].astype(F32).reshape(1, D)
        dq_a, dk_a, vt_a, fq_a, fk_a = _inproj(h, g_mix, w1, w2, w3, e6, bfp, rope, nq=nq)

        lp = diff_lambda[layer].astype(F32)
        lam = jnp.exp(jnp.sum(lp[0] * lp[1])) - jnp.exp(jnp.sum(lp[2] * lp[3])) + lam_init
        par = jnp.zeros((8, LANES), F32)
        par = par.at[1].set(diff_subln_g[layer].astype(F32)).at[2].set(1.0 - lam_init).at[3].set(lam)
        o_a = _attention(dq_a, dk_a, vt_a, par, diff=True, nq=nq, nb=B)
        o_b = _attention(fq_a, fk_a, vt_a, None, diff=False, nq=nq, nb=B)

        wgate = jnp.concatenate([ga, gb], axis=1).astype(BF16)
        h = _mixout(h, o_a, o_b, g_mix, wgate, w_branch_diff[layer].astype(BF16),
                    w_branch_fox[layer].astype(BF16), w_out[layer].astype(BF16))

        g_ffn = norm_ffn_g[layer].astype(F32).reshape(1, D)
        jj = layer // 2
        if layer % 2 == 0:
            h = _dense_ffn(h, g_ffn, ffn_w_gate[jj].astype(BF16), ffn_w_up[jj].astype(BF16),
                           ffn_w_down[jj].astype(BF16))
        else:
            r_hi, r_mid, _ = _split3(jnp.pad(moe_router[jj].astype(F32), ((0, 0), (0, LANES - N_EXPERTS))))
            wr = jnp.stack([r_hi, r_mid]).astype(BF16)
            last = layer == depth - 1
            h = _moe_ffn(h, g_ffn, wr, moe_w_gate[jj].astype(BF16), moe_w_up[jj].astype(BF16),
                         moe_w_down[jj].astype(BF16), g_final if last else None, B * S)
    out = h if depth % 2 == 0 else _final_norm(h, g_final, B * S)
    return out.reshape(B, S, D)
```

```python
import functools
import math

import jax
import jax.numpy as jnp
from jax import lax
from jax.experimental import pallas as pl
from jax.experimental.pallas import tpu as pltpu

D_MODEL = 1024
N_META = 16
ROPE_THETA = 500000.0
RMS_EPS = 1e-6
NEG_INF = -1e30

DIFF_HEADS = 4
DIFF_DH = 64
FOX_HEADS = 8
FOX_DH = 64
HALF = 512
N_EXPERTS = 8
LANES = 128

TM = 512
KEY_TILES_PER_Q = 2
TQ = KEY_TILES_PER_Q * TM
TME = 512
TF_MOE = 1792
FF_CHUNK = 256
ROW_TILE = 8
DMA_UNROLL = 8
LOG2E = 1.4426950408889634
ONES_ROWS = 16
VMEM_LIMIT = 56 * 1024 * 1024

F32 = jnp.float32
BF16 = jnp.bfloat16


def _cparams(sem):
    return pltpu.CompilerParams(dimension_semantics=sem, vmem_limit_bytes=VMEM_LIMIT)


def _rms(x, g):
    ms = jnp.mean(x * x, axis=-1, keepdims=True)
    return x * lax.rsqrt(ms + RMS_EPS) * g


def _split3(x):
    hi = x.astype(BF16).astype(F32)
    r = x - hi
    mid = r.astype(BF16).astype(F32)
    lo = (r - mid).astype(BF16).astype(F32)
    return hi, mid, lo


def _tri_cumsum(x, inclusive):
    n = x.shape[0]
    row = lax.broadcasted_iota(jnp.int32, (n, n), 0)
    col = lax.broadcasted_iota(jnp.int32, (n, n), 1)
    tri = jnp.where((col <= row) if inclusive else (col < row), 1.0, 0.0).astype(BF16)
    parts = jnp.concatenate([p.astype(BF16) for p in _split3(x)], axis=1)
    out = jnp.dot(tri, parts, preferred_element_type=F32)
    return out[:, :LANES] + out[:, LANES:2 * LANES] + out[:, 2 * LANES:]


def _store_row_tiles(ref, x):
    rows = x.shape[0]
    for s in range(ROW_TILE):
        ref[pl.ds(s, rows, stride=ROW_TILE), :] = x[:, LANES * s:LANES * (s + 1)]


def _load_row_tiles(ref, rows):
    return jnp.concatenate([ref[pl.ds(s, rows, stride=ROW_TILE), :] for s in range(ROW_TILE)], axis=1)


def _inproj_kernel(h_ref, g_ref, w1_ref, w2_ref, w3_ref, e6_ref, bf_ref, rope_ref,
                   dq_ref, dk_ref, vt_ref, fq_ref, fk_ref, carry_ref, mcarry_ref, *, nq):
    i = pl.program_id(0)

    @pl.when(i == 0)
    def _():
        carry_ref[...] = jnp.zeros_like(carry_ref)
        mcarry_ref[...] = jnp.zeros_like(mcarry_ref)

    hb = _rms(h_ref[...], g_ref[...]).astype(BF16)

    z1 = jnp.dot(hb, w1_ref[...], preferred_element_type=F32)

    z4 = z1[:, :LANES] + bf_ref[...]
    lane = lax.broadcasted_iota(jnp.int32, z4.shape, 1)
    logf = jnp.minimum(z4, 0.0) - jnp.log1p(jnp.exp(-jnp.abs(z4)))
    logf = jnp.where(lane < FOX_HEADS, logf, 0.0)
    j_in_batch = lax.rem(jnp.maximum(i - 1, 0), nq)
    base = jnp.where(i == 0, 0.0, jnp.where(j_in_batch == 0, mcarry_ref[...], carry_ref[...]))
    c = _tri_cumsum(logf, inclusive=True) + base

    vt_ref[0] = lax.dot_general(w2_ref[...], hb, (((1,), (1,)), ((), ())),
                                preferred_element_type=F32).astype(BF16)

    parts = jnp.concatenate([p.astype(BF16) for p in _split3(c * LOG2E)], axis=1)
    c6 = jnp.dot(parts, e6_ref[...], preferred_element_type=F32)
    z3 = jnp.dot(hb, w3_ref[...], preferred_element_type=F32)

    cos_t = rope_ref[:, 0:LANES]
    sin_lo = rope_ref[:, LANES:2 * LANES]
    sin_hi = rope_ref[:, 2 * LANES:3 * LANES]
    for j in range(8):
        zj = z1[:, LANES * (j + 1):LANES * (j + 2)]
        rot = zj * cos_t + pltpu.roll(zj, LANES - 8, 1) * sin_lo + pltpu.roll(zj, 8, 1) * sin_hi
        dst = dq_ref if j < 4 else dk_ref
        dst[:, LANES * (j % 4):LANES * (j % 4 + 1)] = rot.astype(BF16)

    @pl.when(i == 0)
    def _():
        mcarry_ref[...] = c[N_META - 1:N_META, :]

    carry_ref[...] = c[TM - 1:TM, :]

    for hd in range(FOX_HEADS):
        off = 64 if hd % 2 == 0 else 0
        moved = pltpu.roll(c6, (off - 6 * hd) % LANES, 1)
        first3 = (lane >= off) & (lane < off + 3)
        last3 = (lane >= off + 3) & (lane < off + 6)
        aug_q = jnp.where(first3, moved, jnp.where(last3, 1.0, 0.0))
        aug_k = jnp.where(last3, moved, jnp.where(first3, 1.0, 0.0))
        slab = hd // 2
        keep = (lane < 64) if hd % 2 == 0 else (lane >= 64)
        zq = z3[:, LANES * slab:LANES * (slab + 1)]
        zk = z3[:, HALF + LANES * slab:HALF + LANES * (slab + 1)]
        fq_ref[:, LANES * hd:LANES * (hd + 1)] = jnp.where(keep, zq, aug_q).astype(BF16)
        fk_ref[:, LANES * hd:LANES * (hd + 1)] = jnp.where(keep, zk, aug_k).astype(BF16)


def _inproj(h, g, w1, w2, w3, e6, bfp, rope, *, nq):
    R = h.shape[0]
    nt = R // TM
    n_real = nt - TQ // TM
    rows = lambda i: (jnp.where(i == 0, n_real, jnp.where(i <= n_real, i - 1, i)), 0)
    rope_rows = lambda i: (jnp.where(i == 0, nq, lax.rem(jnp.maximum(i - 1, 0), nq)), 0)
    const = lambda i: (0, 0)
    out_sd = lambda w: jax.ShapeDtypeStruct((R, w), BF16)
    return pl.pallas_call(
        functools.partial(_inproj_kernel, nq=nq),
        grid=(nt,),
        in_specs=[
            pl.BlockSpec((TM, D_MODEL), rows),
            pl.BlockSpec((1, D_MODEL), const),
            pl.BlockSpec((D_MODEL, LANES + 2 * HALF), const),
            pl.BlockSpec((2 * HALF, D_MODEL), const),
            pl.BlockSpec((D_MODEL, 2 * HALF), const),
            pl.BlockSpec((3 * LANES, LANES), const),
            pl.BlockSpec((1, LANES), const),
            pl.BlockSpec((TM, 3 * LANES), rope_rows),
        ],
        out_specs=[
            pl.BlockSpec((TM, HALF), rows), pl.BlockSpec((TM, HALF), rows),
            pl.BlockSpec((1, 2 * HALF, TM), lambda i: (rows(i)[0], 0, 0)),
            pl.BlockSpec((TM, 2 * HALF), rows), pl.BlockSpec((TM, 2 * HALF), rows),
        ],
        out_shape=[out_sd(HALF), out_sd(HALF), jax.ShapeDtypeStruct((nt, 2 * HALF, TM), BF16),
                   out_sd(2 * HALF), out_sd(2 * HALF)],
        scratch_shapes=[pltpu.VMEM((1, LANES), F32), pltpu.VMEM((1, LANES), F32)],
        compiler_params=_cparams(("arbitrary",)),
        name="inproj",
    )(h, g, w1, w2, w3, e6, bfp, rope)


def _attn_kernel(*refs, diff, nq, nb):
    if diff:
        q_ref, k_ref, vt_ref, km_ref, vtm_ref, par_ref, o_ref, acc_ref, m_ref, q_scr, sa_ref, sb_ref = refs
    else:
        q_ref, k_ref, vt_ref, km_ref, vtm_ref, o_ref, acc_ref, m_ref, q_scr, sa_ref, sb_ref = refs
    t = pl.program_id(1)
    nqt = nq // KEY_TILES_PER_Q
    is_real = t < nb * nqt
    jq = lax.rem(t, nqt)
    dv = acc_ref.shape[1] - ONES_ROWS

    for sub in range(2):
        if diff:
            q = q_ref[...]
            lane = lax.broadcasted_iota(jnp.int32, q.shape, 1)
            q_scr[sub] = jnp.where((lane < 64) if sub == 0 else (lane >= 64), q, jnp.zeros_like(q))
        else:
            q_scr[sub] = q_ref[:, LANES * sub:LANES * (sub + 1)]
        m_ref[sub] = jnp.full(m_ref.shape[1:], NEG_INF, F32)
        acc_ref[sub] = jnp.zeros(acc_ref.shape[1:], F32)

    def keys_of(sub, k_tile):
        return k_tile if diff else k_tile[:, LANES * sub:LANES * (sub + 1)]

    def values_of(sub, vt_tile):
        ones = jnp.ones((ONES_ROWS, vt_tile.shape[1]), BF16)
        vt = vt_tile if diff else vt_tile[dv * sub:dv * (sub + 1), :]
        return jnp.concatenate([vt, ones], axis=0)

    def scores(sub, k, q0, q1=TQ):
        return lax.dot_general(k, q_scr[sub, q0:q1, :], (((1,), (1,)), ((), ())), preferred_element_type=F32)

    def update(sub, st, vt, mask, q0, q1=TQ):
        if mask is not None:
            st = jnp.where(mask, st, NEG_INF)
        m_prev = m_ref[sub, :, q0:q1]
        m_new = jnp.maximum(m_prev, jnp.max(st, axis=0, keepdims=True))
        p = jnp.exp2(st - m_new).astype(BF16)
        if p.shape[0] < vt.shape[1]:
            p = jnp.concatenate([p, jnp.zeros((vt.shape[1] - p.shape[0], p.shape[1]), BF16)], axis=0)
        acc_ref[sub, :, q0:q1] = (jnp.exp2(m_prev - m_new) * acc_ref[sub, :, q0:q1]
                                  + jnp.dot(vt, p, preferred_element_type=F32))
        m_ref[sub, :, q0:q1] = m_new

    def scores_into(buf, tile, q0, subs=(0, 1), q1=TQ):
        k_tile = k_ref[pl.ds(pl.multiple_of(tile * TM, TM), TM), :]
        for sub in subs:
            buf[sub, :, q0:q1] = scores(sub, keys_of(sub, k_tile), q0, q1)

    def update_from(buf, tile, causal, q0, subs=(0, 1), q1=TQ):
        vt_tile = vt_ref[tile]
        mask = None
        if causal:
            key = lax.broadcasted_iota(jnp.int32, (TM, q1 - q0), 0)
            qry = lax.broadcasted_iota(jnp.int32, (TM, q1 - q0), 1)
            mask = key <= qry
        for sub in subs:
            update(sub, buf[sub, :, q0:q1], values_of(sub, vt_tile), mask, q0, q1)

    key = lax.broadcasted_iota(jnp.int32, (N_META, TQ), 0)
    qry = lax.broadcasted_iota(jnp.int32, (N_META, TQ), 1)
    meta_mask = key <= jnp.where(is_real, N_META - 1, qry)
    km_tile = km_ref[...]
    vtm_tile = vtm_ref[0][:, :LANES]

    for sub in range(2):
        meta_scores = scores(sub, keys_of(sub, km_tile), 0)
        scores_into(sa_ref, 0, 0, (sub,))
        update(sub, meta_scores, values_of(sub, vtm_tile), meta_mask, 0)

    def pair(i, carry):
        for cur, nxt, tile in ((sa_ref, sb_ref, 2 * i), (sb_ref, sa_ref, 2 * i + 1)):
            for sub in range(2):
                for q0 in range(0, TQ, TM):
                    scores_into(nxt, tile + 1, q0, (sub,), q0 + TM)
                    update_from(cur, tile, False, q0, (sub,), q0 + TM)
        return carry

    lax.fori_loop(0, jnp.where(is_real, jq, 0), pair, 0)

    @pl.when(is_real)
    def _():
        for sub in range(2):
            scores_into(sb_ref, 2 * jq + 1, TM, (sub,))
            update_from(sa_ref, 2 * jq, True, 0, (sub,), TM)
            update_from(sa_ref, 2 * jq, False, TM, (sub,))
        update_from(sb_ref, 2 * jq + 1, True, TM)

    a0 = acc_ref[0]
    a1 = acc_ref[1]
    o0 = a0[:dv] / a0[dv:dv + 1]
    o1 = a1[:dv] / a1[dv:dv + 1]
    if diff:
        d = (o0 - par_ref[3:4, 0:1] * o1).T
        o_ref[...] = (_rms(d, par_ref[1:2, :]) * par_ref[2:3, :]).astype(BF16)
    else:
        o_ref[...] = jnp.concatenate([o0, o1], axis=0).T.astype(BF16)


def _attention(q, k, vt, par, *, diff, nq, nb):
    R = q.shape[0]
    S = nq * TM
    ntq = R // TQ
    qw = LANES if diff else 2 * LANES
    voff = 0 if diff else HALF // LANES
    meta_blk = (nb * S) // N_META
    batch_of = lambda t: jnp.minimum(t // (nq // KEY_TILES_PER_Q), nb - 1)
    in_specs = [
        pl.BlockSpec((TQ, qw), lambda p, t: (t, p)),
        pl.BlockSpec((S, qw), lambda p, t: (batch_of(t), p)),
        pl.BlockSpec((nq, LANES, TM), lambda p, t: (batch_of(t), p + voff, 0)),
        pl.BlockSpec((N_META, qw), lambda p, t: (meta_blk, p)),
        pl.BlockSpec((1, LANES, TM), lambda p, t: (nb * nq, p + voff, 0)),
    ]
    args = [q, k, vt, k, vt]
    if diff:
        in_specs.append(pl.BlockSpec((8, LANES), lambda p, t: (0, 0)))
        args.append(par)
    acc_rows = (LANES if diff else LANES // 2) + ONES_ROWS
    return pl.pallas_call(
        functools.partial(_attn_kernel, diff=diff, nq=nq, nb=nb),
        grid=(4, ntq),
        in_specs=in_specs,
        out_specs=pl.BlockSpec((TQ, LANES), lambda p, t: (t, p)),
        out_shape=jax.ShapeDtypeStruct((R, HALF), BF16),
        scratch_shapes=[pltpu.VMEM((2, acc_rows, TQ), F32), pltpu.VMEM((2, 1, TQ), F32),
                        pltpu.VMEM((2, TQ, LANES), BF16),
                        pltpu.VMEM((2, TM, TQ), F32), pltpu.VMEM((2, TM, TQ), F32)],
        compiler_params=_cparams(("arbitrary", "arbitrary")),
        name="diff_attn" if diff else "fox_attn",
    )(*args)


def _mixout_kernel(h_ref, oa_ref, ob_ref, g_ref, wg_ref, wbd_ref, wbf_ref, wo_ref, out_ref):
    x = h_ref[...]
    hb = _rms(x, g_ref[...]).astype(BF16)
    gates = jax.nn.sigmoid(jnp.dot(hb, wg_ref[...], preferred_element_type=F32))
    a = jnp.dot(oa_ref[...], wbd_ref[...], preferred_element_type=F32)
    b = jnp.dot(ob_ref[...], wbf_ref[...], preferred_element_type=F32)
    merged = gates[:, :D_MODEL] * a + gates[:, D_MODEL:] * b
    out_ref[...] = x + jnp.dot(merged.astype(BF16), wo_ref[...], preferred_element_type=F32)


def _mixout(h, oa, ob, g, wg, wbd, wbf, wo):
    R = h.shape[0]
    rows = lambda i: (i, 0)
    const = lambda i: (0, 0)
    return pl.pallas_call(
        _mixout_kernel,
        grid=(R // TM,),
        in_specs=[
            pl.BlockSpec((TM, D_MODEL), rows), pl.BlockSpec((TM, HALF), rows), pl.BlockSpec((TM, HALF), rows),
            pl.BlockSpec((1, D_MODEL), const), pl.BlockSpec((D_MODEL, 2 * D_MODEL), const),
            pl.BlockSpec((HALF, D_MODEL), const), pl.BlockSpec((HALF, D_MODEL), const),
            pl.BlockSpec((D_MODEL, D_MODEL), const),
        ],
        out_specs=pl.BlockSpec((TM, D_MODEL), rows),
        out_shape=jax.ShapeDtypeStruct((R, D_MODEL), F32),
        compiler_params=_cparams(("arbitrary",)),
        name="mixout",
    )(h, oa, ob, g, wg, wbd, wbf, wo)


def _swiglu_acc(xb, wg_ref, wu_ref, wd_ref, acc):
    def gate_up(c):
        sl = slice(FF_CHUNK * c, FF_CHUNK * (c + 1))
        return (jnp.dot(xb, wg_ref[:, sl], preferred_element_type=F32),
                jnp.dot(xb, wu_ref[:, sl], preferred_element_type=F32))

    nf = wg_ref.shape[1] // FF_CHUNK
    nxt = gate_up(0)
    for c in range(nf):
        gate, up = nxt
        if c + 1 < nf:
            nxt = gate_up(c + 1)
        mid = (gate * jax.nn.sigmoid(gate) * up).astype(BF16)
        acc = acc + jnp.dot(mid, wd_ref[FF_CHUNK * c:FF_CHUNK * (c + 1), :], preferred_element_type=F32)
    return acc


def _dense_ffn_kernel(h_ref, g_ref, wg_ref, wu_ref, wd_ref, out_ref):
    x = h_ref[...]
    hb = _rms(x, g_ref[...]).astype(BF16)
    out_ref[...] = _swiglu_acc(hb, wg_ref, wu_ref, wd_ref, x)


def _dense_ffn(h, g, wg, wu, wd):
    R = h.shape[0]
    dff = wg.shape[1]
    rows = lambda i: (i, 0)
    const = lambda i: (0, 0)
    return pl.pallas_call(
        _dense_ffn_kernel,
        grid=(R // TM,),
        in_specs=[
            pl.BlockSpec((TM, D_MODEL), rows), pl.BlockSpec((1, D_MODEL), const),
            pl.BlockSpec((D_MODEL, dff), const), pl.BlockSpec((D_MODEL, dff), const),
            pl.BlockSpec((dff, D_MODEL), const),
        ],
        out_specs=pl.BlockSpec((TM, D_MODEL), rows),
        out_shape=jax.ShapeDtypeStruct((R, D_MODEL), F32),
        compiler_params=_cparams(("arbitrary",)),
        name="dense_ffn",
    )(h, g, wg, wu, wd)


def _route_kernel(h_ref, g_ref, wr_ref, hn_ref, info_ref, cnt_ref, carry_ref):
    i = pl.program_id(0)

    @pl.when(i == 0)
    def _():
        carry_ref[...] = jnp.zeros_like(carry_ref)

    hn = _rms(h_ref[...], g_ref[...])
    _store_row_tiles(hn_ref, hn)
    h_hi, h_mid, _ = _split3(hn)
    hi_both = jnp.dot(h_hi.astype(BF16), wr_ref[...], preferred_element_type=F32)
    logits = (hi_both[:, :LANES] + jnp.dot(h_mid.astype(BF16), wr_ref[:, :LANES], preferred_element_type=F32)
              + hi_both[:, LANES:])
    lane = lax.broadcasted_iota(jnp.int32, logits.shape, 1)
    logits = jnp.where(lane < N_EXPERTS, logits, -jnp.inf)
    v1 = jnp.max(logits, axis=-1, keepdims=True)
    e1 = jnp.min(jnp.where(logits == v1, lane, LANES), axis=-1, keepdims=True)
    rest = jnp.where(lane == e1, -jnp.inf, logits)
    v2 = jnp.max(rest, axis=-1, keepdims=True)
    e2 = jnp.min(jnp.where(rest == v2, lane, LANES), axis=-1, keepdims=True)
    ex = jnp.exp(v2 - v1)
    w1 = 1.0 / (1.0 + ex)
    w2 = ex / (1.0 + ex)
    hot1 = jnp.where(lane == e1, 1.0, 0.0)
    hot2 = jnp.where(lane == e2, 1.0, 0.0)
    hot = hot1 + hot2
    before = _tri_cumsum(hot, inclusive=False) + carry_ref[...]
    r1 = jnp.sum(before * hot1, axis=-1, keepdims=True)
    r2 = jnp.sum(before * hot2, axis=-1, keepdims=True)
    total = before[TM - 1:TM, :] + hot[TM - 1:TM, :]
    carry_ref[...] = total
    cnt_ref[...] = jnp.broadcast_to(total, cnt_ref.shape)
    info_ref[...] = jnp.where(lane == 0, e1.astype(F32), jnp.where(lane == 1, e2.astype(F32),
                              jnp.where(lane == 2, r1, jnp.where(lane == 3, r2,
                                        jnp.where(lane == 4, w1, jnp.where(lane == 5, w2, 0.0))))))


def _route(h, g, wr):
    R = h.shape[0]
    rows = lambda i: (i, 0)
    return pl.pallas_call(
        _route_kernel,
        grid=(R // TM,),
        in_specs=[pl.BlockSpec((TM, D_MODEL), rows), pl.BlockSpec((1, D_MODEL), lambda i: (0, 0)),
                  pl.BlockSpec((D_MODEL, 2 * LANES), lambda i: (0, 0))],
        out_specs=[pl.BlockSpec((TM * ROW_TILE, LANES), rows), pl.BlockSpec((TM, LANES), rows),
                   pl.BlockSpec((8, LANES), lambda i: (0, 0))],
        out_shape=[jax.ShapeDtypeStruct((R * ROW_TILE, LANES), F32), jax.ShapeDtypeStruct((R, LANES), F32),
                   jax.ShapeDtypeStruct((8, LANES), F32)],
        scratch_shapes=[pltpu.VMEM((1, LANES), F32)],
        compiler_params=_cparams(("arbitrary",)),
        name="route",
    )(h, g, wr)


def _scatter_kernel(dest_ref, last_ref, src_ref, out_ref, zero_ref, sem):
    @pl.when(pl.program_id(0) == 0)
    def _():
        zero_ref[...] = jnp.zeros_like(zero_ref)

        def zero_copy(e):
            row = pl.multiple_of(last_ref[0, e] * (TME * ROW_TILE), TME * ROW_TILE)
            return pltpu.make_async_copy(zero_ref, out_ref.at[pl.ds(row, TME * ROW_TILE)], sem)

        for e in range(N_EXPERTS):
            @pl.when(last_ref[1, e] > 0)
            def _(e=e):
                zero_copy(e).start()
        for e in range(N_EXPERTS):
            @pl.when(last_ref[1, e] > 0)
            def _(e=e):
                zero_copy(e).wait()

        def spare_copy(tile):
            row = pl.multiple_of(tile * (TME * ROW_TILE), TME * ROW_TILE)
            return pltpu.make_async_copy(zero_ref, out_ref.at[pl.ds(row, TME * ROW_TILE)], sem)

        def start_spare(tile, c):
            spare_copy(tile).start()
            return c

        def wait_spare(tile, c):
            spare_copy(tile).wait()
            return c

        n_tiles = out_ref.shape[0] // (TME * ROW_TILE)
        lax.fori_loop(last_ref[2, 0], n_tiles, start_spare, 0)
        lax.fori_loop(last_ref[2, 0], n_tiles, wait_spare, 0)

    def copy(r, k):
        return pltpu.make_async_copy(
            src_ref.at[pl.ds(pl.multiple_of(r * ROW_TILE, ROW_TILE), ROW_TILE)],
            out_ref.at[pl.ds(pl.multiple_of(dest_ref[0, 0, 2 * r + k], ROW_TILE), ROW_TILE)], sem)

    def issue(r, c):
        copy(r, 0).start(priority=0)
        copy(r, 1).start(priority=1)
        return c

    lax.fori_loop(0, TM, issue, 0, unroll=DMA_UNROLL)
    for _ in range(2):
        pltpu.make_async_copy(src_ref, out_ref.at[pl.ds(0, TM * ROW_TILE)], sem).wait()


def _scatter_rows(dest, last_tiles, src, n_rows):
    R = src.shape[0] // ROW_TILE
    return pl.pallas_call(
        _scatter_kernel,
        grid=(R // TM,),
        in_specs=[pl.BlockSpec((1, 1, 2 * TM), lambda i: (i, 0, 0), memory_space=pltpu.SMEM),
                  pl.BlockSpec(memory_space=pltpu.SMEM),
                  pl.BlockSpec((TM * ROW_TILE, LANES), lambda i: (i, 0))],
        out_specs=pl.BlockSpec(memory_space=pl.ANY),
        out_shape=jax.ShapeDtypeStruct((n_rows * ROW_TILE, LANES), src.dtype),
        scratch_shapes=[pltpu.VMEM((TME * ROW_TILE, LANES), F32), pltpu.SemaphoreType.DMA(())],
        compiler_params=pltpu.CompilerParams(dimension_semantics=("arbitrary",), vmem_limit_bytes=VMEM_LIMIT,
                                             has_side_effects=True),
        name="scatter_rows",
    )(dest, last_tiles, src)


def _expert_kernel(te_ref, act_ref, x_ref, wg_ref, wu_ref, wd_ref, y_ref, acc_ref):
    i = pl.program_id(0)
    f = pl.program_id(1)
    active = act_ref[i] > 0
    del te_ref

    def step(first):
        xb = _load_row_tiles(x_ref, TME).astype(BF16)
        prev = jnp.zeros(acc_ref.shape, F32) if first else acc_ref[...]
        acc = _swiglu_acc(xb, wg_ref, wu_ref, wd_ref, prev)
        acc_ref[...] = acc
        _store_row_tiles(y_ref, acc)

    @pl.when(active & (f == 0))
    def _():
        step(True)

    @pl.when(active & (f > 0))
    def _():
        step(False)

    @pl.when(jnp.logical_not(active))
    def _():
        y_ref[...] = jnp.zeros_like(y_ref)


def _experts(tile_expert, tile_active, xs, wg, wu, wd):
    mt = tile_expert.shape[0]
    dffe = wg.shape[2]
    grid_spec = pltpu.PrefetchScalarGridSpec(
        num_scalar_prefetch=2,
        grid=(mt, dffe // TF_MOE),
        in_specs=[
            pl.BlockSpec((TME * ROW_TILE, LANES), lambda i, f, te, act: (i, 0)),
            pl.BlockSpec((None, D_MODEL, TF_MOE), lambda i, f, te, act: (te[i], 0, f)),
            pl.BlockSpec((None, D_MODEL, TF_MOE), lambda i, f, te, act: (te[i], 0, f)),
            pl.BlockSpec((None, TF_MOE, D_MODEL), lambda i, f, te, act: (te[i], f, 0)),
        ],
        out_specs=pl.BlockSpec((TME * ROW_TILE, LANES), lambda i, f, te, act: (i, 0)),
        scratch_shapes=[pltpu.VMEM((TME, D_MODEL), F32)],
    )
    return pl.pallas_call(
        _expert_kernel,
        grid_spec=grid_spec,
        out_shape=jax.ShapeDtypeStruct((mt * TME * ROW_TILE, LANES), F32),
        compiler_params=_cparams(("arbitrary", "arbitrary")),
        name="experts",
    )(tile_expert, tile_active, xs, wg, wu, wd)


def _combine_kernel(dest_ref, next_ref, h_ref, info_ref, g_ref, y_ref, out_ref, buf_ref, sem, *, final):
    i = pl.program_id(0)

    def issue(idx_ref, slot):
        def copy(r, k):
            return pltpu.make_async_copy(
                y_ref.at[pl.ds(pl.multiple_of(idx_ref[0, 0, 2 * r + k], ROW_TILE), ROW_TILE)],
                buf_ref.at[slot, k, pl.ds(pl.multiple_of(r * ROW_TILE, ROW_TILE), ROW_TILE)], sem.at[slot])

        def body(r, c):
            copy(r, 0).start(priority=0)
            copy(r, 1).start(priority=1)
            return c

        lax.fori_loop(0, TM, body, 0, unroll=DMA_UNROLL)

    def finish(slot):
        for k in range(2):
            pltpu.make_async_copy(y_ref.at[pl.ds(0, TM * ROW_TILE)], buf_ref.at[slot, k], sem.at[slot]).wait()
        info = info_ref[...]
        out = (h_ref[...] + info[:, 4:5] * _load_row_tiles(buf_ref.at[slot, 0], TM)
               + info[:, 5:6] * _load_row_tiles(buf_ref.at[slot, 1], TM))
        out_ref[...] = _rms(out, g_ref[...]) if final else out

    @pl.when(i == 0)
    def _():
        issue(dest_ref, 0)

    for slot in range(2):
        @pl.when(lax.rem(i, 2) == slot)
        def _(slot=slot):
            @pl.when(i + 1 < pl.num_programs(0))
            def _():
                issue(next_ref, 1 - slot)

            finish(slot)


def _combine(dest, h, info, y, final_g=None, n_rows=None):
    final = final_g is not None
    n_rows = n_rows if final else h.shape[0]
    g = final_g if final else jnp.ones((1, D_MODEL), F32)
    rows = lambda i: (i, 0)
    nt = n_rows // TM
    return pl.pallas_call(
        functools.partial(_combine_kernel, final=final),
        grid=(nt,),
        in_specs=[pl.BlockSpec((1, 1, 2 * TM), lambda i: (i, 0, 0), memory_space=pltpu.SMEM),
                  pl.BlockSpec((1, 1, 2 * TM), lambda i: (jnp.minimum(i + 1, nt - 1), 0, 0),
                               memory_space=pltpu.SMEM),
                  pl.BlockSpec((TM, D_MODEL), rows), pl.BlockSpec((TM, LANES), rows),
                  pl.BlockSpec((1, D_MODEL), lambda i: (0, 0)), pl.BlockSpec(memory_space=pl.ANY)],
        out_specs=pl.BlockSpec((TM, D_MODEL), rows),
        out_shape=jax.ShapeDtypeStruct((n_rows, D_MODEL), F32),
        scratch_shapes=[pltpu.VMEM((2, 2, TM * ROW_TILE, LANES), F32), pltpu.SemaphoreType.DMA((2,))],
        compiler_params=_cparams(("arbitrary",)),
        name="combine",
    )(dest, dest, h, info, g, y)


def _moe_ffn(h, g, wr, wg, wu, wd, final_g=None, n_rows=None):
    R = h.shape[0]
    hn, info, counts = _route(h, g, wr)
    cnt = counts[0, :N_EXPERTS].astype(jnp.int32)
    padded = ((cnt + TME - 1) // TME) * TME
    ends = jnp.cumsum(padded)
    starts = ends - padded
    mt = (2 * R) // TME + N_EXPERTS
    tile_row = jnp.arange(mt, dtype=jnp.int32) * TME
    tile_expert = jnp.minimum(jnp.searchsorted(ends, tile_row, side="right"), N_EXPERTS - 1).astype(jnp.int32)
    tile_active = (tile_row < ends[-1]).astype(jnp.int32)
    e12 = info[:, 0:2].astype(jnp.int32)
    dest = ((starts[e12] + info[:, 2:4].astype(jnp.int32)) * ROW_TILE).reshape(R // TM, 1, 2 * TM)
    last_tiles = jnp.stack([jnp.maximum(ends // TME - 1, 0), (padded > 0).astype(jnp.int32),
                            jnp.broadcast_to(ends[-1] // TME, (N_EXPERTS,))]).astype(jnp.int32)
    xs = _scatter_rows(dest, last_tiles, hn, mt * TME)
    y = _experts(tile_expert, tile_active, xs, wg, wu, wd)
    return _combine(dest, h, info, y, final_g, n_rows)


def _final_kernel(h_ref, g_ref, out_ref):
    out_ref[...] = _rms(h_ref[...], g_ref[...])


def _final_norm(h, g, n_rows):
    rows = lambda i: (i, 0)
    return pl.pallas_call(
        _final_kernel,
        grid=(n_rows // TM,),
        in_specs=[pl.BlockSpec((TM, D_MODEL), rows), pl.BlockSpec((1, D_MODEL), lambda i: (0, 0))],
        out_specs=pl.BlockSpec((TM, D_MODEL), rows),
        out_shape=jax.ShapeDtypeStruct((n_rows, D_MODEL), F32),
        compiler_params=_cparams(("arbitrary",)),
        name="final_norm",
    )(h, g)


def _forget_column_placement():
    row = jnp.arange(3 * LANES)[:, None]
    col = jnp.arange(LANES)[None, :]
    term, head = row // LANES, row % LANES
    valid = head < FOX_HEADS
    plus = valid & (col == 6 * head + term)
    minus = valid & (col == 6 * head + 3 + term)
    return (plus.astype(F32) - minus.astype(F32)).astype(BF16)


def _rope_table(S):
    rd = DIFF_DH // 4
    inv = ROPE_THETA ** (-jnp.arange(0, rd, 2, dtype=F32) / rd)
    pos = jnp.concatenate([jnp.arange(N_META, N_META + S, dtype=F32), jnp.arange(TM, dtype=F32)])
    ang = pos[:, None] * inv[None, :]
    cos, sin = jnp.cos(ang), jnp.sin(ang)
    n = pos.shape[0]
    cos_t = jnp.tile(jnp.concatenate([cos, cos, jnp.ones((n, 48), F32)], axis=1), (1, 2))
    sin_lo = jnp.tile(jnp.concatenate([-sin, jnp.zeros((n, 56), F32)], axis=1), (1, 2))
    sin_hi = jnp.tile(jnp.concatenate([jnp.zeros((n, 8), F32), sin, jnp.zeros((n, 48), F32)], axis=1), (1, 2))
    return jnp.concatenate([cos_t, sin_lo, sin_hi], axis=1)


def kernel(x, meta_tokens, norm_mix_g, w_in, b_forget, diff_lambda, diff_subln_g, w_branch_diff, w_branch_fox,
           w_out, norm_ffn_g, ffn_w_gate, ffn_w_up, ffn_w_down, moe_router, moe_w_gate, moe_w_up, moe_w_down,
           final_norm_g):
    B, S, D = x.shape
    depth = w_in.shape[0]
    assert D == D_MODEL and S % TQ == 0 and meta_tokens.shape[0] == N_META
    nq = S // TM
    h = jnp.concatenate([x.reshape(B * S, D), meta_tokens.astype(x.dtype),
                         jnp.zeros((TQ - N_META, D), x.dtype)], axis=0)
    rope = _rope_table(S)
    g_final = final_norm_g.astype(F32).reshape(1, D)
    e6 = _forget_column_placement()
    scale = DIFF_DH ** -0.5 * LOG2E
    for layer in range(depth):
        lam_init = 0.8 - 0.6 * math.exp(-0.3 * layer)
        w = w_in[layer]
        dq, dk, dv, fq, fk, fv, ff, ga, gb = jnp.split(
            w, [HALF, 2 * HALF, 3 * HALF, 4 * HALF, 5 * HALF, 6 * HALF, 6 * HALF + FOX_HEADS,
                6 * HALF + FOX_HEADS + D_MODEL], axis=1)
        w1 = jnp.concatenate([jnp.pad(ff, ((0, 0), (0, LANES - FOX_HEADS))), dq * scale, dk], axis=1).astype(BF16)
        w2 = jnp.concatenate([dv, fv], axis=1).T.astype(BF16)
        w3 = jnp.concatenate([fq * scale, fk], axis=1).astype(BF16)
        bfp = jnp.pad(b_forget[layer].astype(F32), (0, LANES - FOX_HEADS)).reshape(1, LANES)
        g_mix = norm_mix_g[layer].astype(F32).reshape(1, D)
        dq_a, dk_a, vt_a, fq_a, fk_a = _inproj(h, g_mix, w1, w2, w3, e6, bfp, rope, nq=nq)

        lp = diff_lambda[layer].astype(F32)
        lam = jnp.exp(jnp.sum(lp[0] * lp[1])) - jnp.exp(jnp.sum(lp[2] * lp[3])) + lam_init
        par = jnp.zeros((8, LANES), F32)
        par = par.at[1].set(diff_subln_g[layer].astype(F32)).at[2].set(1.0 - lam_init).at[3].set(lam)
        o_a = _attention(dq_a, dk_a, vt_a, par, diff=True, nq=nq, nb=B)
        o_b = _attention(fq_a, fk_a, vt_a, None, diff=False, nq=nq, nb=B)

        wgate = jnp.concatenate([ga, gb], axis=1).astype(BF16)
        h = _mixout(h, o_a, o_b, g_mix, wgate, w_branch_diff[layer].astype(BF16),
                    w_branch_fox[layer].astype(BF16), w_out[layer].astype(BF16))

        g_ffn = norm_ffn_g[layer].astype(F32).reshape(1, D)
        jj = layer // 2
        if layer % 2 == 0:
            h = _dense_ffn(h, g_ffn, ffn_w_gate[jj].astype(BF16), ffn_w_up[jj].astype(BF16),
                           ffn_w_down[jj].astype(BF16))
        else:
            r_hi, r_mid, _ = _split3(jnp.pad(moe_router[jj].astype(F32), ((0, 0), (0, LANES - N_EXPERTS))))
            wr = jnp.concatenate([r_hi, r_mid], axis=1).astype(BF16)
            last = layer == depth - 1
            h = _moe_ffn(h, g_ffn, wr, moe_w_gate[jj].astype(BF16), moe_w_up[jj].astype(BF16),
                         moe_w_down[jj].astype(BF16), g_final if last else None, B * S)
    out = h if depth % 2 == 0 else _final_norm(h, g_final, B * S)
    return out.reshape(B, S, D)
```
